```python
import jax, jax.numpy as jnp
from jax import lax
import numpy as np

D_MODEL = 1024
BATCH = 4
SEQ = 4096
DEPTH = 2

CHUNK = 64
SGU_GROUPS = 4
SGU_GROUP_DIM = 64
SGU_WIDTH = SGU_GROUPS * SGU_GROUP_DIM
SGU_BLOCK = 128
RET_HEADS = 4
RET_KDIM = 64
RET_VDIM = 64
POOL_WINDOWS = (2, 4, 8, 16)
POOL_GROUP_DIM = 64
POOL_WIDTH = 4 * POOL_GROUP_DIM
MLA_HEADS = 4
MLA_Q_RANK = 256
MLA_KV_RANK = 128
MLA_NOPE_DIM = 64
MLA_ROPE_DIM = 32
MLA_V_DIM = 64
Q_BLOCK = 128
ROPE_BASE = 10000.0
N_BRANCH = 4
BRANCH_WIDTH = 256
D_FF = 2816
PLE_DIM = 256
ALPHA = (2 * DEPTH) ** 0.25
BETA = (8 * DEPTH) ** -0.25
LN_EPS = 1e-5
RMS_EPS = 1e-6
GN_EPS = 1e-5
IN_SIZES = (SGU_WIDTH, SGU_WIDTH,
            RET_HEADS * RET_KDIM, RET_HEADS * RET_KDIM, RET_HEADS * RET_VDIM, RET_HEADS * RET_VDIM,
            POOL_WIDTH,
            MLA_Q_RANK, MLA_KV_RANK, MLA_ROPE_DIM,
            N_BRANCH * D_MODEL)
IN_COLS = sum(IN_SIZES)

kernel_name = 'hybrid_gated_streaming_block'


def _split(h, sizes):
    parts, start = [], 0
    for size in sizes:
        parts.append(h[..., start:start + size])
        start += size
    return parts


def layer_norm(x, g, b):
    xf = x.astype(jnp.float32)
    mu = jnp.mean(xf, axis=-1, keepdims=True)
    var = jnp.mean(jnp.square(xf - mu), axis=-1, keepdims=True)
    y = (xf - mu) * lax.rsqrt(var + LN_EPS) * g.astype(jnp.float32) + b.astype(jnp.float32)
    return y.astype(x.dtype)


def rms_norm(x, g):
    xf = x.astype(jnp.float32)
    y = xf * lax.rsqrt(jnp.mean(jnp.square(xf), axis=-1, keepdims=True) + RMS_EPS) * g.astype(jnp.float32)
    return y.astype(x.dtype)


def rope_tables(positions, dim):
    inv_freq = ROPE_BASE ** (-jnp.arange(0, dim, 2, dtype=jnp.float32) / dim)
    ang = positions.astype(jnp.float32)[..., None] * inv_freq
    return jnp.cos(ang), jnp.sin(ang)


def apply_rope(x, cos, sin):
    half = x.shape[-1] // 2
    xf = x.astype(jnp.float32)
    x1, x2 = xf[..., :half], xf[..., half:]
    c, s = cos[:, :, None, :], sin[:, :, None, :]
    return jnp.concatenate([x1 * c - x2 * s, x1 * s + x2 * c], axis=-1).astype(x.dtype)


def swiglu(x, w_up, w_down):
    a, b = jnp.split(x @ w_up, 2, axis=-1)
    return (jax.nn.silu(a) * b) @ w_down


def sgu_mixer(u, v, ln_g, ln_b, w_s, b_s):
    bsz, seq, _ = u.shape
    u = jax.nn.gelu(u)
    v = layer_norm(jax.nn.gelu(v), ln_g, ln_b)
    vb = v.reshape(bsz, seq // SGU_BLOCK, SGU_BLOCK, SGU_GROUPS, SGU_GROUP_DIM)
    causal = jnp.tril(jnp.ones((SGU_BLOCK, SGU_BLOCK), dtype=bool))
    w = jnp.where(causal[None], w_s, jnp.zeros_like(w_s))
    mixed = jnp.einsum('gts,bnsgc->bntgc', w, vb) + b_s.T[None, None, :, :, None]
    return u * mixed.reshape(bsz, seq, SGU_WIDTH)


def retention_mixer(q, k, v, g, cos, sin):
    bsz, seq, _ = q.shape
    n_chunks = seq // CHUNK
    f32 = jnp.float32
    q = apply_rope(q.reshape(bsz, seq, RET_HEADS, RET_KDIM), cos, sin).astype(f32)
    k = apply_rope(k.reshape(bsz, seq, RET_HEADS, RET_KDIM), cos, sin).astype(f32) * RET_KDIM ** -0.5
    v = v.reshape(bsz, seq, RET_HEADS, RET_VDIM).astype(f32)
    log_gamma = jnp.log1p(-jnp.exp2(-5.0 - jnp.arange(RET_HEADS, dtype=f32)))
    pos = jnp.arange(CHUNK, dtype=f32)
    intra_decay = jnp.exp(log_gamma[:, None, None] * jnp.abs(pos[:, None] - pos[None, :]))
    key_to_end = jnp.exp(log_gamma[:, None] * (CHUNK - 1 - pos)[None, :])
    start_to_query = jnp.exp(log_gamma[:, None] * (pos + 1)[None, :])
    chunk_decay = jnp.exp(log_gamma * CHUNK)
    qc = q.reshape(bsz, n_chunks, CHUNK, RET_HEADS, RET_KDIM)
    kc = k.reshape(bsz, n_chunks, CHUNK, RET_HEADS, RET_KDIM)
    vc = v.reshape(bsz, n_chunks, CHUNK, RET_HEADS, RET_VDIM)
    scores = jnp.einsum('bnihd,bnjhd->bnhij', qc, kc) * intra_decay
    y = jnp.einsum('bnhij,bnjhe->bnihe', scores, vc)
    kv = jnp.einsum('bnjhd,hj,bnjhe->nbhde', kc, key_to_end, vc)

    def step(state, kv_chunk):
        return state * chunk_decay[None, :, None, None] + kv_chunk, state

    _, prev = lax.scan(step, jnp.zeros((bsz, RET_HEADS, RET_KDIM, RET_VDIM), f32), kv)
    y = y + jnp.einsum('bnihd,nbhde,hi->bnihe', qc, prev, start_to_query)
    y = y.reshape(bsz, seq, RET_HEADS, RET_VDIM)
    mu = jnp.mean(y, axis=-1, keepdims=True)
    var = jnp.mean(jnp.square(y - mu), axis=-1, keepdims=True)
    y = (y - mu) * lax.rsqrt(var + GN_EPS)
    return (jax.nn.silu(g.astype(f32)) * y.reshape(bsz, seq, RET_HEADS * RET_VDIM)).astype(g.dtype)


def pool_mixer(z, w_pool, scale):
    bsz, seq, _ = z.shape
    zf = z.astype(jnp.float32)
    csum = jnp.concatenate([jnp.zeros((bsz, 1, POOL_WIDTH), jnp.float32), jnp.cumsum(zf, axis=1)], axis=1)
    t = jnp.arange(seq)
    groups = []
    for gi, window in enumerate(POOL_WINDOWS):
        ch = slice(gi * POOL_GROUP_DIM, (gi + 1) * POOL_GROUP_DIM)
        lo = jnp.maximum(t + 1 - window, 0)
        count = (t + 1 - lo).astype(jnp.float32)[None, :, None]
        mean = (csum[:, 1:, ch] - csum[:, lo, ch]) / count
        groups.append(mean - zf[:, :, ch])
    pooled = jnp.stack(groups, axis=2).astype(z.dtype)
    y = jnp.einsum('bsgc,gcd->bsgd', pooled, w_pool).reshape(bsz, seq, POOL_WIDTH)
    return y * scale


def mla_mixer(c_q, c_kv, k_rope, q_norm_g, kv_norm_g, w_uq, w_ukv, cos, sin):
    bsz, seq, _ = c_q.shape
    qk_dim = MLA_NOPE_DIM + MLA_ROPE_DIM
    q = (rms_norm(c_q, q_norm_g) @ w_uq).reshape(bsz, seq, MLA_HEADS, qk_dim)
    kv = (rms_norm(c_kv, kv_norm_g) @ w_ukv).reshape(bsz, seq, MLA_HEADS, MLA_NOPE_DIM + MLA_V_DIM)
    k_nope, v = kv[..., :MLA_NOPE_DIM], kv[..., MLA_NOPE_DIM:]
    q = jnp.concatenate([q[..., :MLA_NOPE_DIM], apply_rope(q[..., MLA_NOPE_DIM:], cos, sin)], axis=-1) * qk_dim ** -0.5
    k_pe = apply_rope(k_rope[:, :, None, :], cos, sin)
    k = jnp.concatenate([k_nope, jnp.broadcast_to(k_pe, (bsz, seq, MLA_HEADS, MLA_ROPE_DIM))], axis=-1)
    n_blocks = seq // Q_BLOCK
    q_blocks = jnp.moveaxis(q.reshape(bsz, n_blocks, Q_BLOCK, MLA_HEADS, qk_dim), 1, 0)
    key_chunk = jnp.arange(seq) // CHUNK

    def attend(args):
        q_blk, blk = args
        query_chunk = (blk * Q_BLOCK + jnp.arange(Q_BLOCK)) // CHUNK
        s = jnp.einsum('bqhd,bkhd->bhqk', q_blk, k).astype(jnp.float32)
        s = jnp.where(key_chunk[None, :] <= query_chunk[:, None], s, -jnp.inf)
        probs = jax.nn.softmax(s, axis=-1).astype(v.dtype)
        return jnp.einsum('bhqk,bkhe->bqhe', probs, v)

    out = lax.map(attend, (q_blocks, jnp.arange(n_blocks)))
    return jnp.moveaxis(out, 0, 1).reshape(bsz, seq, MLA_HEADS * MLA_V_DIM)


def token_mix(x, w_in, sgu_ln_g, sgu_ln_b, sgu_w, sgu_b, pool_w, pool_scale,
              mla_q_norm, mla_kv_norm, mla_w_uq, mla_w_ukv, w_branch, w_out,
              ret_cos, ret_sin, mla_cos, mla_sin):
    bsz, seq, dm = x.shape
    (sgu_u, sgu_v, ret_q, ret_k, ret_v, ret_g, pool_in,
     c_q, c_kv, k_rope, gate_logits) = _split(x @ w_in, IN_SIZES)
    ys = (sgu_mixer(sgu_u, sgu_v, sgu_ln_g, sgu_ln_b, sgu_w, sgu_b).astype(x.dtype),
          retention_mixer(ret_q, ret_k, ret_v, ret_g, ret_cos, ret_sin).astype(x.dtype),
          pool_mixer(pool_in, pool_w, pool_scale).astype(x.dtype),
          mla_mixer(c_q, c_kv, k_rope, mla_q_norm, mla_kv_norm, mla_w_uq, mla_w_ukv, mla_cos, mla_sin).astype(x.dtype))
    gates = jax.nn.sigmoid(gate_logits).reshape(bsz, seq, N_BRANCH, dm)
    merged = gates[:, :, 0] * (ys[0] @ w_branch[0])
    for n in range(1, N_BRANCH):
        merged = merged + gates[:, :, n] * (ys[n] @ w_branch[n])
    return merged @ w_out


def setup_inputs(seed: int = 0) -> dict:
    key = jax.random.key(seed)
    ks = jax.random.split(key, 32)

    def nrm(k, shape, scale):
        return jax.random.normal(k, shape, jnp.float32) * scale

    def gain(k, shape):
        return 1.0 + 0.05 * jax.random.normal(k, shape, jnp.float32)

    def bias(k, shape):
        return 0.02 * jax.random.normal(k, shape, jnp.float32)

    offset = jax.random.randint(ks[2], (BATCH, 1), 0, 64, dtype=jnp.int32) * CHUNK
    positions = (offset + jnp.arange(SEQ, dtype=jnp.int32)[None, :]).astype(jnp.int32)
    L, D = DEPTH, D_MODEL
    return {
        'x': nrm(ks[0], (BATCH, SEQ, D), 1.0),
        'p': nrm(ks[1], (DEPTH, BATCH, SEQ, PLE_DIM), 1.0),
        'positions': positions,
        'ffn1_up': nrm(ks[3], (L, D, 2 * D_FF), D ** -0.5),
        'ffn1_down': nrm(ks[4], (L, D_FF, D), BETA * D_FF ** -0.5),
        'ln1_g': gain(ks[5], (L, D)),
        'ln1_b': bias(ks[6], (L, D)),
        'w_in': nrm(ks[7], (L, D, IN_COLS), D ** -0.5),
        'sgu_ln_g': gain(ks[8], (L, SGU_WIDTH)),
        'sgu_ln_b': bias(ks[9], (L, SGU_WIDTH)),
        'sgu_w': nrm(ks[10], (L, SGU_GROUPS, SGU_BLOCK, SGU_BLOCK), SGU_BLOCK ** -0.5),
        'sgu_b': 1.0 + 0.1 * jax.random.normal(ks[11], (L, SGU_GROUPS, SGU_BLOCK), jnp.float32),
        'pool_w': nrm(ks[12], (L, 4, POOL_GROUP_DIM, POOL_GROUP_DIM), POOL_GROUP_DIM ** -0.5),
        'pool_scale': gain(ks[13], (L, POOL_WIDTH)),
        'mla_q_norm': gain(ks[14], (L, MLA_Q_RANK)),
        'mla_kv_norm': gain(ks[15], (L, MLA_KV_RANK)),
        'mla_w_uq': nrm(ks[16], (L, MLA_Q_RANK, MLA_HEADS * (MLA_NOPE_DIM + MLA_ROPE_DIM)), MLA_Q_RANK ** -0.5),
        'mla_w_ukv': nrm(ks[17], (L, MLA_KV_RANK, MLA_HEADS * (MLA_NOPE_DIM + MLA_V_DIM)), MLA_KV_RANK ** -0.5),
        'w_branch': nrm(ks[18], (L, N_BRANCH, BRANCH_WIDTH, D), BRANCH_WIDTH ** -0.5),
        'w_out': nrm(ks[19], (L, D, D), BETA * D ** -0.5),
        'ln2_g': gain(ks[20], (L, D)),
        'ln2_b': bias(ks[21], (L, D)),
        'ffn2_up': nrm(ks[22], (L, D, 2 * D_FF), D ** -0.5),
        'ffn2_down': nrm(ks[23], (L, D_FF, D), BETA * D_FF ** -0.5),
        'w_ple_gate': nrm(ks[24], (L, D, D), D ** -0.5),
        'w_ple': nrm(ks[25], (L, PLE_DIM, D), PLE_DIM ** -0.5),
        'ln3_g': gain(ks[26], (L, D)),
        'ln3_b': bias(ks[27], (L, D)),
    }


def reference(x, p, positions, ffn1_up, ffn1_down, ln1_g, ln1_b, w_in, sgu_ln_g, sgu_ln_b,
              sgu_w, sgu_b, pool_w, pool_scale, mla_q_norm, mla_kv_norm, mla_w_uq, mla_w_ukv,
              w_branch, w_out, ln2_g, ln2_b, ffn2_up, ffn2_down, w_ple_gate, w_ple, ln3_g, ln3_b):
    ret_cos, ret_sin = rope_tables(positions, RET_KDIM)
    mla_cos, mla_sin = rope_tables(positions, MLA_ROPE_DIM)
    for i in range(DEPTH):
        x = layer_norm(ALPHA * x + 0.5 * swiglu(x, ffn1_up[i], ffn1_down[i]), ln1_g[i], ln1_b[i])
        mix = token_mix(x, w_in[i], sgu_ln_g[i], sgu_ln_b[i], sgu_w[i], sgu_b[i], pool_w[i], pool_scale[i],
                        mla_q_norm[i], mla_kv_norm[i], mla_w_uq[i], mla_w_ukv[i], w_branch[i], w_out[i],
                        ret_cos, ret_sin, mla_cos, mla_sin)
        x = layer_norm(ALPHA * x + mix, ln2_g[i], ln2_b[i])
        ple = jax.nn.sigmoid(x @ w_ple_gate[i]) * (p[i] @ w_ple[i])
        x = layer_norm(ALPHA * x + 0.5 * swiglu(x, ffn2_up[i], ffn2_down[i]) + ple, ln3_g[i], ln3_b[i])
    return x
```

```python
import functools
import math

import numpy as np
import jax
import jax.numpy as jnp
from jax import lax
from jax.experimental import pallas as pl
from jax.experimental.pallas import tpu as pltpu

D_MODEL = 1024
DEPTH = 2
CHUNK = 64
SGU_WIDTH = 256
SGU_BLOCK = 128
SGU_GROUPS = 4
RET_HEADS = 4
RET_KDIM = 64
POOL_WINDOWS = (2, 4, 8, 16)
POOL_HALO = 16
MLA_HEADS = 4
MLA_NOPE = 64
MLA_ROPE = 32
MLA_QK = MLA_NOPE + MLA_ROPE
MLA_V = 64
MLA_Q_RANK = 256
MLA_KV_RANK = 128
HEAD_PAD = 128
ROPE_BASE = 10000.0
N_BRANCH = 4
BRANCH_WIDTH = 256
D_FF = 2816
PLE_DIM = 256
ALPHA = (2 * DEPTH) ** 0.25
LN_EPS = 1e-5
RMS_EPS = 1e-6
GN_EPS = 1e-5

_OFF_SGU = 0
_OFF_RET = 512
_OFF_POOL = 1536
_OFF_CQ = 1792
_OFF_CKV = 2048
_OFF_KROPE = 2176
_OFF_GATE = 2208

TM = 512
FF_CHUNK = 1408
RET_TILE = 256
ATT_TQ = 256
ATT_TK = 256
VMEM_LIMIT = 48 * 1024 * 1024

BF16 = jnp.bfloat16
F32 = jnp.float32


def _dot(a, b):
    return jnp.dot(a, b, preferred_element_type=F32)


def _dot_nt(a, b):
    return lax.dot_general(a, b, (((1,), (1,)), ((), ())), preferred_element_type=F32)


def _dot_tn(a, b):
    return lax.dot_general(a, b, (((0,), (0,)), ((), ())), preferred_element_type=F32)


def _layer_norm(r, g, b, eps):
    mu = jnp.mean(r, axis=-1, keepdims=True)
    d = r - mu
    var = jnp.mean(d * d, axis=-1, keepdims=True)
    return d * lax.rsqrt(var + eps) * g + b


def _const_spec(shape):
    zeros = (0,) * len(shape)
    return pl.BlockSpec(shape, lambda *_: zeros, pipeline_mode=pl.Buffered(1))


def _params(n_grid, semantics="parallel"):
    return pltpu.CompilerParams(
        dimension_semantics=(semantics,) * n_grid, vmem_limit_bytes=VMEM_LIMIT)


def _ffn_body(x, up_a_ref, up_b_ref, down_ref):
    xb = x.astype(BF16)
    acc = None
    for c in range(D_FF // FF_CHUNK):
        cols = slice(c * FF_CHUNK, (c + 1) * FF_CHUNK)
        a = _dot(xb, up_a_ref[:, cols])
        b = _dot(xb, up_b_ref[:, cols])
        h = (a * jax.nn.sigmoid(a) * b).astype(BF16)
        y = _dot(h, down_ref[cols, :])
        acc = y if acc is None else acc + y
    return xb, acc


def _ffn_ln_kernel(x_ref, up_a_ref, up_b_ref, down_ref, g_ref, b_ref, o_ref):
    x = x_ref[...]
    _, y = _ffn_body(x, up_a_ref, up_b_ref, down_ref)
    o_ref[...] = _layer_norm(ALPHA * x + 0.5 * y, g_ref[...], b_ref[...], LN_EPS)


def _ffn_ple_ln_kernel(x_ref, p_ref, up_a_ref, up_b_ref, down_ref, wpg_ref, wp_ref,
                       g_ref, b_ref, o_ref):
    x = x_ref[...]
    xb, y = _ffn_body(x, up_a_ref, up_b_ref, down_ref)
    ple = jax.nn.sigmoid(_dot(xb, wpg_ref[...])) * _dot(p_ref[...].astype(BF16), wp_ref[...])
    o_ref[...] = _layer_norm(ALPHA * x + 0.5 * y + ple, g_ref[...], b_ref[...], LN_EPS)


def _ffn_ln(x, up_a, up_b, down, g, b):
    n = x.shape[0]
    row = pl.BlockSpec((TM, D_MODEL), lambda i: (i, 0))
    return pl.pallas_call(
        _ffn_ln_kernel,
        grid=(n // TM,),
        in_specs=[row, _const_spec(up_a.shape), _const_spec(up_b.shape), _const_spec(down.shape),
                  _const_spec(g.shape), _const_spec(b.shape)],
        out_specs=row,
        out_shape=jax.ShapeDtypeStruct((n, D_MODEL), F32),
        compiler_params=_params(1),
        name="ffn_ln",
    )(x, up_a, up_b, down, g, b)


def _ffn_ple_ln(x, p, up_a, up_b, down, wpg, wp, g, b):
    n = x.shape[0]
    row = pl.BlockSpec((TM, D_MODEL), lambda i: (i, 0))
    return pl.pallas_call(
        _ffn_ple_ln_kernel,
        grid=(n // TM,),
        in_specs=[row, pl.BlockSpec((TM, PLE_DIM), lambda i: (i, 0)),
                  _const_spec(up_a.shape), _const_spec(up_b.shape), _const_spec(down.shape),
                  _const_spec(wpg.shape), _const_spec(wp.shape),
                  _const_spec(g.shape), _const_spec(b.shape)],
        out_specs=row,
        out_shape=jax.ShapeDtypeStruct((n, D_MODEL), F32),
        compiler_params=_params(1),
        name="ffn_ple_ln",
    )(x, p, up_a, up_b, down, wpg, wp, g, b)


def _sgu_kernel(x_ref, w_ref, lng_ref, lnb_ref, ws_ref, bias_ref, o_ref):
    xb = x_ref[...].astype(BF16)
    uv = _dot(xb, w_ref[...])
    u = jax.nn.gelu(uv[:, :SGU_WIDTH])
    v = _layer_norm(jax.nn.gelu(uv[:, SGU_WIDTH:]), lng_ref[...], lnb_ref[...], LN_EPS)
    vb = v.astype(BF16)
    t_row = lax.broadcasted_iota(jnp.int32, (SGU_BLOCK, SGU_BLOCK), 0)
    t_col = lax.broadcasted_iota(jnp.int32, (SGU_BLOCK, SGU_BLOCK), 1)
    causal = t_row >= t_col
    w_groups = [jnp.where(causal, ws_ref[g], 0.0).astype(BF16) for g in range(SGU_GROUPS)]
    group = lax.broadcasted_iota(jnp.int32, (SGU_BLOCK, SGU_WIDTH), 1) // (SGU_WIDTH // SGU_GROUPS)
    bias = bias_ref[...]
    for blk in range(TM // SGU_BLOCK):
        rows = slice(blk * SGU_BLOCK, (blk + 1) * SGU_BLOCK)
        v_blk = vb[rows]
        mixed = bias
        for g in range(SGU_GROUPS):
            mixed = mixed + jnp.where(group == g, _dot(w_groups[g], v_blk), 0.0)
        o_ref[rows, :] = (u[rows] * mixed).astype(o_ref.dtype)


def _sgu(x, w, lng, lnb, ws, bias):
    n = x.shape[0]
    return pl.pallas_call(
        _sgu_kernel,
        grid=(n // TM,),
        in_specs=[pl.BlockSpec((TM, D_MODEL), lambda i: (i, 0)), _const_spec(w.shape),
                  _const_spec(lng.shape), _const_spec(lnb.shape), _const_spec(ws.shape),
                  _const_spec(bias.shape)],
        out_specs=pl.BlockSpec((TM, BRANCH_WIDTH), lambda i: (i, 0)),
        out_shape=jax.ShapeDtypeStruct((n, BRANCH_WIDTH), BF16),
        compiler_params=_params(1),
        name="sgu_mixer",
    )(x, w, lng, lnb, ws, bias)


def _swap_halves(x, half):
    width = x.shape[-1]
    lane = lax.broadcasted_iota(jnp.int32, x.shape, x.ndim - 1)
    first = (lane % (2 * half)) < half
    return jnp.where(first, pltpu.roll(x, width - half, x.ndim - 1), pltpu.roll(x, half, x.ndim - 1))


def _ret_kernel(x_ref, w_ref, cos_ref, sin_ref, dmask_ref, rowdec_ref, keydec_ref, tiledec_ref,
                o_ref, state_ref):
    width = RET_HEADS * RET_KDIM

    @pl.when(pl.program_id(1) == 0)
    def _():
        state_ref[...] = jnp.zeros_like(state_ref)

    xb = x_ref[...].astype(BF16)
    proj = _dot(xb, w_ref[...])
    cos = jnp.concatenate([cos_ref[...], cos_ref[...]], axis=1)
    sin = jnp.concatenate([sin_ref[...], sin_ref[...]], axis=1)
    q = proj[:, :width]
    k = proj[:, width:2 * width]
    q = q * cos + _swap_halves(q, RET_KDIM // 2) * sin
    k = (k * cos + _swap_halves(k, RET_KDIM // 2) * sin) * RET_KDIM ** -0.5
    v = proj[:, 2 * width:3 * width]
    gate = proj[:, 3 * width:]
    vb = v.astype(BF16)
    kb = k.astype(BF16)

    head = lax.broadcasted_iota(jnp.int32, (RET_TILE, width), 1) // RET_KDIM
    y = _dot(q.astype(BF16), state_ref[...].astype(BF16)) * rowdec_ref[...]
    for h in range(RET_HEADS):
        qh = jnp.where(head == h, q, 0.0).astype(BF16)
        scores = _dot_nt(qh, kb) * dmask_ref[h]
        y = y + jnp.where(head == h, _dot(scores.astype(BF16), vb), 0.0)

    kd = (k * keydec_ref[...]).astype(BF16)
    row_head = lax.broadcasted_iota(jnp.int32, (width, width), 0) // RET_KDIM
    col_head = lax.broadcasted_iota(jnp.int32, (width, width), 1) // RET_KDIM
    kv = jnp.where(row_head == col_head, _dot_tn(kd, vb), 0.0)
    state_ref[...] = state_ref[...] * tiledec_ref[...] + kv

    inv = 1.0 / RET_KDIM
    mu = jnp.zeros_like(y)
    for h in range(RET_HEADS):
        s = jnp.sum(jnp.where(head == h, y, 0.0), axis=1, keepdims=True) * inv
        mu = jnp.where(head == h, s, mu)
    d = y - mu
    var = jnp.zeros_like(y)
    for h in range(RET_HEADS):
        s = jnp.sum(jnp.where(head == h, d * d, 0.0), axis=1, keepdims=True) * inv
        var = jnp.where(head == h, s, var)
    yn = d * lax.rsqrt(var + GN_EPS)
    o_ref[...] = (gate * jax.nn.sigmoid(gate) * yn).astype(o_ref.dtype)


def _retention_tables():
    heads = np.arange(RET_HEADS, dtype=np.float64)
    log_gamma = np.log1p(-np.exp2(-5.0 - heads))
    t = np.arange(RET_TILE)
    chunk = t // CHUNK
    diff = (t[:, None] - t[None, :]).astype(np.float64)
    same = chunk[:, None] == chunk[None, :]
    earlier = chunk[None, :] < chunk[:, None]
    expo = np.where(same, np.abs(diff), diff)
    dmask = np.where((same | earlier)[None], np.exp(log_gamma[:, None, None] * expo[None]), 0.0)
    rowdec = np.exp(log_gamma[None, :] * (t[:, None] + 1.0))
    keydec = np.exp(log_gamma[None, :] * (RET_TILE - 1.0 - t[:, None]))
    tiledec = np.exp(log_gamma * RET_TILE)
    rep = lambda a: np.repeat(a, RET_KDIM, axis=-1)
    width = RET_HEADS * RET_KDIM
    tiledec_full = np.broadcast_to(rep(tiledec[None, :]).T, (width, width))
    return (jnp.asarray(dmask, F32), jnp.asarray(rep(rowdec), F32), jnp.asarray(rep(keydec), F32),
            jnp.asarray(tiledec_full, F32))


def _retention(x3, w, cos, sin):
    bsz, seq, _ = x3.shape
    width = RET_HEADS * RET_KDIM
    dmask, rowdec, keydec, tiledec = _retention_tables()
    tile = lambda last: pl.BlockSpec((None, RET_TILE, last), lambda b, s: (b, s, 0))
    return pl.pallas_call(
        _ret_kernel,
        grid=(bsz, seq // RET_TILE),
        in_specs=[tile(D_MODEL), _const_spec(w.shape), tile(128), tile(128),
                  _const_spec(dmask.shape), _const_spec(rowdec.shape), _const_spec(keydec.shape),
                  _const_spec(tiledec.shape)],
        out_specs=tile(BRANCH_WIDTH),
        out_shape=jax.ShapeDtypeStruct((bsz, seq, BRANCH_WIDTH), BF16),
        scratch_shapes=[pltpu.VMEM((width, width), F32)],
        compiler_params=_params(2, "arbitrary"),
        name="retention_mixer",
    )(x3, w, cos, sin, dmask, rowdec, keydec, tiledec)


def _pool_kernel(x_ref, w_ref, wp_ref, scale_ref, o_ref, zext_ref):
    s = pl.program_id(1)

    @pl.when(s == 0)
    def _():
        zext_ref[0:POOL_HALO, :] = jnp.zeros((POOL_HALO, BRANCH_WIDTH), F32)

    z = _dot(x_ref[...].astype(BF16), w_ref[...])
    zext_ref[POOL_HALO:POOL_HALO + TM, :] = z
    group = lax.broadcasted_iota(jnp.int32, (TM, BRANCH_WIDTH), 1) // (BRANCH_WIDTH // len(POOL_WINDOWS))
    acc = z
    win = jnp.zeros_like(z)
    window = jnp.zeros((TM, BRANCH_WIDTH), jnp.int32)
    for d in range(1, POOL_HALO):
        acc = acc + zext_ref[POOL_HALO - d:POOL_HALO - d + TM, :]
        if d + 1 in POOL_WINDOWS:
            gi = POOL_WINDOWS.index(d + 1)
            win = jnp.where(group == gi, acc, win)
            window = jnp.where(group == gi, d + 1, window)
    t = s * TM + lax.broadcasted_iota(jnp.int32, (TM, BRANCH_WIDTH), 0)
    count = jnp.minimum(t + 1, window).astype(F32)
    pooled = win / count - z
    y = _dot(pooled.astype(BF16), wp_ref[...]) * scale_ref[...]
    o_ref[...] = y.astype(o_ref.dtype)
    zext_ref[0:POOL_HALO, :] = zext_ref[TM:TM + POOL_HALO, :]


def _pool(x3, w, wp, scale):
    bsz, seq, _ = x3.shape
    tile = lambda last: pl.BlockSpec((None, TM, last), lambda b, s: (b, s, 0))
    return pl.pallas_call(
        _pool_kernel,
        grid=(bsz, seq // TM),
        in_specs=[tile(D_MODEL), _const_spec(w.shape), _const_spec(wp.shape), _const_spec(scale.shape)],
        out_specs=tile(BRANCH_WIDTH),
        out_shape=jax.ShapeDtypeStruct((bsz, seq, BRANCH_WIDTH), BF16),
        scratch_shapes=[pltpu.VMEM((TM + POOL_HALO, BRANCH_WIDTH), F32)],
        compiler_params=_params(2, "arbitrary"),
        name="pool_mixer",
    )(x3, w, wp, scale)


def _rms_norm(x, g):
    return x * lax.rsqrt(jnp.mean(x * x, axis=-1, keepdims=True) + RMS_EPS) * g


def _mla_pre_kernel(x_ref, w_ref, ck_ref, sk_ref, qg_ref, kvg_ref, wq1_ref, wq2_ref, wk_ref, wv_ref,
                    q_ref, k_ref, v_ref):
    xb = x_ref[...].astype(BF16)
    proj = _dot(xb, w_ref[...])
    cq = _rms_norm(proj[:, :MLA_Q_RANK], qg_ref[...]).astype(BF16)
    ckv = _rms_norm(proj[:, MLA_Q_RANK:MLA_Q_RANK + MLA_KV_RANK], kvg_ref[...]).astype(BF16)
    off = MLA_Q_RANK + MLA_KV_RANK
    ck = ck_ref[...]
    sk = sk_ref[...]
    k_pe = proj[:, off:off + HEAD_PAD] * ck + proj[:, off + HEAD_PAD:off + 2 * HEAD_PAD] * sk
    lane = lax.broadcasted_iota(jnp.int32, ck.shape, 1)
    scale = MLA_QK ** -0.5
    cq_tab = (ck + jnp.where(lane < MLA_NOPE, 1.0, 0.0)) * scale
    sq_tab = sk * scale
    tile4 = lambda a: jnp.concatenate([a] * MLA_HEADS, axis=1)
    q = _dot(cq, wq1_ref[...]) * tile4(cq_tab) + _dot(cq, wq2_ref[...]) * tile4(sq_tab)
    k = _dot(ckv, wk_ref[...]) + tile4(k_pe)
    q_ref[...] = q.astype(q_ref.dtype)
    k_ref[...] = k.astype(k_ref.dtype)
    v_ref[...] = _dot(ckv, wv_ref[...]).astype(v_ref.dtype)


def _mla_pre(x, w, ck, sk, qg, kvg, wq1, wq2, wk, wv):
    n = x.shape[0]
    row = lambda last: pl.BlockSpec((TM, last), lambda i: (i, 0))
    consts = [w, qg, kvg, wq1, wq2, wk, wv]
    return pl.pallas_call(
        _mla_pre_kernel,
        grid=(n // TM,),
        in_specs=[row(D_MODEL), _const_spec(w.shape), row(HEAD_PAD), row(HEAD_PAD)]
                 + [_const_spec(a.shape) for a in consts[1:]],
        out_specs=[row(MLA_HEADS * HEAD_PAD), row(MLA_HEADS * HEAD_PAD), row(MLA_HEADS * MLA_V)],
        out_shape=[jax.ShapeDtypeStruct((n, MLA_HEADS * HEAD_PAD), BF16),
                   jax.ShapeDtypeStruct((n, MLA_HEADS * HEAD_PAD), BF16),
                   jax.ShapeDtypeStruct((n, MLA_HEADS * MLA_V), BF16)],
        compiler_params=_params(1),
        name="mla_project",
    )(x, w, ck, sk, qg, kvg, wq1, wq2, wk, wv)


def _attn_kernel(q_ref, k_ref, v_ref, o_ref):
    qi = pl.program_id(1)
    row_chunk = (qi * ATT_TQ + lax.broadcasted_iota(jnp.int32, (ATT_TQ, ATT_TK), 0)) // CHUNK
    col_chunk = (qi * ATT_TK + lax.broadcasted_iota(jnp.int32, (ATT_TQ, ATT_TK), 1)) // CHUNK
    diag_mask = col_chunk <= row_chunk
    lane = lax.broadcasted_iota(jnp.int32, (ATT_TQ, 2 * MLA_V), 1)

    def head_out(h):
        hcols = slice(h * HEAD_PAD, (h + 1) * HEAD_PAD)
        vcols = slice((h // 2) * 2 * MLA_V, (h // 2 + 1) * 2 * MLA_V)
        qh = q_ref[:, hcols]

        def step(j, carry, masked):
            m, l, acc = carry
            rows = pl.ds(pl.multiple_of(j * ATT_TK, ATT_TK), ATT_TK)
            s = _dot_nt(qh, k_ref[rows, hcols])
            if masked:
                s = jnp.where(diag_mask, s, -jnp.inf)
            m_new = jnp.maximum(m, jnp.max(s, axis=1, keepdims=True))
            a = jnp.exp(m - m_new)
            p = jnp.exp(s - m_new)
            l = a * l + jnp.sum(p, axis=1, keepdims=True)
            acc = a * acc + _dot(p.astype(BF16), v_ref[rows, vcols])
            return m_new, l, acc

        init = (jnp.full((ATT_TQ, 1), -jnp.inf, F32), jnp.zeros((ATT_TQ, 1), F32),
                jnp.zeros((ATT_TQ, 2 * MLA_V), F32))
        carry = lax.fori_loop(0, qi, functools.partial(step, masked=False), init)
        _, l, acc = step(qi, carry, True)
        return acc / l

    for pair in range(MLA_HEADS // 2):
        out = jnp.where(lane < MLA_V, head_out(2 * pair), head_out(2 * pair + 1))
        o_ref[:, pair * 2 * MLA_V:(pair + 1) * 2 * MLA_V] = out.astype(o_ref.dtype)


def _attention(q3, k3, v3):
    bsz, seq, _ = q3.shape
    return pl.pallas_call(
        _attn_kernel,
        grid=(bsz, seq // ATT_TQ),
        in_specs=[pl.BlockSpec((None, ATT_TQ, MLA_HEADS * HEAD_PAD), lambda b, i: (b, i, 0)),
                  pl.BlockSpec((None, seq, MLA_HEADS * HEAD_PAD), lambda b, i: (b, 0, 0)),
                  pl.BlockSpec((None, seq, MLA_HEADS * MLA_V), lambda b, i: (b, 0, 0))],
        out_specs=pl.BlockSpec((None, ATT_TQ, MLA_HEADS * MLA_V), lambda b, i: (b, i, 0)),
        out_shape=jax.ShapeDtypeStruct((bsz, seq, MLA_HEADS * MLA_V), BF16),
        compiler_params=_params(2),
        name="mla_attention",
    )(q3, k3, v3)


def _merge_kernel(x_ref, ya_ref, yb_ref, yc_ref, yd_ref, wg_ref, wb_ref, wo_ref, g_ref, b_ref, o_ref):
    x = x_ref[...]
    xb = x.astype(BF16)
    merged = None
    for n, y_ref in enumerate((ya_ref, yb_ref, yc_ref, yd_ref)):
        gate = jax.nn.sigmoid(_dot(xb, wg_ref[:, n * D_MODEL:(n + 1) * D_MODEL]))
        term = gate * _dot(y_ref[...], wb_ref[n])
        merged = term if merged is None else merged + term
    mix = _dot(merged.astype(BF16), wo_ref[...])
    o_ref[...] = _layer_norm(ALPHA * x + mix, g_ref[...], b_ref[...], LN_EPS)


def _merge(x, ys, wg, wb, wo, g, b):
    n = x.shape[0]
    row = lambda last: pl.BlockSpec((TM, last), lambda i: (i, 0))
    return pl.pallas_call(
        _merge_kernel,
        grid=(n // TM,),
        in_specs=[row(D_MODEL)] + [row(BRANCH_WIDTH)] * N_BRANCH
                 + [_const_spec(a.shape) for a in (wg, wb, wo, g, b)],
        out_specs=row(D_MODEL),
        out_shape=jax.ShapeDtypeStruct((n, D_MODEL), F32),
        compiler_params=_params(1),
        name="merge_ln",
    )(x, *ys, wg, wb, wo, g, b)


def _rope_tables(positions):
    pos = positions.astype(F32)[..., None]
    ang_r = pos * (ROPE_BASE ** (-jnp.arange(0, RET_KDIM, 2, dtype=F32) / RET_KDIM))
    cr, sr = jnp.cos(ang_r), jnp.sin(ang_r)
    ret_cos = jnp.tile(jnp.concatenate([cr, cr], axis=-1), (1, 1, 2))
    ret_sin = jnp.tile(jnp.concatenate([-sr, sr], axis=-1), (1, 1, 2))
    ang_m = pos * (ROPE_BASE ** (-jnp.arange(0, MLA_ROPE, 2, dtype=F32) / MLA_ROPE))
    cm, sm = jnp.cos(ang_m), jnp.sin(ang_m)
    pad = lambda a: jnp.pad(a, ((0, 0), (0, 0), (MLA_NOPE, HEAD_PAD - MLA_QK)))
    mla_cos = pad(jnp.concatenate([cm, cm], axis=-1))
    mla_sin = pad(jnp.concatenate([-sm, sm], axis=-1))
    return ret_cos, ret_sin, mla_cos, mla_sin


def _swap_cols(w):
    half = w.shape[-1] // 2
    return jnp.concatenate([w[..., half:], w[..., :half]], axis=-1)


def _mla_weights(w_in, w_uq, w_ukv):
    rows = w_in.shape[0]
    w_kr = w_in[:, _OFF_KROPE:_OFF_KROPE + MLA_ROPE]
    place = lambda w: jnp.pad(w, ((0, 0), (MLA_NOPE, HEAD_PAD - MLA_QK)))
    w_proj = jnp.concatenate([w_in[:, _OFF_CQ:_OFF_KROPE], place(w_kr), place(_swap_cols(w_kr))], axis=1)
    uq = w_uq.reshape(MLA_Q_RANK, MLA_HEADS, MLA_QK)
    q_nope, q_rope = uq[..., :MLA_NOPE], uq[..., MLA_NOPE:]
    zq = jnp.zeros((MLA_Q_RANK, MLA_HEADS, HEAD_PAD - MLA_QK), w_uq.dtype)
    wq1 = jnp.concatenate([q_nope, q_rope, zq], axis=-1).reshape(MLA_Q_RANK, MLA_HEADS * HEAD_PAD)
    wq2 = jnp.concatenate([jnp.zeros_like(q_nope), _swap_cols(q_rope), zq], axis=-1)
    wq2 = wq2.reshape(MLA_Q_RANK, MLA_HEADS * HEAD_PAD)
    ukv = w_ukv.reshape(MLA_KV_RANK, MLA_HEADS, MLA_NOPE + MLA_V)
    k_nope, v = ukv[..., :MLA_NOPE], ukv[..., MLA_NOPE:]
    wk = jnp.concatenate([k_nope, jnp.zeros((MLA_KV_RANK, MLA_HEADS, HEAD_PAD - MLA_NOPE), w_ukv.dtype)],
                         axis=-1).reshape(MLA_KV_RANK, MLA_HEADS * HEAD_PAD)
    wv = v.reshape(MLA_KV_RANK, MLA_HEADS * MLA_V)
    del rows
    return [a.astype(BF16) for a in (w_proj, wq1, wq2, wk, wv)]


def kernel(x, p, positions, ffn1_up, ffn1_down, ln1_g, ln1_b, w_in, sgu_ln_g, sgu_ln_b, sgu_w, sgu_b,
           pool_w, pool_scale, mla_q_norm, mla_kv_norm, mla_w_uq, mla_w_ukv, w_branch, w_out,
           ln2_g, ln2_b, ffn2_up, ffn2_down, w_ple_gate, w_ple, ln3_g, ln3_b):
    bsz, seq, dm = x.shape
    n = bsz * seq
    ret_cos, ret_sin, mla_cos, mla_sin = _rope_tables(positions)
    mla_cos2, mla_sin2 = mla_cos.reshape(n, HEAD_PAD), mla_sin.reshape(n, HEAD_PAD)
    row = lambda a: a.reshape(1, -1)
    bf = lambda a: a.astype(BF16)

    h = x.reshape(n, dm)
    for i in range(DEPTH):
        h = _ffn_ln(h, bf(ffn1_up[i][:, :D_FF]), bf(ffn1_up[i][:, D_FF:]), bf(ffn1_down[i]),
                    row(ln1_g[i]), row(ln1_b[i]))
        h3 = h.reshape(bsz, seq, dm)
        wi = w_in[i]
        y_a = _sgu(h, bf(wi[:, _OFF_SGU:_OFF_RET]), row(sgu_ln_g[i]), row(sgu_ln_b[i]), sgu_w[i],
                   jnp.repeat(sgu_b[i].T, SGU_WIDTH // SGU_GROUPS, axis=1))
        y_b = _retention(h3, bf(wi[:, _OFF_RET:_OFF_POOL]), ret_cos, ret_sin)
        y_c = _pool(h3, bf(wi[:, _OFF_POOL:_OFF_CQ]),
                    bf(jax.scipy.linalg.block_diag(*[pool_w[i, g] for g in range(len(POOL_WINDOWS))])),
                    row(pool_scale[i]))
        w_proj, wq1, wq2, wk, wv = _mla_weights(wi, mla_w_uq[i], mla_w_ukv[i])
        q, k, v = _mla_pre(h, w_proj, mla_cos2, mla_sin2, row(mla_q_norm[i]), row(mla_kv_norm[i]),
                           wq1, wq2, wk, wv)
        y_d = _attention(q.reshape(bsz, seq, -1), k.reshape(bsz, seq, -1), v.reshape(bsz, seq, -1))
        ys = (y_a, y_b.reshape(n, -1), y_c.reshape(n, -1), y_d.reshape(n, -1))
        h = _merge(h, ys, bf(wi[:, _OFF_GATE:]), bf(w_branch[i]), bf(w_out[i]), row(ln2_g[i]), row(ln2_b[i]))
        h = _ffn_ple_ln(h, p[i].reshape(n, PLE_DIM), bf(ffn2_up[i][:, :D_FF]), bf(ffn2_up[i][:, D_FF:]),
                        bf(ffn2_down[i]), bf(w_ple_gate[i]), bf(w_ple[i]), row(ln3_g[i]), row(ln3_b[i]))
    return h.reshape(bsz, seq, dm)
```

```python
import functools
import math

import numpy as np
import jax
import jax.numpy as jnp
from jax import lax
from jax.experimental import pallas as pl
from jax.experimental.pallas import tpu as pltpu

D_MODEL = 1024
DEPTH = 2
CHUNK = 64
SGU_WIDTH = 256
SGU_BLOCK = 128
SGU_GROUPS = 4
RET_HEADS = 4
RET_KDIM = 64
POOL_WINDOWS = (2, 4, 8, 16)
POOL_HALO = 16
MLA_HEADS = 4
MLA_NOPE = 64
MLA_ROPE = 32
MLA_QK = MLA_NOPE + MLA_ROPE
MLA_V = 64
MLA_Q_RANK = 256
MLA_KV_RANK = 128
HEAD_PAD = 128
ROPE_BASE = 10000.0
N_BRANCH = 4
BRANCH_WIDTH = 256
D_FF = 2816
PLE_DIM = 256
ALPHA = (2 * DEPTH) ** 0.25
LN_EPS = 1e-5
RMS_EPS = 1e-6
GN_EPS = 1e-5

_OFF_SGU = 0
_OFF_RET = 512
_OFF_POOL = 1536
_OFF_CQ = 1792
_OFF_CKV = 2048
_OFF_KROPE = 2176
_OFF_GATE = 2208

TM = 512
FF_CHUNK = 1408
RET_TILE = 256
ATT_TQ = 256
ATT_TK = 256
VMEM_LIMIT = 48 * 1024 * 1024

BF16 = jnp.bfloat16
F32 = jnp.float32


def _dot(a, b):
    return jnp.dot(a, b, preferred_element_type=F32)


def _dot_nt(a, b):
    return lax.dot_general(a, b, (((1,), (1,)), ((), ())), preferred_element_type=F32)


def _dot_tn(a, b):
    return lax.dot_general(a, b, (((0,), (0,)), ((), ())), preferred_element_type=F32)


def _layer_norm(r, g, b, eps):
    mu = jnp.mean(r, axis=-1, keepdims=True)
    d = r - mu
    var = jnp.mean(d * d, axis=-1, keepdims=True)
    return d * lax.rsqrt(var + eps) * g + b


def _const_spec(shape):
    zeros = (0,) * len(shape)
    return pl.BlockSpec(shape, lambda *_: zeros, pipeline_mode=pl.Buffered(1))


def _params(n_grid, semantics="parallel"):
    return pltpu.CompilerParams(
        dimension_semantics=(semantics,) * n_grid, vmem_limit_bytes=VMEM_LIMIT)


def _ffn_body(x, up_a_ref, up_b_ref, down_ref):
    xb = x.astype(BF16)
    acc = None
    for c in range(D_FF // FF_CHUNK):
        cols = slice(c * FF_CHUNK, (c + 1) * FF_CHUNK)
        a = _dot(xb, up_a_ref[:, cols])
        b = _dot(xb, up_b_ref[:, cols])
        h = (a * jax.nn.sigmoid(a) * b).astype(BF16)
        y = _dot(h, down_ref[cols, :])
        acc = y if acc is None else acc + y
    return xb, acc


def _ffn_ln_kernel(x_ref, up_a_ref, up_b_ref, down_ref, g_ref, b_ref, o_ref):
    x = x_ref[...]
    _, y = _ffn_body(x, up_a_ref, up_b_ref, down_ref)
    o_ref[...] = _layer_norm(ALPHA * x + 0.5 * y, g_ref[...], b_ref[...], LN_EPS)


def _ffn_ple_ln_kernel(x_ref, p_ref, up_a_ref, up_b_ref, down_ref, wpg_ref, wp_ref,
                       g_ref, b_ref, o_ref):
    x = x_ref[...]
    xb, y = _ffn_body(x, up_a_ref, up_b_ref, down_ref)
    ple = jax.nn.sigmoid(_dot(xb, wpg_ref[...])) * _dot(p_ref[...].astype(BF16), wp_ref[...])
    o_ref[...] = _layer_norm(ALPHA * x + 0.5 * y + ple, g_ref[...], b_ref[...], LN_EPS)


def _ffn_ln(x, up_a, up_b, down, g, b):
    n = x.shape[0]
    row = pl.BlockSpec((TM, D_MODEL), lambda i: (i, 0))
    return pl.pallas_call(
        _ffn_ln_kernel,
        grid=(n // TM,),
        in_specs=[row, _const_spec(up_a.shape), _const_spec(up_b.shape), _const_spec(down.shape),
                  _const_spec(g.shape), _const_spec(b.shape)],
        out_specs=row,
        out_shape=jax.ShapeDtypeStruct((n, D_MODEL), F32),
        compiler_params=_params(1),
        name="ffn_ln",
    )(x, up_a, up_b, down, g, b)


def _ffn_ple_ln(x, p, up_a, up_b, down, wpg, wp, g, b):
    n = x.shape[0]
    row = pl.BlockSpec((TM, D_MODEL), lambda i: (i, 0))
    return pl.pallas_call(
        _ffn_ple_ln_kernel,
        grid=(n // TM,),
        in_specs=[row, pl.BlockSpec((TM, PLE_DIM), lambda i: (i, 0)),
                  _const_spec(up_a.shape), _const_spec(up_b.shape), _const_spec(down.shape),
                  _const_spec(wpg.shape), _const_spec(wp.shape),
                  _const_spec(g.shape), _const_spec(b.shape)],
        out_specs=row,
        out_shape=jax.ShapeDtypeStruct((n, D_MODEL), F32),
        compiler_params=_params(1),
        name="ffn_ple_ln",
    )(x, p, up_a, up_b, down, wpg, wp, g, b)


def _sgu_kernel(x_ref, w_ref, lng_ref, lnb_ref, ws_ref, bias_ref, o_ref):
    xb = x_ref[...].astype(BF16)
    uv = _dot(xb, w_ref[...])
    u = jax.nn.gelu(uv[:, :SGU_WIDTH])
    v = _layer_norm(jax.nn.gelu(uv[:, SGU_WIDTH:]), lng_ref[...], lnb_ref[...], LN_EPS)
    vb = v.astype(BF16)
    t_row = lax.broadcasted_iota(jnp.int32, (SGU_BLOCK, SGU_BLOCK), 0)
    t_col = lax.broadcasted_iota(jnp.int32, (SGU_BLOCK, SGU_BLOCK), 1)
    causal = t_row >= t_col
    w_groups = [jnp.where(causal, ws_ref[g], 0.0).astype(BF16) for g in range(SGU_GROUPS)]
    group = lax.broadcasted_iota(jnp.int32, (SGU_BLOCK, SGU_WIDTH), 1) // (SGU_WIDTH // SGU_GROUPS)
    bias = bias_ref[...]
    for blk in range(TM // SGU_BLOCK):
        rows = slice(blk * SGU_BLOCK, (blk + 1) * SGU_BLOCK)
        v_blk = vb[rows]
        mixed = bias
        for g in range(SGU_GROUPS):
            mixed = mixed + jnp.where(group == g, _dot(w_groups[g], v_blk), 0.0)
        o_ref[rows, :] = (u[rows] * mixed).astype(o_ref.dtype)


def _sgu(x, w, lng, lnb, ws, bias):
    n = x.shape[0]
    return pl.pallas_call(
        _sgu_kernel,
        grid=(n // TM,),
        in_specs=[pl.BlockSpec((TM, D_MODEL), lambda i: (i, 0)), _const_spec(w.shape),
                  _const_spec(lng.shape), _const_spec(lnb.shape), _const_spec(ws.shape),
                  _const_spec(bias.shape)],
        out_specs=pl.BlockSpec((TM, BRANCH_WIDTH), lambda i: (i, 0)),
        out_shape=jax.ShapeDtypeStruct((n, BRANCH_WIDTH), BF16),
        compiler_params=_params(1),
        name="sgu_mixer",
    )(x, w, lng, lnb, ws, bias)


def _swap_halves(x, half):
    width = x.shape[-1]
    lane = lax.broadcasted_iota(jnp.int32, x.shape, x.ndim - 1)
    first = (lane % (2 * half)) < half
    return jnp.where(first, pltpu.roll(x, width - half, x.ndim - 1), pltpu.roll(x, half, x.ndim - 1))


def _ret_kernel(x_ref, w_ref, cos_ref, sin_ref, dmask_ref, rowdec_ref, keydec_ref, tiledec_ref,
                o_ref, state_ref):
    width = RET_HEADS * RET_KDIM

    @pl.when(pl.program_id(1) == 0)
    def _():
        state_ref[...] = jnp.zeros_like(state_ref)

    xb = x_ref[...].astype(BF16)
    proj = _dot(xb, w_ref[...])
    cos = jnp.concatenate([cos_ref[...], cos_ref[...]], axis=1)
    sin = jnp.concatenate([sin_ref[...], sin_ref[...]], axis=1)
    q = proj[:, :width]
    k = proj[:, width:2 * width]
    q = q * cos + _swap_halves(q, RET_KDIM // 2) * sin
    k = (k * cos + _swap_halves(k, RET_KDIM // 2) * sin) * RET_KDIM ** -0.5
    v = proj[:, 2 * width:3 * width]
    gate = proj[:, 3 * width:]
    vb = v.astype(BF16)
    kb = k.astype(BF16)

    head = lax.broadcasted_iota(jnp.int32, (RET_TILE, width), 1) // RET_KDIM
    y = _dot(q.astype(BF16), state_ref[...].astype(BF16)) * rowdec_ref[...]
    for h in range(RET_HEADS):
        qh = jnp.where(head == h, q, 0.0).astype(BF16)
        scores = _dot_nt(qh, kb) * dmask_ref[h]
        y = y + jnp.where(head == h, _dot(scores.astype(BF16), vb), 0.0)

    kd = (k * keydec_ref[...]).astype(BF16)
    row_head = lax.broadcasted_iota(jnp.int32, (width, width), 0) // RET_KDIM
    col_head = lax.broadcasted_iota(jnp.int32, (width, width), 1) // RET_KDIM
    kv = jnp.where(row_head == col_head, _dot_tn(kd, vb), 0.0)
    state_ref[...] = state_ref[...] * tiledec_ref[...] + kv

    inv = 1.0 / RET_KDIM
    mu = jnp.zeros_like(y)
    for h in range(RET_HEADS):
        s = jnp.sum(jnp.where(head == h, y, 0.0), axis=1, keepdims=True) * inv
        mu = jnp.where(head == h, s, mu)
    d = y - mu
    var = jnp.zeros_like(y)
    for h in range(RET_HEADS):
        s = jnp.sum(jnp.where(head == h, d * d, 0.0), axis=1, keepdims=True) * inv
        var = jnp.where(head == h, s, var)
    yn = d * lax.rsqrt(var + GN_EPS)
    o_ref[...] = (gate * jax.nn.sigmoid(gate) * yn).astype(o_ref.dtype)


def _retention_tables():
    heads = np.arange(RET_HEADS, dtype=np.float64)
    log_gamma = np.log1p(-np.exp2(-5.0 - heads))
    t = np.arange(RET_TILE)
    chunk = t // CHUNK
    diff = (t[:, None] - t[None, :]).astype(np.float64)
    same = chunk[:, None] == chunk[None, :]
    earlier = chunk[None, :] < chunk[:, None]
    expo = np.where(same, np.abs(diff), diff)
    dmask = np.where((same | earlier)[None], np.exp(log_gamma[:, None, None] * expo[None]), 0.0)
    rowdec = np.exp(log_gamma[None, :] * (t[:, None] + 1.0))
    keydec = np.exp(log_gamma[None, :] * (RET_TILE - 1.0 - t[:, None]))
    tiledec = np.exp(log_gamma * RET_TILE)
    rep = lambda a: np.repeat(a, RET_KDIM, axis=-1)
    width = RET_HEADS * RET_KDIM
    tiledec_full = np.broadcast_to(rep(tiledec[None, :]).T, (width, width))
    return (jnp.asarray(dmask, F32), jnp.asarray(rep(rowdec), F32), jnp.asarray(rep(keydec), F32),
            jnp.asarray(tiledec_full, F32))


def _retention(x3, w, cos, sin):
    bsz, seq, _ = x3.shape
    width = RET_HEADS * RET_KDIM
    dmask, rowdec, keydec, tiledec = _retention_tables()
    tile = lambda last: pl.BlockSpec((None, RET_TILE, last), lambda b, s: (b, s, 0))
    return pl.pallas_call(
        _ret_kernel,
        grid=(bsz, seq // RET_TILE),
        in_specs=[tile(D_MODEL), _const_spec(w.shape), tile(128), tile(128),
                  _const_spec(dmask.shape), _const_spec(rowdec.shape), _const_spec(keydec.shape),
                  _const_spec(tiledec.shape)],
        out_specs=tile(BRANCH_WIDTH),
        out_shape=jax.ShapeDtypeStruct((bsz, seq, BRANCH_WIDTH), BF16),
        scratch_shapes=[pltpu.VMEM((width, width), F32)],
        compiler_params=_params(2, "arbitrary"),
        name="retention_mixer",
    )(x3, w, cos, sin, dmask, rowdec, keydec, tiledec)


def _pool_kernel(x_ref, w_ref, wp_ref, scale_ref, o_ref, zext_ref):
    s = pl.program_id(1)

    @pl.when(s == 0)
    def _():
        zext_ref[0:POOL_HALO, :] = jnp.zeros((POOL_HALO, BRANCH_WIDTH), F32)

    z = _dot(x_ref[...].astype(BF16), w_ref[...])
    zext_ref[POOL_HALO:POOL_HALO + TM, :] = z
    group = lax.broadcasted_iota(jnp.int32, (TM, BRANCH_WIDTH), 1) // (BRANCH_WIDTH // len(POOL_WINDOWS))
    acc = z
    win = jnp.zeros_like(z)
    window = jnp.zeros((TM, BRANCH_WIDTH), jnp.int32)
    for d in range(1, POOL_HALO):
        acc = acc + zext_ref[POOL_HALO - d:POOL_HALO - d + TM, :]
        if d + 1 in POOL_WINDOWS:
            gi = POOL_WINDOWS.index(d + 1)
            win = jnp.where(group == gi, acc, win)
            window = jnp.where(group == gi, d + 1, window)
    t = s * TM + lax.broadcasted_iota(jnp.int32, (TM, BRANCH_WIDTH), 0)
    count = jnp.minimum(t + 1, window).astype(F32)
    pooled = win / count - z
    y = _dot(pooled.astype(BF16), wp_ref[...]) * scale_ref[...]
    o_ref[...] = y.astype(o_ref.dtype)
    zext_ref[0:POOL_HALO, :] = zext_ref[TM:TM + POOL_HALO, :]


def _pool(x3, w, wp, scale):
    bsz, seq, _ = x3.shape
    tile = lambda last: pl.BlockSpec((None, TM, last), lambda b, s: (b, s, 0))
    return pl.pallas_call(
        _pool_kernel,
        grid=(bsz, seq // TM),
        in_specs=[tile(D_MODEL), _const_spec(w.shape), _const_spec(wp.shape), _const_spec(scale.shape)],
        out_specs=tile(BRANCH_WIDTH),
        out_shape=jax.ShapeDtypeStruct((bsz, seq, BRANCH_WIDTH), BF16),
        scratch_shapes=[pltpu.VMEM((TM + POOL_HALO, BRANCH_WIDTH), F32)],
        compiler_params=_params(2, "arbitrary"),
        name="pool_mixer",
    )(x3, w, wp, scale)


def _rms_norm(x, g):
    return x * lax.rsqrt(jnp.mean(x * x, axis=-1, keepdims=True) + RMS_EPS) * g


def _mla_pre_kernel(x_ref, w_ref, ck_ref, sk_ref, qg_ref, kvg_ref, wq1_ref, wq2_ref, wk_ref, wvt_ref,
                    q_ref, k_ref, vt_ref):
    xb = x_ref[...].astype(BF16)
    proj = _dot(xb, w_ref[...])
    cq = _rms_norm(proj[:, :MLA_Q_RANK], qg_ref[...]).astype(BF16)
    ckv = _rms_norm(proj[:, MLA_Q_RANK:MLA_Q_RANK + MLA_KV_RANK], kvg_ref[...]).astype(BF16)
    off = MLA_Q_RANK + MLA_KV_RANK
    ck = ck_ref[...]
    sk = sk_ref[...]
    k_pe = proj[:, off:off + HEAD_PAD] * ck + proj[:, off + HEAD_PAD:off + 2 * HEAD_PAD] * sk
    lane = lax.broadcasted_iota(jnp.int32, ck.shape, 1)
    scale = MLA_QK ** -0.5 * math.log2(math.e)
    cq_tab = (ck + jnp.where(lane < MLA_NOPE, 1.0, 0.0)) * scale
    sq_tab = sk * scale
    tile4 = lambda a: jnp.concatenate([a] * MLA_HEADS, axis=1)
    q = _dot(cq, wq1_ref[...]) * tile4(cq_tab) + _dot(cq, wq2_ref[...]) * tile4(sq_tab)
    k = _dot(ckv, wk_ref[...]) + tile4(k_pe)
    q_ref[...] = q.astype(q_ref.dtype)
    k_ref[...] = k.astype(k_ref.dtype)
    for t in range(TM // ATT_TK):
        vt_ref[t] = _dot_nt(wvt_ref[...], ckv[t * ATT_TK:(t + 1) * ATT_TK]).astype(vt_ref.dtype)


def _mla_pre(x, w, ck, sk, qg, kvg, wq1, wq2, wk, wvt):
    n = x.shape[0]
    row = lambda last: pl.BlockSpec((TM, last), lambda i: (i, 0))
    consts = [w, qg, kvg, wq1, wq2, wk, wvt]
    return pl.pallas_call(
        _mla_pre_kernel,
        grid=(n // TM,),
        in_specs=[row(D_MODEL), _const_spec(w.shape), row(HEAD_PAD), row(HEAD_PAD)]
                 + [_const_spec(a.shape) for a in consts[1:]],
        out_specs=[row(MLA_HEADS * HEAD_PAD), row(MLA_HEADS * HEAD_PAD),
                   pl.BlockSpec((TM // ATT_TK, MLA_HEADS * MLA_V, ATT_TK), lambda i: (i, 0, 0))],
        out_shape=[jax.ShapeDtypeStruct((n, MLA_HEADS * HEAD_PAD), BF16),
                   jax.ShapeDtypeStruct((n, MLA_HEADS * HEAD_PAD), BF16),
                   jax.ShapeDtypeStruct((n // ATT_TK, MLA_HEADS * MLA_V, ATT_TK), BF16)],
        compiler_params=_params(1),
        name="mla_project",
    )(x, w, ck, sk, qg, kvg, wq1, wq2, wk, wvt)


def _attn_kernel(q_ref, k_ref, vt_ref, o_ref, st_ref):
    qi = pl.program_id(1)
    key_chunk = (qi * ATT_TK + lax.broadcasted_iota(jnp.int32, (ATT_TK, ATT_TQ), 0)) // CHUNK
    qry_chunk = (qi * ATT_TQ + lax.broadcasted_iota(jnp.int32, (ATT_TK, ATT_TQ), 1)) // CHUNK
    diag_mask = key_chunk <= qry_chunk
    heads = range(MLA_HEADS)
    qs = [q_ref[:, h * HEAD_PAD:(h + 1) * HEAD_PAD] for h in heads]

    def scores(j, slot):
        rows = pl.ds(pl.multiple_of(j * ATT_TK, ATT_TK), ATT_TK)
        for h in heads:
            st_ref[slot, h] = _dot_nt(k_ref[rows, h * HEAD_PAD:(h + 1) * HEAD_PAD], qs[h])

    def softmax_pv(j, slot, carry, masked):
        out = []
        for h in heads:
            m, l, acc = carry[h]
            st = st_ref[slot, h]
            if masked:
                st = jnp.where(diag_mask, st, -jnp.inf)
            m_new = jnp.maximum(m, jnp.max(st, axis=0, keepdims=True))
            a = jnp.exp2(m - m_new)
            p = jnp.exp2(st - m_new)
            l = a * l + jnp.sum(p, axis=0, keepdims=True)
            acc = a * acc + _dot(vt_ref[j, h * MLA_V:(h + 1) * MLA_V, :], p.astype(BF16))
            out.append((m_new, l, acc))
        return tuple(out)

    def finish(carry):
        out_t = jnp.concatenate([acc / l for _, l, acc in carry], axis=0)
        o_ref[...] = out_t.T.astype(o_ref.dtype)

    def pair(i, carry):
        scores(2 * i + 1, 1)
        carry = softmax_pv(2 * i, 0, carry, False)
        scores(2 * i + 2, 0)
        return softmax_pv(2 * i + 1, 1, carry, False)

    init = tuple((jnp.full((1, ATT_TQ), -jnp.inf, F32), jnp.zeros((1, ATT_TQ), F32),
                  jnp.zeros((MLA_V, ATT_TQ), F32)) for _ in heads)
    scores(0, 0)
    carry = lax.fori_loop(0, qi // 2, pair, init)

    @pl.when(qi % 2 == 0)
    def _():
        finish(softmax_pv(qi, 0, carry, True))

    @pl.when(qi % 2 == 1)
    def _():
        scores(qi, 1)
        finish(softmax_pv(qi, 1, softmax_pv(qi - 1, 0, carry, False), True))


def _attention(q3, k3, vt4):
    bsz, seq, _ = q3.shape
    return pl.pallas_call(
        _attn_kernel,
        grid=(bsz, seq // ATT_TQ),
        in_specs=[pl.BlockSpec((None, ATT_TQ, MLA_HEADS * HEAD_PAD), lambda b, i: (b, i, 0)),
                  pl.BlockSpec((None, seq, MLA_HEADS * HEAD_PAD), lambda b, i: (b, 0, 0)),
                  pl.BlockSpec((None, seq // ATT_TK, MLA_HEADS * MLA_V, ATT_TK), lambda b, i: (b, 0, 0, 0))],
        out_specs=pl.BlockSpec((None, ATT_TQ, MLA_HEADS * MLA_V), lambda b, i: (b, i, 0)),
        out_shape=jax.ShapeDtypeStruct((bsz, seq, MLA_HEADS * MLA_V), BF16),
        scratch_shapes=[pltpu.VMEM((2, MLA_HEADS, ATT_TK, ATT_TQ), F32)],
        compiler_params=_params(2),
        name="mla_attention",
    )(q3, k3, vt4)


def _merge_kernel(x_ref, ya_ref, yb_ref, yc_ref, yd_ref, wg_ref, wb_ref, wo_ref, g_ref, b_ref, o_ref):
    x = x_ref[...]
    xb = x.astype(BF16)
    merged = None
    for n, y_ref in enumerate((ya_ref, yb_ref, yc_ref, yd_ref)):
        gate = jax.nn.sigmoid(_dot(xb, wg_ref[:, n * D_MODEL:(n + 1) * D_MODEL]))
        term = gate * _dot(y_ref[...], wb_ref[n])
        merged = term if merged is None else merged + term
    mix = _dot(merged.astype(BF16), wo_ref[...])
    o_ref[...] = _layer_norm(ALPHA * x + mix, g_ref[...], b_ref[...], LN_EPS)


def _merge(x, ys, wg, wb, wo, g, b):
    n = x.shape[0]
    row = lambda last: pl.BlockSpec((TM, last), lambda i: (i, 0))
    return pl.pallas_call(
        _merge_kernel,
        grid=(n // TM,),
        in_specs=[row(D_MODEL)] + [row(BRANCH_WIDTH)] * N_BRANCH
                 + [_const_spec(a.shape) for a in (wg, wb, wo, g, b)],
        out_specs=row(D_MODEL),
        out_shape=jax.ShapeDtypeStruct((n, D_MODEL), F32),
        compiler_params=_params(1),
        name="merge_ln",
    )(x, *ys, wg, wb, wo, g, b)


def _rope_tables(positions):
    pos = positions.astype(F32)[..., None]
    ang_r = pos * (ROPE_BASE ** (-jnp.arange(0, RET_KDIM, 2, dtype=F32) / RET_KDIM))
    cr, sr = jnp.cos(ang_r), jnp.sin(ang_r)
    ret_cos = jnp.tile(jnp.concatenate([cr, cr], axis=-1), (1, 1, 2))
    ret_sin = jnp.tile(jnp.concatenate([-sr, sr], axis=-1), (1, 1, 2))
    ang_m = pos * (ROPE_BASE ** (-jnp.arange(0, MLA_ROPE, 2, dtype=F32) / MLA_ROPE))
    cm, sm = jnp.cos(ang_m), jnp.sin(ang_m)
    pad = lambda a: jnp.pad(a, ((0, 0), (0, 0), (MLA_NOPE, HEAD_PAD - MLA_QK)))
    mla_cos = pad(jnp.concatenate([cm, cm], axis=-1))
    mla_sin = pad(jnp.concatenate([-sm, sm], axis=-1))
    return ret_cos, ret_sin, mla_cos, mla_sin


def _swap_cols(w):
    half = w.shape[-1] // 2
    return jnp.concatenate([w[..., half:], w[..., :half]], axis=-1)


def _mla_weights(w_in, w_uq, w_ukv):
    rows = w_in.shape[0]
    w_kr = w_in[:, _OFF_KROPE:_OFF_KROPE + MLA_ROPE]
    place = lambda w: jnp.pad(w, ((0, 0), (MLA_NOPE, HEAD_PAD - MLA_QK)))
    w_proj = jnp.concatenate([w_in[:, _OFF_CQ:_OFF_KROPE], place(w_kr), place(_swap_cols(w_kr))], axis=1)
    uq = w_uq.reshape(MLA_Q_RANK, MLA_HEADS, MLA_QK)
    q_nope, q_rope = uq[..., :MLA_NOPE], uq[..., MLA_NOPE:]
    zq = jnp.zeros((MLA_Q_RANK, MLA_HEADS, HEAD_PAD - MLA_QK), w_uq.dtype)
    wq1 = jnp.concatenate([q_nope, q_rope, zq], axis=-1).reshape(MLA_Q_RANK, MLA_HEADS * HEAD_PAD)
    wq2 = jnp.concatenate([jnp.zeros_like(q_nope), _swap_cols(q_rope), zq], axis=-1)
    wq2 = wq2.reshape(MLA_Q_RANK, MLA_HEADS * HEAD_PAD)
    ukv = w_ukv.reshape(MLA_KV_RANK, MLA_HEADS, MLA_NOPE + MLA_V)
    k_nope, v = ukv[..., :MLA_NOPE], ukv[..., MLA_NOPE:]
    wk = jnp.concatenate([k_nope, jnp.zeros((MLA_KV_RANK, MLA_HEADS, HEAD_PAD - MLA_NOPE), w_ukv.dtype)],
                         axis=-1).reshape(MLA_KV_RANK, MLA_HEADS * HEAD_PAD)
    wvt = v.reshape(MLA_KV_RANK, MLA_HEADS * MLA_V).T
    return [a.astype(BF16) for a in (w_proj, wq1, wq2, wk, wvt)]


def kernel(x, p, positions, ffn1_up, ffn1_down, ln1_g, ln1_b, w_in, sgu_ln_g, sgu_ln_b, sgu_w, sgu_b,
           pool_w, pool_scale, mla_q_norm, mla_kv_norm, mla_w_uq, mla_w_ukv, w_branch, w_out,
           ln2_g, ln2_b, ffn2_up, ffn2_down, w_ple_gate, w_ple, ln3_g, ln3_b):
    bsz, seq, dm = x.shape
    n = bsz * seq
    ret_cos, ret_sin, mla_cos, mla_sin = _rope_tables(positions)
    mla_cos2, mla_sin2 = mla_cos.reshape(n, HEAD_PAD), mla_sin.reshape(n, HEAD_PAD)
    row = lambda a: a.reshape(1, -1)
    bf = lambda a: a.astype(BF16)

    h = x.reshape(n, dm)
    for i in range(DEPTH):
        h = _ffn_ln(h, bf(ffn1_up[i][:, :D_FF]), bf(ffn1_up[i][:, D_FF:]), bf(ffn1_down[i]),
                    row(ln1_g[i]), row(ln1_b[i]))
        h3 = h.reshape(bsz, seq, dm)
        wi = w_in[i]
        y_a = _sgu(h, bf(wi[:, _OFF_SGU:_OFF_RET]), row(sgu_ln_g[i]), row(sgu_ln_b[i]), sgu_w[i],
                   jnp.repeat(sgu_b[i].T, SGU_WIDTH // SGU_GROUPS, axis=1))
        y_b = _retention(h3, bf(wi[:, _OFF_RET:_OFF_POOL]), ret_cos, ret_sin)
        y_c = _pool(h3, bf(wi[:, _OFF_POOL:_OFF_CQ]),
                    bf(jax.scipy.linalg.block_diag(*[pool_w[i, g] for g in range(len(POOL_WINDOWS))])),
                    row(pool_scale[i]))
        w_proj, wq1, wq2, wk, wvt = _mla_weights(wi, mla_w_uq[i], mla_w_ukv[i])
        q, k, vt = _mla_pre(h, w_proj, mla_cos2, mla_sin2, row(mla_q_norm[i]), row(mla_kv_norm[i]),
                            wq1, wq2, wk, wvt)
        y_d = _attention(q.reshape(bsz, seq, -1), k.reshape(bsz, seq, -1),
                         vt.reshape(bsz, seq // ATT_TK, MLA_HEADS * MLA_V, ATT_TK))
        ys = (y_a, y_b.reshape(n, -1), y_c.reshape(n, -1), y_d.reshape(n, -1))
        h = _merge(h, ys, bf(wi[:, _OFF_GATE:]), bf(w_branch[i]), bf(w_out[i]), row(ln2_g[i]), row(ln2_b[i]))
        h = _ffn_ple_ln(h, p[i].reshape(n, PLE_DIM), bf(ffn2_up[i][:, :D_FF]), bf(ffn2_up[i][:, D_FF:]),
                        bf(ffn2_down[i]), bf(w_ple_gate[i]), bf(w_ple[i]), row(ln3_g[i]), row(ln3_b[i]))
    return h.reshape(bsz, seq, dm)
```

```python
import math

import numpy as np
import jax
import jax.numpy as jnp
from jax import lax
from jax.experimental import pallas as pl
from jax.experimental.pallas import tpu as pltpu

D_MODEL = 1024
DEPTH = 2
CHUNK = 64
SGU_WIDTH = 256
SGU_BLOCK = 128
SGU_GROUPS = 4
RET_HEADS = 4
RET_KDIM = 64
POOL_WINDOWS = (2, 4, 8, 16)
POOL_HALO = 16
MLA_HEADS = 4
MLA_NOPE = 64
MLA_ROPE = 32
MLA_QK = MLA_NOPE + MLA_ROPE
MLA_V = 64
MLA_Q_RANK = 256
MLA_KV_RANK = 128
HEAD_PAD = 128
ROPE_BASE = 10000.0
N_BRANCH = 4
BRANCH_WIDTH = 256
D_FF = 2816
PLE_DIM = 256
ALPHA = (2 * DEPTH) ** 0.25
LN_EPS = 1e-5
RMS_EPS = 1e-6
GN_EPS = 1e-5

_OFF_SGU = 0
_OFF_RET = 512
_OFF_POOL = 1536
_OFF_CQ = 1792
_OFF_GATE = 2208
MIX_COLS = 2304

TM = 512
RET_TILE = 256
ATT_TQ = 256
ATT_TK = 256
VMEM_LIMIT = 56 * 1024 * 1024

BF16 = jnp.bfloat16
F32 = jnp.float32


def _dot(a, b):
    return jnp.dot(a, b, preferred_element_type=F32)


def _dot_nt(a, b):
    return lax.dot_general(a, b, (((1,), (1,)), ((), ())), preferred_element_type=F32)


def _dot_tn(a, b):
    return lax.dot_general(a, b, (((0,), (0,)), ((), ())), preferred_element_type=F32)


def _layer_norm(r, g, b, eps):
    mu = jnp.mean(r, axis=-1, keepdims=True)
    d = r - mu
    var = jnp.mean(d * d, axis=-1, keepdims=True)
    return d * lax.rsqrt(var + eps) * g + b


def _const_spec(shape):
    zeros = (0,) * len(shape)
    return pl.BlockSpec(shape, lambda *_: zeros, pipeline_mode=pl.Buffered(1))


def _layer_spec(arr, layer):
    index = (layer,) + (0,) * (arr.ndim - 1)
    return pl.BlockSpec((None,) + arr.shape[1:], lambda *_: index, pipeline_mode=pl.Buffered(1))


def _params(n_grid, semantics="parallel"):
    return pltpu.CompilerParams(
        dimension_semantics=(semantics,) * n_grid, vmem_limit_bytes=VMEM_LIMIT)


def _ffn_body(x, up_ref, down_ref):
    xb = x.astype(BF16)
    a = _dot(xb, up_ref[:, :D_FF])
    b = _dot(xb, up_ref[:, D_FF:])
    h = (a * jax.nn.sigmoid(a) * b).astype(BF16)
    return xb, _dot(h, down_ref[...])


def _ffn_ln_kernel(x_ref, up_ref, down_ref, g_ref, b_ref, o_ref):
    x = x_ref[...]
    _, y = _ffn_body(x, up_ref, down_ref)
    o_ref[...] = _layer_norm(ALPHA * x + 0.5 * y, g_ref[...], b_ref[...], LN_EPS)


def _ffn_ple_ln_kernel(x_ref, p_ref, up_ref, down_ref, wpg_ref, wp_ref, g_ref, b_ref, o_ref):
    x = x_ref[...]
    xb, y = _ffn_body(x, up_ref, down_ref)
    ple = jax.nn.sigmoid(_dot(xb, wpg_ref[...])) * _dot(p_ref[...].astype(BF16), wp_ref[...])
    o_ref[...] = _layer_norm(ALPHA * x + 0.5 * y + ple, g_ref[...], b_ref[...], LN_EPS)


def _ffn_ln(layer, x, up, down, g, b):
    n = x.shape[0]
    row = pl.BlockSpec((TM, D_MODEL), lambda i: (i, 0))
    return pl.pallas_call(
        _ffn_ln_kernel,
        grid=(n // TM,),
        in_specs=[row] + [_layer_spec(a, layer) for a in (up, down, g, b)],
        out_specs=row,
        out_shape=jax.ShapeDtypeStruct((n, D_MODEL), F32),
        compiler_params=_params(1),
        name="ffn_ln",
    )(x, up, down, g, b)


def _ffn_ple_ln(layer, x, p, up, down, wpg, wp, g, b):
    n = x.shape[0]
    row = pl.BlockSpec((TM, D_MODEL), lambda i: (i, 0))
    return pl.pallas_call(
        _ffn_ple_ln_kernel,
        grid=(n // TM,),
        in_specs=[row, pl.BlockSpec((None, TM, PLE_DIM), lambda i: (layer, i, 0))]
                 + [_layer_spec(a, layer) for a in (up, down, wpg, wp, g, b)],
        out_specs=row,
        out_shape=jax.ShapeDtypeStruct((n, D_MODEL), F32),
        compiler_params=_params(1),
        name="ffn_ple_ln",
    )(x, p, up, down, wpg, wp, g, b)


def _sgu_kernel(x_ref, w_ref, lng_ref, lnb_ref, ws_ref, bias_ref, o_ref):
    xb = x_ref[...].astype(BF16)
    uv = _dot(xb, w_ref[:, _OFF_SGU:_OFF_RET])
    u = jax.nn.gelu(uv[:, :SGU_WIDTH])
    v = _layer_norm(jax.nn.gelu(uv[:, SGU_WIDTH:]), lng_ref[...], lnb_ref[...], LN_EPS)
    vb = v.astype(BF16)
    t_row = lax.broadcasted_iota(jnp.int32, (SGU_BLOCK, SGU_BLOCK), 0)
    t_col = lax.broadcasted_iota(jnp.int32, (SGU_BLOCK, SGU_BLOCK), 1)
    causal = t_row >= t_col
    w_groups = [jnp.where(causal, ws_ref[g], 0.0).astype(BF16) for g in range(SGU_GROUPS)]
    group = lax.broadcasted_iota(jnp.int32, (SGU_BLOCK, SGU_WIDTH), 1) // (SGU_WIDTH // SGU_GROUPS)
    bias = bias_ref[...]
    for blk in range(TM // SGU_BLOCK):
        rows = slice(blk * SGU_BLOCK, (blk + 1) * SGU_BLOCK)
        v_blk = vb[rows]
        mixed = bias
        for g in range(SGU_GROUPS):
            mixed = mixed + jnp.where(group == g, _dot(w_groups[g], v_blk), 0.0)
        o_ref[rows, :] = (u[rows] * mixed).astype(o_ref.dtype)


def _sgu(layer, x, w, lng, lnb, ws, bias):
    n = x.shape[0]
    return pl.pallas_call(
        _sgu_kernel,
        grid=(n // TM,),
        in_specs=[pl.BlockSpec((TM, D_MODEL), lambda i: (i, 0))]
                 + [_layer_spec(a, layer) for a in (w, lng, lnb, ws, bias)],
        out_specs=pl.BlockSpec((TM, BRANCH_WIDTH), lambda i: (i, 0)),
        out_shape=jax.ShapeDtypeStruct((n, BRANCH_WIDTH), BF16),
        compiler_params=_params(1),
        name="sgu_mixer",
    )(x, w, lng, lnb, ws, bias)


def _swap_halves(x, half):
    width = x.shape[-1]
    lane = lax.broadcasted_iota(jnp.int32, x.shape, x.ndim - 1)
    first = (lane % (2 * half)) < half
    return jnp.where(first, pltpu.roll(x, width - half, x.ndim - 1), pltpu.roll(x, half, x.ndim - 1))


def _ret_kernel(x_ref, w_ref, cos_ref, sin_ref, dmask_ref, rowdec_ref, keydec_ref, tiledec_ref,
                o_ref, state_ref):
    width = RET_HEADS * RET_KDIM

    @pl.when(pl.program_id(1) == 0)
    def _():
        state_ref[...] = jnp.zeros_like(state_ref)

    xb = x_ref[...].astype(BF16)
    proj = _dot(xb, w_ref[:, _OFF_RET:_OFF_POOL])
    cos = jnp.concatenate([cos_ref[...], cos_ref[...]], axis=1)
    sin = jnp.concatenate([sin_ref[...], sin_ref[...]], axis=1)
    q = proj[:, :width]
    k = proj[:, width:2 * width]
    q = q * cos + _swap_halves(q, RET_KDIM // 2) * sin
    k = (k * cos + _swap_halves(k, RET_KDIM // 2) * sin) * RET_KDIM ** -0.5
    v = proj[:, 2 * width:3 * width]
    gate = proj[:, 3 * width:]
    vb = v.astype(BF16)
    kb = k.astype(BF16)

    head = lax.broadcasted_iota(jnp.int32, (RET_TILE, width), 1) // RET_KDIM
    y = _dot(q.astype(BF16), state_ref[...].astype(BF16)) * rowdec_ref[...]
    for h in range(RET_HEADS):
        qh = jnp.where(head == h, q, 0.0).astype(BF16)
        scores = _dot_nt(qh, kb) * dmask_ref[h]
        y = y + jnp.where(head == h, _dot(scores.astype(BF16), vb), 0.0)

    kd = (k * keydec_ref[...]).astype(BF16)
    row_head = lax.broadcasted_iota(jnp.int32, (width, width), 0) // RET_KDIM
    col_head = lax.broadcasted_iota(jnp.int32, (width, width), 1) // RET_KDIM
    kv = jnp.where(row_head == col_head, _dot_tn(kd, vb), 0.0)
    state_ref[...] = state_ref[...] * tiledec_ref[...] + kv

    inv = 1.0 / RET_KDIM
    mu = jnp.zeros_like(y)
    for h in range(RET_HEADS):
        s = jnp.sum(jnp.where(head == h, y, 0.0), axis=1, keepdims=True) * inv
        mu = jnp.where(head == h, s, mu)
    d = y - mu
    var = jnp.zeros_like(y)
    for h in range(RET_HEADS):
        s = jnp.sum(jnp.where(head == h, d * d, 0.0), axis=1, keepdims=True) * inv
        var = jnp.where(head == h, s, var)
    yn = d * lax.rsqrt(var + GN_EPS)
    o_ref[...] = (gate * jax.nn.sigmoid(gate) * yn).astype(o_ref.dtype)


def _retention_tables():
    heads = np.arange(RET_HEADS, dtype=np.float64)
    log_gamma = np.log1p(-np.exp2(-5.0 - heads))
    t = np.arange(RET_TILE)
    chunk = t // CHUNK
    diff = (t[:, None] - t[None, :]).astype(np.float64)
    same = chunk[:, None] == chunk[None, :]
    earlier = chunk[None, :] < chunk[:, None]
    expo = np.where(same, np.abs(diff), diff)
    dmask = np.where((same | earlier)[None], np.exp(log_gamma[:, None, None] * expo[None]), 0.0)
    rowdec = np.exp(log_gamma[None, :] * (t[:, None] + 1.0))
    keydec = np.exp(log_gamma[None, :] * (RET_TILE - 1.0 - t[:, None]))
    tiledec = np.exp(log_gamma * RET_TILE)
    rep = lambda a: np.repeat(a, RET_KDIM, axis=-1)
    width = RET_HEADS * RET_KDIM
    tiledec_full = np.broadcast_to(rep(tiledec[None, :]).T, (width, width))
    return (jnp.asarray(dmask, F32), jnp.asarray(rep(rowdec), F32), jnp.asarray(rep(keydec), F32),
            jnp.asarray(tiledec_full, F32))


def _retention(layer, x3, w, cos, sin):
    bsz, seq, _ = x3.shape
    width = RET_HEADS * RET_KDIM
    dmask, rowdec, keydec, tiledec = _retention_tables()
    tile = lambda last: pl.BlockSpec((None, RET_TILE, last), lambda b, s: (b, s, 0))
    return pl.pallas_call(
        _ret_kernel,
        grid=(bsz, seq // RET_TILE),
        in_specs=[tile(D_MODEL), _layer_spec(w, layer), tile(128), tile(128),
                  _const_spec(dmask.shape), _const_spec(rowdec.shape), _const_spec(keydec.shape),
                  _const_spec(tiledec.shape)],
        out_specs=tile(BRANCH_WIDTH),
        out_shape=jax.ShapeDtypeStruct((bsz, seq, BRANCH_WIDTH), BF16),
        scratch_shapes=[pltpu.VMEM((width, width), F32)],
        compiler_params=_params(2, "arbitrary"),
        name="retention_mixer",
    )(x3, w, cos, sin, dmask, rowdec, keydec, tiledec)


def _pool_kernel(x_ref, w_ref, wp_ref, scale_ref, o_ref, zext_ref):
    s = pl.program_id(1)

    @pl.when(s == 0)
    def _():
        zext_ref[0:POOL_HALO, :] = jnp.zeros((POOL_HALO, BRANCH_WIDTH), F32)

    z = _dot(x_ref[...].astype(BF16), w_ref[:, _OFF_POOL:_OFF_CQ])
    zext_ref[POOL_HALO:POOL_HALO + TM, :] = z
    group = lax.broadcasted_iota(jnp.int32, (TM, BRANCH_WIDTH), 1) // (BRANCH_WIDTH // len(POOL_WINDOWS))
    acc = z
    win = jnp.zeros_like(z)
    window = jnp.zeros((TM, BRANCH_WIDTH), jnp.int32)
    for d in range(1, POOL_HALO):
        acc = acc + zext_ref[POOL_HALO - d:POOL_HALO - d + TM, :]
        if d + 1 in POOL_WINDOWS:
            gi = POOL_WINDOWS.index(d + 1)
            win = jnp.where(group == gi, acc, win)
            window = jnp.where(group == gi, d + 1, window)
    t = s * TM + lax.broadcasted_iota(jnp.int32, (TM, BRANCH_WIDTH), 0)
    count = jnp.minimum(t + 1, window).astype(F32)
    pooled = win / count - z
    y = _dot(pooled.astype(BF16), wp_ref[...]) * scale_ref[...]
    o_ref[...] = y.astype(o_ref.dtype)
    zext_ref[0:POOL_HALO, :] = zext_ref[TM:TM + POOL_HALO, :]


def _pool(layer, x3, w, wp, scale):
    bsz, seq, _ = x3.shape
    tile = lambda last: pl.BlockSpec((None, TM, last), lambda b, s: (b, s, 0))
    return pl.pallas_call(
        _pool_kernel,
        grid=(bsz, seq // TM),
        in_specs=[tile(D_MODEL)] + [_layer_spec(a, layer) for a in (w, wp, scale)],
        out_specs=tile(BRANCH_WIDTH),
        out_shape=jax.ShapeDtypeStruct((bsz, seq, BRANCH_WIDTH), BF16),
        scratch_shapes=[pltpu.VMEM((TM + POOL_HALO, BRANCH_WIDTH), F32)],
        compiler_params=_params(2, "arbitrary"),
        name="pool_mixer",
    )(x3, w, wp, scale)


def _rms_norm(x, g):
    return x * lax.rsqrt(jnp.mean(x * x, axis=-1, keepdims=True) + RMS_EPS) * g


def _mla_pre_kernel(x_ref, w_ref, ck_ref, sk_ref, qg_ref, kvg_ref, wq1_ref, wq2_ref, wk_ref, wvt_ref,
                    q_ref, k_ref, vt_ref):
    xb = x_ref[...].astype(BF16)
    proj = _dot(xb, w_ref[:, _OFF_CQ:MIX_COLS])
    cq = _rms_norm(proj[:, :MLA_Q_RANK], qg_ref[...]).astype(BF16)
    ckv = _rms_norm(proj[:, MLA_Q_RANK:MLA_Q_RANK + MLA_KV_RANK], kvg_ref[...]).astype(BF16)
    k_raw = proj[:, MLA_Q_RANK + MLA_KV_RANK:]
    k_pe = pltpu.roll(k_raw * ck_ref[...] + _swap_halves(k_raw, MLA_ROPE // 2) * sk_ref[...], MLA_NOPE, 1)
    ck = pltpu.roll(ck_ref[...], MLA_NOPE, 1)
    sk = pltpu.roll(sk_ref[...], MLA_NOPE, 1)
    lane = lax.broadcasted_iota(jnp.int32, ck.shape, 1)
    scale = MLA_QK ** -0.5 * math.log2(math.e)
    cq_tab = (ck + jnp.where(lane < MLA_NOPE, 1.0, 0.0)) * scale
    sq_tab = sk * scale
    tile4 = lambda a: jnp.concatenate([a] * MLA_HEADS, axis=1)
    q = _dot(cq, wq1_ref[...]) * tile4(cq_tab) + _dot(cq, wq2_ref[...]) * tile4(sq_tab)
    k = _dot(ckv, wk_ref[...]) + tile4(k_pe)
    q_ref[...] = q.astype(q_ref.dtype)
    k_ref[...] = k.astype(k_ref.dtype)
    for t in range(TM // ATT_TK):
        vt_ref[t] = _dot_nt(wvt_ref[...], ckv[t * ATT_TK:(t + 1) * ATT_TK]).astype(vt_ref.dtype)


def _mla_pre(layer, x, w, ck, sk, qg, kvg, wq1, wq2, wk, wvt):
    n = x.shape[0]
    row = lambda last: pl.BlockSpec((TM, last), lambda i: (i, 0))
    return pl.pallas_call(
        _mla_pre_kernel,
        grid=(n // TM,),
        in_specs=[row(D_MODEL), _layer_spec(w, layer), row(HEAD_PAD), row(HEAD_PAD)]
                 + [_layer_spec(a, layer) for a in (qg, kvg, wq1, wq2, wk, wvt)],
        out_specs=[row(MLA_HEADS * HEAD_PAD), row(MLA_HEADS * HEAD_PAD),
                   pl.BlockSpec((TM // ATT_TK, MLA_HEADS * MLA_V, ATT_TK), lambda i: (i, 0, 0))],
        out_shape=[jax.ShapeDtypeStruct((n, MLA_HEADS * HEAD_PAD), BF16),
                   jax.ShapeDtypeStruct((n, MLA_HEADS * HEAD_PAD), BF16),
                   jax.ShapeDtypeStruct((n // ATT_TK, MLA_HEADS * MLA_V, ATT_TK), BF16)],
        compiler_params=_params(1),
        name="mla_project",
    )(x, w, ck, sk, qg, kvg, wq1, wq2, wk, wvt)


def _attn_kernel(q_ref, k_ref, vt_ref, o_ref, st_ref):
    qi = pl.program_id(1)
    key_chunk = (qi * ATT_TK + lax.broadcasted_iota(jnp.int32, (ATT_TK, ATT_TQ), 0)) // CHUNK
    qry_chunk = (qi * ATT_TQ + lax.broadcasted_iota(jnp.int32, (ATT_TK, ATT_TQ), 1)) // CHUNK
    diag_mask = key_chunk <= qry_chunk
    heads = range(MLA_HEADS)
    qs = [q_ref[:, h * HEAD_PAD:(h + 1) * HEAD_PAD] for h in heads]

    def scores(j, slot):
        rows = pl.ds(pl.multiple_of(j * ATT_TK, ATT_TK), ATT_TK)
        for h in heads:
            st_ref[slot, h] = _dot_nt(k_ref[rows, h * HEAD_PAD:(h + 1) * HEAD_PAD], qs[h])

    def softmax_pv(j, slot, carry, masked):
        out = []
        for h in heads:
            m, l, acc = carry[h]
            st = st_ref[slot, h]
            if masked:
                st = jnp.where(diag_mask, st, -jnp.inf)
            m_new = jnp.maximum(m, jnp.max(st, axis=0, keepdims=True))
            a = jnp.exp2(m - m_new)
            p = jnp.exp2(st - m_new)
            l = a * l + jnp.sum(p, axis=0, keepdims=True)
            acc = a * acc + _dot(vt_ref[j, h * MLA_V:(h + 1) * MLA_V, :], p.astype(BF16))
            out.append((m_new, l, acc))
        return tuple(out)

    def finish(carry):
        out_t = jnp.concatenate([acc / l for _, l, acc in carry], axis=0)
        o_ref[...] = out_t.T.astype(o_ref.dtype)

    def pair(i, carry):
        scores(2 * i + 1, 1)
        carry = softmax_pv(2 * i, 0, carry, False)
        scores(2 * i + 2, 0)
        return softmax_pv(2 * i + 1, 1, carry, False)

    init = tuple((jnp.full((1, ATT_TQ), -jnp.inf, F32), jnp.zeros((1, ATT_TQ), F32),
                  jnp.zeros((MLA_V, ATT_TQ), F32)) for _ in heads)
    scores(0, 0)
    carry = lax.fori_loop(0, qi // 2, pair, init)

    @pl.when(qi % 2 == 0)
    def _():
        finish(softmax_pv(qi, 0, carry, True))

    @pl.when(qi % 2 == 1)
    def _():
        scores(qi, 1)
        finish(softmax_pv(qi, 1, softmax_pv(qi - 1, 0, carry, False), True))


def _attention(q3, k3, vt4):
    bsz, seq, _ = q3.shape
    return pl.pallas_call(
        _attn_kernel,
        grid=(bsz, seq // ATT_TQ),
        in_specs=[pl.BlockSpec((None, ATT_TQ, MLA_HEADS * HEAD_PAD), lambda b, i: (b, i, 0)),
                  pl.BlockSpec((None, seq, MLA_HEADS * HEAD_PAD), lambda b, i: (b, 0, 0)),
                  pl.BlockSpec((None, seq // ATT_TK, MLA_HEADS * MLA_V, ATT_TK), lambda b, i: (b, 0, 0, 0))],
        out_specs=pl.BlockSpec((None, ATT_TQ, MLA_HEADS * MLA_V), lambda b, i: (b, i, 0)),
        out_shape=jax.ShapeDtypeStruct((bsz, seq, MLA_HEADS * MLA_V), BF16),
        scratch_shapes=[pltpu.VMEM((2, MLA_HEADS, ATT_TK, ATT_TQ), F32)],
        compiler_params=_params(2),
        name="mla_attention",
    )(q3, k3, vt4)


def _merge_kernel(x_ref, ya_ref, yb_ref, yc_ref, yd_ref, wg_ref, wb_ref, wo_ref, g_ref, b_ref, o_ref):
    x = x_ref[...]
    xb = x.astype(BF16)
    merged = None
    for n, y_ref in enumerate((ya_ref, yb_ref, yc_ref, yd_ref)):
        gate = jax.nn.sigmoid(_dot(xb, wg_ref[:, n * D_MODEL:(n + 1) * D_MODEL]))
        term = gate * _dot(y_ref[...], wb_ref[n])
        merged = term if merged is None else merged + term
    mix = _dot(merged.astype(BF16), wo_ref[...])
    o_ref[...] = _layer_norm(ALPHA * x + mix, g_ref[...], b_ref[...], LN_EPS)


def _merge(layer, x, ys, wg, wb, wo, g, b):
    n = x.shape[0]
    row = lambda last: pl.BlockSpec((TM, last), lambda i: (i, 0))
    return pl.pallas_call(
        _merge_kernel,
        grid=(n // TM,),
        in_specs=[row(D_MODEL)] + [row(BRANCH_WIDTH)] * N_BRANCH
                 + [_layer_spec(a, layer) for a in (wg, wb, wo, g, b)],
        out_specs=row(D_MODEL),
        out_shape=jax.ShapeDtypeStruct((n, D_MODEL), F32),
        compiler_params=_params(1),
        name="merge_ln",
    )(x, *ys, wg, wb, wo, g, b)


def _rope_tables(positions):
    pos = positions.astype(F32)[..., None]
    ang_r = pos * (ROPE_BASE ** (-jnp.arange(0, RET_KDIM, 2, dtype=F32) / RET_KDIM))
    cr, sr = jnp.cos(ang_r), jnp.sin(ang_r)
    ret_cos = jnp.tile(jnp.concatenate([cr, cr], axis=-1), (1, 1, 2))
    ret_sin = jnp.tile(jnp.concatenate([-sr, sr], axis=-1), (1, 1, 2))
    ang_m = pos * (ROPE_BASE ** (-jnp.arange(0, MLA_ROPE, 2, dtype=F32) / MLA_ROPE))
    cm, sm = jnp.cos(ang_m), jnp.sin(ang_m)
    pad = lambda a: jnp.pad(a, ((0, 0), (0, 0), (0, HEAD_PAD - MLA_ROPE)))
    mla_cos = pad(jnp.concatenate([cm, cm], axis=-1))
    mla_sin = pad(jnp.concatenate([-sm, sm], axis=-1))
    return ret_cos, ret_sin, mla_cos, mla_sin


def _swap_cols(w):
    half = w.shape[-1] // 2
    return jnp.concatenate([w[..., half:], w[..., :half]], axis=-1)


def _mla_weights(w_uq, w_ukv):
    depth = w_uq.shape[0]
    uq = w_uq.reshape(depth, MLA_Q_RANK, MLA_HEADS, MLA_QK)
    q_nope, q_rope = uq[..., :MLA_NOPE], uq[..., MLA_NOPE:]
    zq = jnp.zeros((depth, MLA_Q_RANK, MLA_HEADS, HEAD_PAD - MLA_QK), w_uq.dtype)
    wq1 = jnp.concatenate([q_nope, q_rope, zq], axis=-1)
    wq2 = jnp.concatenate([jnp.zeros_like(q_nope), _swap_cols(q_rope), zq], axis=-1)
    ukv = w_ukv.reshape(depth, MLA_KV_RANK, MLA_HEADS, MLA_NOPE + MLA_V)
    k_nope, v = ukv[..., :MLA_NOPE], ukv[..., MLA_NOPE:]
    wk = jnp.concatenate([k_nope, jnp.zeros_like(k_nope)], axis=-1)
    wvt = jnp.swapaxes(v.reshape(depth, MLA_KV_RANK, MLA_HEADS * MLA_V), 1, 2)
    flat = lambda a: a.reshape(depth, a.shape[1], MLA_HEADS * HEAD_PAD).astype(BF16)
    return flat(wq1), flat(wq2), flat(wk), wvt.astype(BF16)


def kernel(x, p, positions, ffn1_up, ffn1_down, ln1_g, ln1_b, w_in, sgu_ln_g, sgu_ln_b, sgu_w, sgu_b,
           pool_w, pool_scale, mla_q_norm, mla_kv_norm, mla_w_uq, mla_w_ukv, w_branch, w_out,
           ln2_g, ln2_b, ffn2_up, ffn2_down, w_ple_gate, w_ple, ln3_g, ln3_b):
    bsz, seq, dm = x.shape
    n = bsz * seq
    ret_cos, ret_sin, mla_cos, mla_sin = _rope_tables(positions)
    mla_cos, mla_sin = mla_cos.reshape(n, HEAD_PAD), mla_sin.reshape(n, HEAD_PAD)
    bf = lambda a: a.astype(BF16)
    rows = lambda a: a[:, None, :]

    up1, down1, up2, down2 = bf(ffn1_up), bf(ffn1_down), bf(ffn2_up), bf(ffn2_down)
    w_mix = bf(w_in[:, :, :MIX_COLS])
    w_gate = bf(w_in[:, :, _OFF_GATE:])
    wb, wo, wpg, wp = bf(w_branch), bf(w_out), bf(w_ple_gate), bf(w_ple)
    wq1, wq2, wk, wvt = _mla_weights(mla_w_uq, mla_w_ukv)
    sgu_bias = jnp.repeat(jnp.swapaxes(sgu_b, 1, 2), SGU_WIDTH // SGU_GROUPS, axis=2)
    groups = len(POOL_WINDOWS)
    pool_bd = bf(jnp.einsum("lgcd,gh->lgchd", pool_w, jnp.eye(groups, dtype=pool_w.dtype))
                 .reshape(DEPTH, BRANCH_WIDTH, BRANCH_WIDTH))
    p2 = p.reshape(DEPTH, n, PLE_DIM)

    h = x.reshape(n, dm)
    for i in range(DEPTH):
        h = _ffn_ln(i, h, up1, down1, rows(ln1_g), rows(ln1_b))
        h3 = h.reshape(bsz, seq, dm)
        y_a = _sgu(i, h, w_mix, rows(sgu_ln_g), rows(sgu_ln_b), sgu_w, sgu_bias)
        y_b = _retention(i, h3, w_mix, ret_cos, ret_sin)
        y_c = _pool(i, h3, w_mix, pool_bd, rows(pool_scale))
        q, k, vt = _mla_pre(i, h, w_mix, mla_cos, mla_sin, rows(mla_q_norm), rows(mla_kv_norm),
                            wq1, wq2, wk, wvt)
        y_d = _attention(q.reshape(bsz, seq, -1), k.reshape(bsz, seq, -1),
                         vt.reshape(bsz, seq // ATT_TK, MLA_HEADS * MLA_V, ATT_TK))
        ys = (y_a, y_b.reshape(n, -1), y_c.reshape(n, -1), y_d.reshape(n, -1))
        h = _merge(i, h, ys, w_gate, wb, wo, rows(ln2_g), rows(ln2_b))
        h = _ffn_ple_ln(i, h, p2, up2, down2, wpg, wp, rows(ln3_g), rows(ln3_b))
    return h.reshape(bsz, seq, dm)
```

```python
import math

import numpy as np
import jax
import jax.numpy as jnp
from jax import lax
from jax.experimental import pallas as pl
from jax.experimental.pallas import tpu as pltpu

D_MODEL = 1024
DEPTH = 2
CHUNK = 64
SGU_WIDTH = 256
SGU_BLOCK = 128
SGU_GROUPS = 4
RET_HEADS = 4
RET_KDIM = 64
POOL_WINDOWS = (2, 4, 8, 16)
POOL_HALO = 16
MLA_HEADS = 4
MLA_NOPE = 64
MLA_ROPE = 32
MLA_QK = MLA_NOPE + MLA_ROPE
MLA_V = 64
MLA_Q_RANK = 256
MLA_KV_RANK = 128
HEAD_PAD = 128
ROPE_BASE = 10000.0
N_BRANCH = 4
BRANCH_WIDTH = 256
D_FF = 2816
PLE_DIM = 256
ALPHA = (2 * DEPTH) ** 0.25
LN_EPS = 1e-5
RMS_EPS = 1e-6
GN_EPS = 1e-5

_OFF_SGU = 0
_OFF_RET = 512
_OFF_POOL = 1536
_OFF_CQ = 1792
_OFF_GATE = 2208
MIX_COLS = 2304

TM = 512
RET_TILE = 256
ATT_TQ = 512
ATT_TK = 256
ATT_ONES = 16
VMEM_LIMIT = 56 * 1024 * 1024

BF16 = jnp.bfloat16
F32 = jnp.float32


def _dot(a, b):
    return jnp.dot(a, b, preferred_element_type=F32)


def _dot_nt(a, b):
    return lax.dot_general(a, b, (((1,), (1,)), ((), ())), preferred_element_type=F32)


def _dot_tn(a, b):
    return lax.dot_general(a, b, (((0,), (0,)), ((), ())), preferred_element_type=F32)


def _layer_norm(r, g, b, eps):
    mu = jnp.mean(r, axis=-1, keepdims=True)
    d = r - mu
    var = jnp.mean(d * d, axis=-1, keepdims=True)
    return d * lax.rsqrt(var + eps) * g + b


def _const_spec(shape):
    zeros = (0,) * len(shape)
    return pl.BlockSpec(shape, lambda *_: zeros, pipeline_mode=pl.Buffered(1))


def _layer_spec(arr, layer):
    index = (layer,) + (0,) * (arr.ndim - 1)
    return pl.BlockSpec((None,) + arr.shape[1:], lambda *_: index, pipeline_mode=pl.Buffered(1))


def _params(n_grid, semantics="parallel"):
    return pltpu.CompilerParams(
        dimension_semantics=(semantics,) * n_grid, vmem_limit_bytes=VMEM_LIMIT)


def _ffn_body(x, up_ref, down_ref):
    xb = x.astype(BF16)
    a = _dot(xb, up_ref[:, :D_FF])
    b = _dot(xb, up_ref[:, D_FF:])
    h = (a * jax.nn.sigmoid(a) * b).astype(BF16)
    return xb, _dot(h, down_ref[...])


def _ffn_ln_kernel(x_ref, up_ref, down_ref, g_ref, b_ref, o_ref):
    x = x_ref[...]
    _, y = _ffn_body(x, up_ref, down_ref)
    o_ref[...] = _layer_norm(ALPHA * x + 0.5 * y, g_ref[...], b_ref[...], LN_EPS)


def _ffn_ple_ln_kernel(x_ref, p_ref, up_ref, down_ref, wpg_ref, wp_ref, g_ref, b_ref, o_ref):
    x = x_ref[...]
    xb, y = _ffn_body(x, up_ref, down_ref)
    ple = jax.nn.sigmoid(_dot(xb, wpg_ref[...])) * _dot(p_ref[...].astype(BF16), wp_ref[...])
    o_ref[...] = _layer_norm(ALPHA * x + 0.5 * y + ple, g_ref[...], b_ref[...], LN_EPS)


def _ffn_ln(layer, x, up, down, g, b):
    n = x.shape[0]
    row = pl.BlockSpec((TM, D_MODEL), lambda i: (i, 0))
    return pl.pallas_call(
        _ffn_ln_kernel,
        grid=(n // TM,),
        in_specs=[row] + [_layer_spec(a, layer) for a in (up, down, g, b)],
        out_specs=row,
        out_shape=jax.ShapeDtypeStruct((n, D_MODEL), F32),
        compiler_params=_params(1),
        name="ffn_ln",
    )(x, up, down, g, b)


def _ffn_ple_ln(layer, x, p, up, down, wpg, wp, g, b):
    n = x.shape[0]
    row = pl.BlockSpec((TM, D_MODEL), lambda i: (i, 0))
    return pl.pallas_call(
        _ffn_ple_ln_kernel,
        grid=(n // TM,),
        in_specs=[row, pl.BlockSpec((None, TM, PLE_DIM), lambda i: (layer, i, 0))]
                 + [_layer_spec(a, layer) for a in (up, down, wpg, wp, g, b)],
        out_specs=row,
        out_shape=jax.ShapeDtypeStruct((n, D_MODEL), F32),
        compiler_params=_params(1),
        name="ffn_ple_ln",
    )(x, p, up, down, wpg, wp, g, b)


def _sgu_kernel(x_ref, w_ref, lng_ref, lnb_ref, ws_ref, bias_ref, o_ref):
    xb = x_ref[...].astype(BF16)
    uv = _dot(xb, w_ref[:, _OFF_SGU:_OFF_RET])
    u = jax.nn.gelu(uv[:, :SGU_WIDTH])
    v = _layer_norm(jax.nn.gelu(uv[:, SGU_WIDTH:]), lng_ref[...], lnb_ref[...], LN_EPS)
    vb = v.astype(BF16)
    t_row = lax.broadcasted_iota(jnp.int32, (SGU_BLOCK, SGU_BLOCK), 0)
    t_col = lax.broadcasted_iota(jnp.int32, (SGU_BLOCK, SGU_BLOCK), 1)
    causal = t_row >= t_col
    w_groups = [jnp.where(causal, ws_ref[g], 0.0).astype(BF16) for g in range(SGU_GROUPS)]
    group = lax.broadcasted_iota(jnp.int32, (SGU_BLOCK, SGU_WIDTH), 1) // (SGU_WIDTH // SGU_GROUPS)
    bias = bias_ref[...]
    for blk in range(TM // SGU_BLOCK):
        rows = slice(blk * SGU_BLOCK, (blk + 1) * SGU_BLOCK)
        v_blk = vb[rows]
        mixed = bias
        for g in range(SGU_GROUPS):
            mixed = mixed + jnp.where(group == g, _dot(w_groups[g], v_blk), 0.0)
        o_ref[rows, :] = (u[rows] * mixed).astype(o_ref.dtype)


def _sgu(layer, x, w, lng, lnb, ws, bias):
    n = x.shape[0]
    return pl.pallas_call(
        _sgu_kernel,
        grid=(n // TM,),
        in_specs=[pl.BlockSpec((TM, D_MODEL), lambda i: (i, 0))]
                 + [_layer_spec(a, layer) for a in (w, lng, lnb, ws, bias)],
        out_specs=pl.BlockSpec((TM, BRANCH_WIDTH), lambda i: (i, 0)),
        out_shape=jax.ShapeDtypeStruct((n, BRANCH_WIDTH), BF16),
        compiler_params=_params(1),
        name="sgu_mixer",
    )(x, w, lng, lnb, ws, bias)


def _swap_halves(x, half):
    width = x.shape[-1]
    lane = lax.broadcasted_iota(jnp.int32, x.shape, x.ndim - 1)
    first = (lane % (2 * half)) < half
    return jnp.where(first, pltpu.roll(x, width - half, x.ndim - 1), pltpu.roll(x, half, x.ndim - 1))


def _ret_kernel(x_ref, w_ref, cos_ref, sin_ref, dmask_ref, rowdec_ref, keydec_ref, tiledec_ref,
                o_ref, state_ref):
    width = RET_HEADS * RET_KDIM

    @pl.when(pl.program_id(1) == 0)
    def _():
        state_ref[...] = jnp.zeros_like(state_ref)

    xb = x_ref[...].astype(BF16)
    proj = _dot(xb, w_ref[:, _OFF_RET:_OFF_POOL])
    cos = jnp.concatenate([cos_ref[...], cos_ref[...]], axis=1)
    sin = jnp.concatenate([sin_ref[...], sin_ref[...]], axis=1)
    q = proj[:, :width]
    k = proj[:, width:2 * width]
    q = q * cos + _swap_halves(q, RET_KDIM // 2) * sin
    k = (k * cos + _swap_halves(k, RET_KDIM // 2) * sin) * RET_KDIM ** -0.5
    v = proj[:, 2 * width:3 * width]
    gate = proj[:, 3 * width:]
    vb = v.astype(BF16)
    kb = k.astype(BF16)

    head = lax.broadcasted_iota(jnp.int32, (RET_TILE, width), 1) // RET_KDIM
    y = _dot(q.astype(BF16), state_ref[...].astype(BF16)) * rowdec_ref[...]
    for h in range(RET_HEADS):
        qh = jnp.where(head == h, q, 0.0).astype(BF16)
        scores = _dot_nt(qh, kb) * dmask_ref[h]
        y = y + jnp.where(head == h, _dot(scores.astype(BF16), vb), 0.0)

    kd = (k * keydec_ref[...]).astype(BF16)
    row_head = lax.broadcasted_iota(jnp.int32, (width, width), 0) // RET_KDIM
    col_head = lax.broadcasted_iota(jnp.int32, (width, width), 1) // RET_KDIM
    kv = jnp.where(row_head == col_head, _dot_tn(kd, vb), 0.0)
    state_ref[...] = state_ref[...] * tiledec_ref[...] + kv

    inv = 1.0 / RET_KDIM
    mu = jnp.zeros_like(y)
    for h in range(RET_HEADS):
        s = jnp.sum(jnp.where(head == h, y, 0.0), axis=1, keepdims=True) * inv
        mu = jnp.where(head == h, s, mu)
    d = y - mu
    var = jnp.zeros_like(y)
    for h in range(RET_HEADS):
        s = jnp.sum(jnp.where(head == h, d * d, 0.0), axis=1, keepdims=True) * inv
        var = jnp.where(head == h, s, var)
    yn = d * lax.rsqrt(var + GN_EPS)
    o_ref[...] = (gate * jax.nn.sigmoid(gate) * yn).astype(o_ref.dtype)


def _retention_tables():
    heads = np.arange(RET_HEADS, dtype=np.float64)
    log_gamma = np.log1p(-np.exp2(-5.0 - heads))
    t = np.arange(RET_TILE)
    chunk = t // CHUNK
    diff = (t[:, None] - t[None, :]).astype(np.float64)
    same = chunk[:, None] == chunk[None, :]
    earlier = chunk[None, :] < chunk[:, None]
    expo = np.where(same, np.abs(diff), diff)
    dmask = np.where((same | earlier)[None], np.exp(log_gamma[:, None, None] * expo[None]), 0.0)
    rowdec = np.exp(log_gamma[None, :] * (t[:, None] + 1.0))
    keydec = np.exp(log_gamma[None, :] * (RET_TILE - 1.0 - t[:, None]))
    tiledec = np.exp(log_gamma * RET_TILE)
    rep = lambda a: np.repeat(a, RET_KDIM, axis=-1)
    width = RET_HEADS * RET_KDIM
    tiledec_full = np.broadcast_to(rep(tiledec[None, :]).T, (width, width))
    return (jnp.asarray(dmask, F32), jnp.asarray(rep(rowdec), F32), jnp.asarray(rep(keydec), F32),
            jnp.asarray(tiledec_full, F32))


def _retention(layer, x3, w, cos, sin):
    bsz, seq, _ = x3.shape
    width = RET_HEADS * RET_KDIM
    dmask, rowdec, keydec, tiledec = _retention_tables()
    tile = lambda last: pl.BlockSpec((None, RET_TILE, last), lambda b, s: (b, s, 0))
    return pl.pallas_call(
        _ret_kernel,
        grid=(bsz, seq // RET_TILE),
        in_specs=[tile(D_MODEL), _layer_spec(w, layer), tile(128), tile(128),
                  _const_spec(dmask.shape), _const_spec(rowdec.shape), _const_spec(keydec.shape),
                  _const_spec(tiledec.shape)],
        out_specs=tile(BRANCH_WIDTH),
        out_shape=jax.ShapeDtypeStruct((bsz, seq, BRANCH_WIDTH), BF16),
        scratch_shapes=[pltpu.VMEM((width, width), F32)],
        compiler_params=_params(2, "arbitrary"),
        name="retention_mixer",
    )(x3, w, cos, sin, dmask, rowdec, keydec, tiledec)


def _pool_kernel(x_ref, w_ref, wp_ref, scale_ref, o_ref, zext_ref):
    s = pl.program_id(1)

    @pl.when(s == 0)
    def _():
        zext_ref[0:POOL_HALO, :] = jnp.zeros((POOL_HALO, BRANCH_WIDTH), F32)

    z = _dot(x_ref[...].astype(BF16), w_ref[:, _OFF_POOL:_OFF_CQ])
    zext_ref[POOL_HALO:POOL_HALO + TM, :] = z
    group = lax.broadcasted_iota(jnp.int32, (TM, BRANCH_WIDTH), 1) // (BRANCH_WIDTH // len(POOL_WINDOWS))
    acc = z
    win = jnp.zeros_like(z)
    window = jnp.zeros((TM, BRANCH_WIDTH), jnp.int32)
    for d in range(1, POOL_HALO):
        acc = acc + zext_ref[POOL_HALO - d:POOL_HALO - d + TM, :]
        if d + 1 in POOL_WINDOWS:
            gi = POOL_WINDOWS.index(d + 1)
            win = jnp.where(group == gi, acc, win)
            window = jnp.where(group == gi, d + 1, window)
    t = s * TM + lax.broadcasted_iota(jnp.int32, (TM, BRANCH_WIDTH), 0)
    count = jnp.minimum(t + 1, window).astype(F32)
    pooled = win / count - z
    y = _dot(pooled.astype(BF16), wp_ref[...]) * scale_ref[...]
    o_ref[...] = y.astype(o_ref.dtype)
    zext_ref[0:POOL_HALO, :] = zext_ref[TM:TM + POOL_HALO, :]


def _pool(layer, x3, w, wp, scale):
    bsz, seq, _ = x3.shape
    tile = lambda last: pl.BlockSpec((None, TM, last), lambda b, s: (b, s, 0))
    return pl.pallas_call(
        _pool_kernel,
        grid=(bsz, seq // TM),
        in_specs=[tile(D_MODEL)] + [_layer_spec(a, layer) for a in (w, wp, scale)],
        out_specs=tile(BRANCH_WIDTH),
        out_shape=jax.ShapeDtypeStruct((bsz, seq, BRANCH_WIDTH), BF16),
        scratch_shapes=[pltpu.VMEM((TM + POOL_HALO, BRANCH_WIDTH), F32)],
        compiler_params=_params(2, "arbitrary"),
        name="pool_mixer",
    )(x3, w, wp, scale)


def _rms_norm(x, g):
    return x * lax.rsqrt(jnp.mean(x * x, axis=-1, keepdims=True) + RMS_EPS) * g


def _mla_pre_kernel(x_ref, w_ref, ck_ref, sk_ref, qg_ref, kvg_ref, wq1_ref, wq2_ref, wk_ref, wvt_ref,
                    q_ref, k_ref, vt_ref):
    xb = x_ref[...].astype(BF16)
    proj = _dot(xb, w_ref[:, _OFF_CQ:MIX_COLS])
    cq = _rms_norm(proj[:, :MLA_Q_RANK], qg_ref[...]).astype(BF16)
    ckv = _rms_norm(proj[:, MLA_Q_RANK:MLA_Q_RANK + MLA_KV_RANK], kvg_ref[...]).astype(BF16)
    k_raw = proj[:, MLA_Q_RANK + MLA_KV_RANK:]
    k_pe = pltpu.roll(k_raw * ck_ref[...] + _swap_halves(k_raw, MLA_ROPE // 2) * sk_ref[...], MLA_NOPE, 1)
    ck = pltpu.roll(ck_ref[...], MLA_NOPE, 1)
    sk = pltpu.roll(sk_ref[...], MLA_NOPE, 1)
    lane = lax.broadcasted_iota(jnp.int32, ck.shape, 1)
    scale = MLA_QK ** -0.5 * math.log2(math.e)
    cq_tab = (ck + jnp.where(lane < MLA_NOPE, 1.0, 0.0)) * scale
    sq_tab = sk * scale
    tile4 = lambda a: jnp.concatenate([a] * MLA_HEADS, axis=1)
    q = _dot(cq, wq1_ref[...]) * tile4(cq_tab) + _dot(cq, wq2_ref[...]) * tile4(sq_tab)
    k = _dot(ckv, wk_ref[...]) + tile4(k_pe)
    q_ref[...] = q.astype(q_ref.dtype)
    k_ref[...] = k.astype(k_ref.dtype)
    for t in range(TM // ATT_TK):
        vt_ref[t] = _dot_nt(wvt_ref[...], ckv[t * ATT_TK:(t + 1) * ATT_TK]).astype(vt_ref.dtype)


def _mla_pre(layer, x, w, ck, sk, qg, kvg, wq1, wq2, wk, wvt):
    n = x.shape[0]
    row = lambda last: pl.BlockSpec((TM, last), lambda i: (i, 0))
    return pl.pallas_call(
        _mla_pre_kernel,
        grid=(n // TM,),
        in_specs=[row(D_MODEL), _layer_spec(w, layer), row(HEAD_PAD), row(HEAD_PAD)]
                 + [_layer_spec(a, layer) for a in (qg, kvg, wq1, wq2, wk, wvt)],
        out_specs=[row(MLA_HEADS * HEAD_PAD), row(MLA_HEADS * HEAD_PAD),
                   pl.BlockSpec((TM // ATT_TK, MLA_HEADS * MLA_V, ATT_TK), lambda i: (i, 0, 0))],
        out_shape=[jax.ShapeDtypeStruct((n, MLA_HEADS * HEAD_PAD), BF16),
                   jax.ShapeDtypeStruct((n, MLA_HEADS * HEAD_PAD), BF16),
                   jax.ShapeDtypeStruct((n // ATT_TK, MLA_HEADS * MLA_V, ATT_TK), BF16)],
        compiler_params=_params(1),
        name="mla_project",
    )(x, w, ck, sk, qg, kvg, wq1, wq2, wk, wvt)


def _attn_kernel(q_ref, k_ref, vt_ref, o_ref, st_ref, m_ref, acc_ref):
    qi = pl.program_id(1)
    key_chunk = lax.broadcasted_iota(jnp.int32, (ATT_TK, ATT_TQ), 0) // CHUNK
    qry_chunk = lax.broadcasted_iota(jnp.int32, (ATT_TK, ATT_TQ), 1) // CHUNK
    diag_masks = [key_chunk + t * (ATT_TK // CHUNK) <= qry_chunk for t in range(ATT_TQ // ATT_TK)]
    heads = range(MLA_HEADS)

    def scores(j, slot):
        rows = pl.ds(pl.multiple_of(j * ATT_TK, ATT_TK), ATT_TK)
        for h in heads:
            cols = slice(h * HEAD_PAD, (h + 1) * HEAD_PAD)
            st_ref[slot, h] = _dot_nt(k_ref[rows, cols], q_ref[:, cols])

    ones = jnp.ones((ATT_ONES, ATT_TK), BF16)

    def softmax_pv(j, slot, mask=None):
        for h in heads:
            load = lambda: (st_ref[slot, h] if mask is None
                            else jnp.where(mask, st_ref[slot, h], -jnp.inf))
            m = m_ref[h]
            m_new = jnp.maximum(m, jnp.max(load(), axis=0, keepdims=True))
            p = jnp.exp2((load() - m_new).astype(BF16))
            lhs = jnp.concatenate([vt_ref[j, h * MLA_V:(h + 1) * MLA_V, :], ones], axis=0)
            acc_ref[h] = jnp.exp2(m - m_new) * acc_ref[h] + _dot(lhs, p)
            m_ref[h] = m_new

    def pair(i, carry):
        scores(2 * i + 1, 1)
        softmax_pv(2 * i, 0)
        scores(2 * i + 2, 0)
        softmax_pv(2 * i + 1, 1)
        return carry

    m_ref[...] = jnp.full(m_ref.shape, -jnp.inf, F32)
    acc_ref[...] = jnp.zeros(acc_ref.shape, F32)
    scores(0, 0)
    lax.fori_loop(0, qi, pair, 0)
    scores(2 * qi + 1, 1)
    softmax_pv(2 * qi, 0, diag_masks[0])
    softmax_pv(2 * qi + 1, 1, diag_masks[1])
    out_t = jnp.concatenate([acc_ref[h, :MLA_V] / acc_ref[h, MLA_V:MLA_V + 1] for h in heads], axis=0)
    o_ref[...] = out_t.T.astype(o_ref.dtype)


def _attention(q3, k3, vt4):
    bsz, seq, _ = q3.shape
    return pl.pallas_call(
        _attn_kernel,
        grid=(bsz, seq // ATT_TQ),
        in_specs=[pl.BlockSpec((None, ATT_TQ, MLA_HEADS * HEAD_PAD), lambda b, i: (b, i, 0)),
                  pl.BlockSpec((None, seq, MLA_HEADS * HEAD_PAD), lambda b, i: (b, 0, 0)),
                  pl.BlockSpec((None, seq // ATT_TK, MLA_HEADS * MLA_V, ATT_TK), lambda b, i: (b, 0, 0, 0))],
        out_specs=pl.BlockSpec((None, ATT_TQ, MLA_HEADS * MLA_V), lambda b, i: (b, i, 0)),
        out_shape=jax.ShapeDtypeStruct((bsz, seq, MLA_HEADS * MLA_V), BF16),
        scratch_shapes=[pltpu.VMEM((2, MLA_HEADS, ATT_TK, ATT_TQ), F32),
                        pltpu.VMEM((MLA_HEADS, 1, ATT_TQ), F32),
                        pltpu.VMEM((MLA_HEADS, MLA_V + ATT_ONES, ATT_TQ), F32)],
        compiler_params=_params(2),
        name="mla_attention",
    )(q3, k3, vt4)


def _merge_kernel(x_ref, ya_ref, yb_ref, yc_ref, yd_ref, wg_ref, wb_ref, wo_ref, g_ref, b_ref, o_ref):
    x = x_ref[...]
    xb = x.astype(BF16)
    merged = None
    for n, y_ref in enumerate((ya_ref, yb_ref, yc_ref, yd_ref)):
        gate = jax.nn.sigmoid(_dot(xb, wg_ref[:, n * D_MODEL:(n + 1) * D_MODEL]))
        term = gate * _dot(y_ref[...], wb_ref[n])
        merged = term if merged is None else merged + term
    mix = _dot(merged.astype(BF16), wo_ref[...])
    o_ref[...] = _layer_norm(ALPHA * x + mix, g_ref[...], b_ref[...], LN_EPS)


def _merge(layer, x, ys, wg, wb, wo, g, b):
    n = x.shape[0]
    row = lambda last: pl.BlockSpec((TM, last), lambda i: (i, 0))
    return pl.pallas_call(
        _merge_kernel,
        grid=(n // TM,),
        in_specs=[row(D_MODEL)] + [row(BRANCH_WIDTH)] * N_BRANCH
                 + [_layer_spec(a, layer) for a in (wg, wb, wo, g, b)],
        out_specs=row(D_MODEL),
        out_shape=jax.ShapeDtypeStruct((n, D_MODEL), F32),
        compiler_params=_params(1),
        name="merge_ln",
    )(x, *ys, wg, wb, wo, g, b)


def _rope_tables(positions):
    pos = positions.astype(F32)[..., None]
    ang_r = pos * (ROPE_BASE ** (-jnp.arange(0, RET_KDIM, 2, dtype=F32) / RET_KDIM))
    cr, sr = jnp.cos(ang_r), jnp.sin(ang_r)
    ret_cos = jnp.tile(jnp.concatenate([cr, cr], axis=-1), (1, 1, 2))
    ret_sin = jnp.tile(jnp.concatenate([-sr, sr], axis=-1), (1, 1, 2))
    ang_m = pos * (ROPE_BASE ** (-jnp.arange(0, MLA_ROPE, 2, dtype=F32) / MLA_ROPE))
    cm, sm = jnp.cos(ang_m), jnp.sin(ang_m)
    pad = lambda a: jnp.pad(a, ((0, 0), (0, 0), (0, HEAD_PAD - MLA_ROPE)))
    mla_cos = pad(jnp.concatenate([cm, cm], axis=-1))
    mla_sin = pad(jnp.concatenate([-sm, sm], axis=-1))
    return ret_cos, ret_sin, mla_cos, mla_sin


def _swap_cols(w):
    half = w.shape[-1] // 2
    return jnp.concatenate([w[..., half:], w[..., :half]], axis=-1)


def _mla_weights(w_uq, w_ukv):
    depth = w_uq.shape[0]
    uq = w_uq.reshape(depth, MLA_Q_RANK, MLA_HEADS, MLA_QK)
    q_nope, q_rope = uq[..., :MLA_NOPE], uq[..., MLA_NOPE:]
    zq = jnp.zeros((depth, MLA_Q_RANK, MLA_HEADS, HEAD_PAD - MLA_QK), w_uq.dtype)
    wq1 = jnp.concatenate([q_nope, q_rope, zq], axis=-1)
    wq2 = jnp.concatenate([jnp.zeros_like(q_nope), _swap_cols(q_rope), zq], axis=-1)
    ukv = w_ukv.reshape(depth, MLA_KV_RANK, MLA_HEADS, MLA_NOPE + MLA_V)
    k_nope, v = ukv[..., :MLA_NOPE], ukv[..., MLA_NOPE:]
    wk = jnp.concatenate([k_nope, jnp.zeros_like(k_nope)], axis=-1)
    wvt = jnp.swapaxes(v.reshape(depth, MLA_KV_RANK, MLA_HEADS * MLA_V), 1, 2)
    flat = lambda a: a.reshape(depth, a.shape[1], MLA_HEADS * HEAD_PAD).astype(BF16)
    return flat(wq1), flat(wq2), flat(wk), wvt.astype(BF16)


def kernel(x, p, positions, ffn1_up, ffn1_down, ln1_g, ln1_b, w_in, sgu_ln_g, sgu_ln_b, sgu_w, sgu_b,
           pool_w, pool_scale, mla_q_norm, mla_kv_norm, mla_w_uq, mla_w_ukv, w_branch, w_out,
           ln2_g, ln2_b, ffn2_up, ffn2_down, w_ple_gate, w_ple, ln3_g, ln3_b):
    bsz, seq, dm = x.shape
    n = bsz * seq
    ret_cos, ret_sin, mla_cos, mla_sin = _rope_tables(positions)
    mla_cos, mla_sin = mla_cos.reshape(n, HEAD_PAD), mla_sin.reshape(n, HEAD_PAD)
    bf = lambda a: a.astype(BF16)
    rows = lambda a: a[:, None, :]

    up1, down1, up2, down2 = bf(ffn1_up), bf(ffn1_down), bf(ffn2_up), bf(ffn2_down)
    w_mix = bf(w_in[:, :, :MIX_COLS])
    w_gate = bf(w_in[:, :, _OFF_GATE:])
    wb, wo, wpg, wp = bf(w_branch), bf(w_out), bf(w_ple_gate), bf(w_ple)
    wq1, wq2, wk, wvt = _mla_weights(mla_w_uq, mla_w_ukv)
    sgu_bias = jnp.repeat(jnp.swapaxes(sgu_b, 1, 2), SGU_WIDTH // SGU_GROUPS, axis=2)
    groups = len(POOL_WINDOWS)
    pool_bd = bf(jnp.einsum("lgcd,gh->lgchd", pool_w, jnp.eye(groups, dtype=pool_w.dtype))
                 .reshape(DEPTH, BRANCH_WIDTH, BRANCH_WIDTH))
    p2 = p.reshape(DEPTH, n, PLE_DIM)

    h = x.reshape(n, dm)
    for i in range(DEPTH):
        h = _ffn_ln(i, h, up1, down1, rows(ln1_g), rows(ln1_b))
        h3 = h.reshape(bsz, seq, dm)
        y_a = _sgu(i, h, w_mix, rows(sgu_ln_g), rows(sgu_ln_b), sgu_w, sgu_bias)
        y_b = _retention(i, h3, w_mix, ret_cos, ret_sin)
        y_c = _pool(i, h3, w_mix, pool_bd, rows(pool_scale))
        q, k, vt = _mla_pre(i, h, w_mix, mla_cos, mla_sin, rows(mla_q_norm), rows(mla_kv_norm),
                            wq1, wq2, wk, wvt)
        y_d = _attention(q.reshape(bsz, seq, -1), k.reshape(bsz, seq, -1),
                         vt.reshape(bsz, seq // ATT_TK, MLA_HEADS * MLA_V, ATT_TK))
        ys = (y_a, y_b.reshape(n, -1), y_c.reshape(n, -1), y_d.reshape(n, -1))
        h = _merge(i, h, ys, w_gate, wb, wo, rows(ln2_g), rows(ln2_b))
        h = _ffn_ple_ln(i, h, p2, up2, down2, wpg, wp, rows(ln3_g), rows(ln3_b))
    return h.reshape(bsz, seq, dm)
```

```python
import math

import numpy as np
import jax
import jax.numpy as jnp
from jax import lax
from jax.experimental import pallas as pl
from jax.experimental.pallas import tpu as pltpu

D_MODEL = 1024
DEPTH = 2
CHUNK = 64
SGU_WIDTH = 256
SGU_BLOCK = 128
SGU_GROUPS = 4
RET_HEADS = 4
RET_KDIM = 64
POOL_WINDOWS = (2, 4, 8, 16)
POOL_HALO = 16
MLA_HEADS = 4
MLA_NOPE = 64
MLA_ROPE = 32
MLA_QK = MLA_NOPE + MLA_ROPE
MLA_V = 64
MLA_Q_RANK = 256
MLA_KV_RANK = 128
HEAD_PAD = 128
ROPE_BASE = 10000.0
N_BRANCH = 4
BRANCH_WIDTH = 256
D_FF = 2816
PLE_DIM = 256
ALPHA = (2 * DEPTH) ** 0.25
LN_EPS = 1e-5
RMS_EPS = 1e-6
GN_EPS = 1e-5

_OFF_SGU = 0
_OFF_RET = 512
_OFF_POOL = 1536
_OFF_CQ = 1792
_OFF_GATE = 2208
MIX_COLS = 2304

TM = 512
TM_WIDE = 1024
SUB = 256
RET_TILE = 256
ATT_TQ = 512
ATT_TK = 256
ATT_ONES = 16
VMEM_LIMIT = 56 * 1024 * 1024

BF16 = jnp.bfloat16
F32 = jnp.float32


def _dot(a, b):
    return jnp.dot(a, b, preferred_element_type=F32)


def _dot_nt(a, b):
    return lax.dot_general(a, b, (((1,), (1,)), ((), ())), preferred_element_type=F32)


def _dot_tn(a, b):
    return lax.dot_general(a, b, (((0,), (0,)), ((), ())), preferred_element_type=F32)


def _layer_norm(r, g, b, eps):
    mu = jnp.mean(r, axis=-1, keepdims=True)
    d = r - mu
    var = jnp.mean(d * d, axis=-1, keepdims=True)
    return d * lax.rsqrt(var + eps) * g + b


def _const_spec(shape):
    zeros = (0,) * len(shape)
    return pl.BlockSpec(shape, lambda *_: zeros, pipeline_mode=pl.Buffered(1))


def _layer_spec(arr, layer):
    index = (layer,) + (0,) * (arr.ndim - 1)
    return pl.BlockSpec((None,) + arr.shape[1:], lambda *_: index, pipeline_mode=pl.Buffered(1))


def _params(n_grid, semantics="parallel"):
    return pltpu.CompilerParams(
        dimension_semantics=(semantics,) * n_grid, vmem_limit_bytes=VMEM_LIMIT)


def _pipelined_rows(o_ref, g_ref, b_ref, residual_fn):
    pending = None
    for s in range(o_ref.shape[0] // SUB):
        rows = slice(s * SUB, (s + 1) * SUB)
        r = residual_fn(rows)
        if pending is not None:
            o_ref[pending[0], :] = _layer_norm(pending[1], g_ref[...], b_ref[...], LN_EPS)
        pending = (rows, r)
    o_ref[pending[0], :] = _layer_norm(pending[1], g_ref[...], b_ref[...], LN_EPS)


def _ffn_body(x, up_ref, down_ref):
    xb = x.astype(BF16)
    a = _dot(xb, up_ref[:, :D_FF])
    b = _dot(xb, up_ref[:, D_FF:])
    h = (a * jax.nn.sigmoid(a) * b).astype(BF16)
    return xb, _dot(h, down_ref[...])


def _ffn_ln_kernel(x_ref, up_ref, down_ref, g_ref, b_ref, o_ref):
    def residual(rows):
        x = x_ref[rows, :]
        _, y = _ffn_body(x, up_ref, down_ref)
        return ALPHA * x + 0.5 * y

    _pipelined_rows(o_ref, g_ref, b_ref, residual)


def _ffn_ple_ln_kernel(x_ref, p_ref, up_ref, down_ref, wpg_ref, wp_ref, g_ref, b_ref, o_ref):
    def residual(rows):
        x = x_ref[rows, :]
        xb, y = _ffn_body(x, up_ref, down_ref)
        ple = jax.nn.sigmoid(_dot(xb, wpg_ref[...])) * _dot(p_ref[rows, :].astype(BF16), wp_ref[...])
        return ALPHA * x + 0.5 * y + ple

    _pipelined_rows(o_ref, g_ref, b_ref, residual)


def _ffn_ln(layer, x, up, down, g, b):
    n = x.shape[0]
    row = pl.BlockSpec((TM_WIDE, D_MODEL), lambda i: (i, 0))
    return pl.pallas_call(
        _ffn_ln_kernel,
        grid=(n // TM_WIDE,),
        in_specs=[row] + [_layer_spec(a, layer) for a in (up, down, g, b)],
        out_specs=row,
        out_shape=jax.ShapeDtypeStruct((n, D_MODEL), F32),
        compiler_params=_params(1),
        name="ffn_ln",
    )(x, up, down, g, b)


def _ffn_ple_ln(layer, x, p, up, down, wpg, wp, g, b):
    n = x.shape[0]
    row = pl.BlockSpec((TM_WIDE, D_MODEL), lambda i: (i, 0))
    return pl.pallas_call(
        _ffn_ple_ln_kernel,
        grid=(n // TM_WIDE,),
        in_specs=[row, pl.BlockSpec((None, TM_WIDE, PLE_DIM), lambda i: (layer, i, 0))]
                 + [_layer_spec(a, layer) for a in (up, down, wpg, wp, g, b)],
        out_specs=row,
        out_shape=jax.ShapeDtypeStruct((n, D_MODEL), F32),
        compiler_params=_params(1),
        name="ffn_ple_ln",
    )(x, p, up, down, wpg, wp, g, b)


def _sgu_kernel(x_ref, w_ref, lng_ref, lnb_ref, ws_ref, bias_ref, o_ref):
    xb = x_ref[...].astype(BF16)
    uv = _dot(xb, w_ref[:, _OFF_SGU:_OFF_RET])
    u = jax.nn.gelu(uv[:, :SGU_WIDTH])
    v = _layer_norm(jax.nn.gelu(uv[:, SGU_WIDTH:]), lng_ref[...], lnb_ref[...], LN_EPS)
    vb = v.astype(BF16)
    t_row = lax.broadcasted_iota(jnp.int32, (SGU_BLOCK, SGU_BLOCK), 0)
    t_col = lax.broadcasted_iota(jnp.int32, (SGU_BLOCK, SGU_BLOCK), 1)
    causal = t_row >= t_col
    w_groups = [jnp.where(causal, ws_ref[g], 0.0).astype(BF16) for g in range(SGU_GROUPS)]
    group = lax.broadcasted_iota(jnp.int32, (SGU_BLOCK, SGU_WIDTH), 1) // (SGU_WIDTH // SGU_GROUPS)
    bias = bias_ref[...]
    for blk in range(TM // SGU_BLOCK):
        rows = slice(blk * SGU_BLOCK, (blk + 1) * SGU_BLOCK)
        v_blk = vb[rows]
        mixed = bias
        for g in range(SGU_GROUPS):
            mixed = mixed + jnp.where(group == g, _dot(w_groups[g], v_blk), 0.0)
        o_ref[rows, :] = (u[rows] * mixed).astype(o_ref.dtype)


def _sgu(layer, x, w, lng, lnb, ws, bias):
    n = x.shape[0]
    return pl.pallas_call(
        _sgu_kernel,
        grid=(n // TM,),
        in_specs=[pl.BlockSpec((TM, D_MODEL), lambda i: (i, 0))]
                 + [_layer_spec(a, layer) for a in (w, lng, lnb, ws, bias)],
        out_specs=pl.BlockSpec((TM, BRANCH_WIDTH), lambda i: (i, 0)),
        out_shape=jax.ShapeDtypeStruct((n, BRANCH_WIDTH), BF16),
        compiler_params=_params(1),
        name="sgu_mixer",
    )(x, w, lng, lnb, ws, bias)


def _swap_halves(x, half):
    width = x.shape[-1]
    lane = lax.broadcasted_iota(jnp.int32, x.shape, x.ndim - 1)
    first = (lane % (2 * half)) < half
    return jnp.where(first, pltpu.roll(x, width - half, x.ndim - 1), pltpu.roll(x, half, x.ndim - 1))


def _ret_kernel(x_ref, w_ref, cos_ref, sin_ref, dmask_ref, rowdec_ref, keydec_ref, tiledec_ref,
                o_ref, state_ref):
    width = RET_HEADS * RET_KDIM

    @pl.when(pl.program_id(1) == 0)
    def _():
        state_ref[...] = jnp.zeros_like(state_ref)

    xb = x_ref[...].astype(BF16)
    proj = _dot(xb, w_ref[:, _OFF_RET:_OFF_POOL])
    cos = jnp.concatenate([cos_ref[...], cos_ref[...]], axis=1)
    sin = jnp.concatenate([sin_ref[...], sin_ref[...]], axis=1)
    q = proj[:, :width]
    k = proj[:, width:2 * width]
    q = q * cos + _swap_halves(q, RET_KDIM // 2) * sin
    k = (k * cos + _swap_halves(k, RET_KDIM // 2) * sin) * RET_KDIM ** -0.5
    v = proj[:, 2 * width:3 * width]
    gate = proj[:, 3 * width:]
    vb = v.astype(BF16)
    kb = k.astype(BF16)

    head = lax.broadcasted_iota(jnp.int32, (RET_TILE, width), 1) // RET_KDIM
    y = _dot(q.astype(BF16), state_ref[...].astype(BF16)) * rowdec_ref[...]
    for h in range(RET_HEADS):
        qh = jnp.where(head == h, q, 0.0).astype(BF16)
        scores = _dot_nt(qh, kb) * dmask_ref[h]
        y = y + jnp.where(head == h, _dot(scores.astype(BF16), vb), 0.0)

    kd = (k * keydec_ref[...]).astype(BF16)
    row_head = lax.broadcasted_iota(jnp.int32, (width, width), 0) // RET_KDIM
    col_head = lax.broadcasted_iota(jnp.int32, (width, width), 1) // RET_KDIM
    kv = jnp.where(row_head == col_head, _dot_tn(kd, vb), 0.0)
    state_ref[...] = state_ref[...] * tiledec_ref[...] + kv

    inv = 1.0 / RET_KDIM
    mu = jnp.zeros_like(y)
    for h in range(RET_HEADS):
        s = jnp.sum(jnp.where(head == h, y, 0.0), axis=1, keepdims=True) * inv
        mu = jnp.where(head == h, s, mu)
    d = y - mu
    var = jnp.zeros_like(y)
    for h in range(RET_HEADS):
        s = jnp.sum(jnp.where(head == h, d * d, 0.0), axis=1, keepdims=True) * inv
        var = jnp.where(head == h, s, var)
    yn = d * lax.rsqrt(var + GN_EPS)
    o_ref[...] = (gate * jax.nn.sigmoid(gate) * yn).astype(o_ref.dtype)


def _retention_tables():
    heads = np.arange(RET_HEADS, dtype=np.float64)
    log_gamma = np.log1p(-np.exp2(-5.0 - heads))
    t = np.arange(RET_TILE)
    chunk = t // CHUNK
    diff = (t[:, None] - t[None, :]).astype(np.float64)
    same = chunk[:, None] == chunk[None, :]
    earlier = chunk[None, :] < chunk[:, None]
    expo = np.where(same, np.abs(diff), diff)
    dmask = np.where((same | earlier)[None], np.exp(log_gamma[:, None, None] * expo[None]), 0.0)
    rowdec = np.exp(log_gamma[None, :] * (t[:, None] + 1.0))
    keydec = np.exp(log_gamma[None, :] * (RET_TILE - 1.0 - t[:, None]))
    tiledec = np.exp(log_gamma * RET_TILE)
    rep = lambda a: np.repeat(a, RET_KDIM, axis=-1)
    width = RET_HEADS * RET_KDIM
    tiledec_full = np.broadcast_to(rep(tiledec[None, :]).T, (width, width))
    return (jnp.asarray(dmask, F32), jnp.asarray(rep(rowdec), F32), jnp.asarray(rep(keydec), F32),
            jnp.asarray(tiledec_full, F32))


def _retention(layer, x3, w, cos, sin):
    bsz, seq, _ = x3.shape
    width = RET_HEADS * RET_KDIM
    dmask, rowdec, keydec, tiledec = _retention_tables()
    tile = lambda last: pl.BlockSpec((None, RET_TILE, last), lambda b, s: (b, s, 0))
    return pl.pallas_call(
        _ret_kernel,
        grid=(bsz, seq // RET_TILE),
        in_specs=[tile(D_MODEL), _layer_spec(w, layer), tile(128), tile(128),
                  _const_spec(dmask.shape), _const_spec(rowdec.shape), _const_spec(keydec.shape),
                  _const_spec(tiledec.shape)],
        out_specs=tile(BRANCH_WIDTH),
        out_shape=jax.ShapeDtypeStruct((bsz, seq, BRANCH_WIDTH), BF16),
        scratch_shapes=[pltpu.VMEM((width, width), F32)],
        compiler_params=_params(2, "arbitrary"),
        name="retention_mixer",
    )(x3, w, cos, sin, dmask, rowdec, keydec, tiledec)


def _pool_kernel(x_ref, w_ref, wp_ref, scale_ref, o_ref, zext_ref):
    s = pl.program_id(1)

    @pl.when(s == 0)
    def _():
        zext_ref[0:POOL_HALO, :] = jnp.zeros((POOL_HALO, BRANCH_WIDTH), F32)

    z = _dot(x_ref[...].astype(BF16), w_ref[:, _OFF_POOL:_OFF_CQ])
    zext_ref[POOL_HALO:POOL_HALO + TM, :] = z
    group = lax.broadcasted_iota(jnp.int32, (TM, BRANCH_WIDTH), 1) // (BRANCH_WIDTH // len(POOL_WINDOWS))
    acc = z
    win = jnp.zeros_like(z)
    window = jnp.zeros((TM, BRANCH_WIDTH), jnp.int32)
    for d in range(1, POOL_HALO):
        acc = acc + zext_ref[POOL_HALO - d:POOL_HALO - d + TM, :]
        if d + 1 in POOL_WINDOWS:
            gi = POOL_WINDOWS.index(d + 1)
            win = jnp.where(group == gi, acc, win)
            window = jnp.where(group == gi, d + 1, window)
    t = s * TM + lax.broadcasted_iota(jnp.int32, (TM, BRANCH_WIDTH), 0)
    count = jnp.minimum(t + 1, window).astype(F32)
    pooled = win / count - z
    y = _dot(pooled.astype(BF16), wp_ref[...]) * scale_ref[...]
    o_ref[...] = y.astype(o_ref.dtype)
    zext_ref[0:POOL_HALO, :] = zext_ref[TM:TM + POOL_HALO, :]


def _pool(layer, x3, w, wp, scale):
    bsz, seq, _ = x3.shape
    tile = lambda last: pl.BlockSpec((None, TM, last), lambda b, s: (b, s, 0))
    return pl.pallas_call(
        _pool_kernel,
        grid=(bsz, seq // TM),
        in_specs=[tile(D_MODEL)] + [_layer_spec(a, layer) for a in (w, wp, scale)],
        out_specs=tile(BRANCH_WIDTH),
        out_shape=jax.ShapeDtypeStruct((bsz, seq, BRANCH_WIDTH), BF16),
        scratch_shapes=[pltpu.VMEM((TM + POOL_HALO, BRANCH_WIDTH), F32)],
        compiler_params=_params(2, "arbitrary"),
        name="pool_mixer",
    )(x3, w, wp, scale)


def _rms_norm(x, g):
    return x * lax.rsqrt(jnp.mean(x * x, axis=-1, keepdims=True) + RMS_EPS) * g


def _mla_pre_kernel(x_ref, w_ref, ck_ref, sk_ref, qg_ref, kvg_ref, wq1_ref, wq2_ref, wk_ref, wvt_ref,
                    q_ref, k_ref, vt_ref):
    xb = x_ref[...].astype(BF16)
    proj = _dot(xb, w_ref[:, _OFF_CQ:MIX_COLS])
    cq = _rms_norm(proj[:, :MLA_Q_RANK], qg_ref[...]).astype(BF16)
    ckv = _rms_norm(proj[:, MLA_Q_RANK:MLA_Q_RANK + MLA_KV_RANK], kvg_ref[...]).astype(BF16)
    k_raw = proj[:, MLA_Q_RANK + MLA_KV_RANK:]
    k_pe = pltpu.roll(k_raw * ck_ref[...] + _swap_halves(k_raw, MLA_ROPE // 2) * sk_ref[...], MLA_NOPE, 1)
    ck = pltpu.roll(ck_ref[...], MLA_NOPE, 1)
    sk = pltpu.roll(sk_ref[...], MLA_NOPE, 1)
    lane = lax.broadcasted_iota(jnp.int32, ck.shape, 1)
    scale = MLA_QK ** -0.5 * math.log2(math.e)
    cq_tab = (ck + jnp.where(lane < MLA_NOPE, 1.0, 0.0)) * scale
    sq_tab = sk * scale
    tile4 = lambda a: jnp.concatenate([a] * MLA_HEADS, axis=1)
    q = _dot(cq, wq1_ref[...]) * tile4(cq_tab) + _dot(cq, wq2_ref[...]) * tile4(sq_tab)
    k = _dot(ckv, wk_ref[...]) + tile4(k_pe)
    q_ref[...] = q.astype(q_ref.dtype)
    k_ref[...] = k.astype(k_ref.dtype)
    for t in range(TM // ATT_TK):
        vt_ref[t] = _dot_nt(wvt_ref[...], ckv[t * ATT_TK:(t + 1) * ATT_TK]).astype(vt_ref.dtype)


def _mla_pre(layer, x, w, ck, sk, qg, kvg, wq1, wq2, wk, wvt):
    n = x.shape[0]
    row = lambda last: pl.BlockSpec((TM, last), lambda i: (i, 0))
    return pl.pallas_call(
        _mla_pre_kernel,
        grid=(n // TM,),
        in_specs=[row(D_MODEL), _layer_spec(w, layer), row(HEAD_PAD), row(HEAD_PAD)]
                 + [_layer_spec(a, layer) for a in (qg, kvg, wq1, wq2, wk, wvt)],
        out_specs=[row(MLA_HEADS * HEAD_PAD), row(MLA_HEADS * HEAD_PAD),
                   pl.BlockSpec((TM // ATT_TK, MLA_HEADS * MLA_V, ATT_TK), lambda i: (i, 0, 0))],
        out_shape=[jax.ShapeDtypeStruct((n, MLA_HEADS * HEAD_PAD), BF16),
                   jax.ShapeDtypeStruct((n, MLA_HEADS * HEAD_PAD), BF16),
                   jax.ShapeDtypeStruct((n // ATT_TK, MLA_HEADS * MLA_V, ATT_TK), BF16)],
        compiler_params=_params(1),
        name="mla_project",
    )(x, w, ck, sk, qg, kvg, wq1, wq2, wk, wvt)


def _attn_kernel(q_ref, k_ref, vt_ref, o_ref, st_ref, m_ref, acc_ref):
    qi = pl.program_id(1)
    key_chunk = lax.broadcasted_iota(jnp.int32, (ATT_TK, ATT_TQ), 0) // CHUNK
    qry_chunk = lax.broadcasted_iota(jnp.int32, (ATT_TK, ATT_TQ), 1) // CHUNK
    diag_masks = [key_chunk + t * (ATT_TK // CHUNK) <= qry_chunk for t in range(ATT_TQ // ATT_TK)]
    heads = range(MLA_HEADS)

    def scores(j, slot):
        rows = pl.ds(pl.multiple_of(j * ATT_TK, ATT_TK), ATT_TK)
        for h in heads:
            cols = slice(h * HEAD_PAD, (h + 1) * HEAD_PAD)
            st_ref[slot, h] = _dot_nt(k_ref[rows, cols], q_ref[:, cols])

    ones = jnp.ones((ATT_ONES, ATT_TK), BF16)

    def softmax_pv(j, slot, mask=None):
        for h in heads:
            load = lambda: (st_ref[slot, h] if mask is None
                            else jnp.where(mask, st_ref[slot, h], -jnp.inf))
            m = m_ref[h]
            m_new = jnp.maximum(m, jnp.max(load(), axis=0, keepdims=True))
            p = jnp.exp2((load() - m_new).astype(BF16))
            lhs = jnp.concatenate([vt_ref[j, h * MLA_V:(h + 1) * MLA_V, :], ones], axis=0)
            acc_ref[h] = jnp.exp2(m - m_new) * acc_ref[h] + _dot(lhs, p)
            m_ref[h] = m_new

    def pair(i, carry):
        scores(2 * i + 1, 1)
        softmax_pv(2 * i, 0)
        scores(2 * i + 2, 0)
        softmax_pv(2 * i + 1, 1)
        return carry

    m_ref[...] = jnp.full(m_ref.shape, -jnp.inf, F32)
    acc_ref[...] = jnp.zeros(acc_ref.shape, F32)
    scores(0, 0)
    lax.fori_loop(0, qi, pair, 0)
    scores(2 * qi + 1, 1)
    softmax_pv(2 * qi, 0, diag_masks[0])
    softmax_pv(2 * qi + 1, 1, diag_masks[1])
    out_t = jnp.concatenate([acc_ref[h, :MLA_V] / acc_ref[h, MLA_V:MLA_V + 1] for h in heads], axis=0)
    o_ref[...] = out_t.T.astype(o_ref.dtype)


def _attention(q3, k3, vt4):
    bsz, seq, _ = q3.shape
    return pl.pallas_call(
        _attn_kernel,
        grid=(bsz, seq // ATT_TQ),
        in_specs=[pl.BlockSpec((None, ATT_TQ, MLA_HEADS * HEAD_PAD), lambda b, i: (b, i, 0)),
                  pl.BlockSpec((None, seq, MLA_HEADS * HEAD_PAD), lambda b, i: (b, 0, 0)),
                  pl.BlockSpec((None, seq // ATT_TK, MLA_HEADS * MLA_V, ATT_TK), lambda b, i: (b, 0, 0, 0))],
        out_specs=pl.BlockSpec((None, ATT_TQ, MLA_HEADS * MLA_V), lambda b, i: (b, i, 0)),
        out_shape=jax.ShapeDtypeStruct((bsz, seq, MLA_HEADS * MLA_V), BF16),
        scratch_shapes=[pltpu.VMEM((2, MLA_HEADS, ATT_TK, ATT_TQ), F32),
                        pltpu.VMEM((MLA_HEADS, 1, ATT_TQ), F32),
                        pltpu.VMEM((MLA_HEADS, MLA_V + ATT_ONES, ATT_TQ), F32)],
        compiler_params=_params(2),
        name="mla_attention",
    )(q3, k3, vt4)


def _merge_kernel(x_ref, ya_ref, yb_ref, yc_ref, yd_ref, wg_ref, wb_ref, wo_ref, g_ref, b_ref, o_ref):
    def residual(rows):
        x = x_ref[rows, :]
        xb = x.astype(BF16)
        merged = None
        for n, y_ref in enumerate((ya_ref, yb_ref, yc_ref, yd_ref)):
            gate = jax.nn.sigmoid(_dot(xb, wg_ref[:, n * D_MODEL:(n + 1) * D_MODEL]))
            term = gate * _dot(y_ref[rows, :], wb_ref[n])
            merged = term if merged is None else merged + term
        return ALPHA * x + _dot(merged.astype(BF16), wo_ref[...])

    _pipelined_rows(o_ref, g_ref, b_ref, residual)


def _merge(layer, x, ys, wg, wb, wo, g, b):
    n = x.shape[0]
    row = lambda last: pl.BlockSpec((TM_WIDE, last), lambda i: (i, 0))
    return pl.pallas_call(
        _merge_kernel,
        grid=(n // TM_WIDE,),
        in_specs=[row(D_MODEL)] + [row(BRANCH_WIDTH)] * N_BRANCH
                 + [_layer_spec(a, layer) for a in (wg, wb, wo, g, b)],
        out_specs=row(D_MODEL),
        out_shape=jax.ShapeDtypeStruct((n, D_MODEL), F32),
        compiler_params=_params(1),
        name="merge_ln",
    )(x, *ys, wg, wb, wo, g, b)


def _rope_tables(positions):
    pos = positions.astype(F32)[..., None]
    ang_r = pos * (ROPE_BASE ** (-jnp.arange(0, RET_KDIM, 2, dtype=F32) / RET_KDIM))
    cr, sr = jnp.cos(ang_r), jnp.sin(ang_r)
    ret_cos = jnp.tile(jnp.concatenate([cr, cr], axis=-1), (1, 1, 2))
    ret_sin = jnp.tile(jnp.concatenate([-sr, sr], axis=-1), (1, 1, 2))
    ang_m = pos * (ROPE_BASE ** (-jnp.arange(0, MLA_ROPE, 2, dtype=F32) / MLA_ROPE))
    cm, sm = jnp.cos(ang_m), jnp.sin(ang_m)
    pad = lambda a: jnp.pad(a, ((0, 0), (0, 0), (0, HEAD_PAD - MLA_ROPE)))
    mla_cos = pad(jnp.concatenate([cm, cm], axis=-1))
    mla_sin = pad(jnp.concatenate([-sm, sm], axis=-1))
    return ret_cos, ret_sin, mla_cos, mla_sin


def _swap_cols(w):
    half = w.shape[-1] // 2
    return jnp.concatenate([w[..., half:], w[..., :half]], axis=-1)


def _mla_weights(w_uq, w_ukv):
    depth = w_uq.shape[0]
    uq = w_uq.reshape(depth, MLA_Q_RANK, MLA_HEADS, MLA_QK)
    q_nope, q_rope = uq[..., :MLA_NOPE], uq[..., MLA_NOPE:]
    zq = jnp.zeros((depth, MLA_Q_RANK, MLA_HEADS, HEAD_PAD - MLA_QK), w_uq.dtype)
    wq1 = jnp.concatenate([q_nope, q_rope, zq], axis=-1)
    wq2 = jnp.concatenate([jnp.zeros_like(q_nope), _swap_cols(q_rope), zq], axis=-1)
    ukv = w_ukv.reshape(depth, MLA_KV_RANK, MLA_HEADS, MLA_NOPE + MLA_V)
    k_nope, v = ukv[..., :MLA_NOPE], ukv[..., MLA_NOPE:]
    wk = jnp.concatenate([k_nope, jnp.zeros_like(k_nope)], axis=-1)
    wvt = jnp.swapaxes(v.reshape(depth, MLA_KV_RANK, MLA_HEADS * MLA_V), 1, 2)
    flat = lambda a: a.reshape(depth, a.shape[1], MLA_HEADS * HEAD_PAD).astype(BF16)
    return flat(wq1), flat(wq2), flat(wk), wvt.astype(BF16)


def kernel(x, p, positions, ffn1_up, ffn1_down, ln1_g, ln1_b, w_in, sgu_ln_g, sgu_ln_b, sgu_w, sgu_b,
           pool_w, pool_scale, mla_q_norm, mla_kv_norm, mla_w_uq, mla_w_ukv, w_branch, w_out,
           ln2_g, ln2_b, ffn2_up, ffn2_down, w_ple_gate, w_ple, ln3_g, ln3_b):
    bsz, seq, dm = x.shape
    n = bsz * seq
    ret_cos, ret_sin, mla_cos, mla_sin = _rope_tables(positions)
    mla_cos, mla_sin = mla_cos.reshape(n, HEAD_PAD), mla_sin.reshape(n, HEAD_PAD)
    bf = lambda a: a.astype(BF16)
    rows = lambda a: a[:, None, :]

    up1, down1, up2, down2 = bf(ffn1_up), bf(ffn1_down), bf(ffn2_up), bf(ffn2_down)
    w_mix = bf(w_in[:, :, :MIX_COLS])
    w_gate = bf(w_in[:, :, _OFF_GATE:])
    wb, wo, wpg, wp = bf(w_branch), bf(w_out), bf(w_ple_gate), bf(w_ple)
    wq1, wq2, wk, wvt = _mla_weights(mla_w_uq, mla_w_ukv)
    sgu_bias = jnp.repeat(jnp.swapaxes(sgu_b, 1, 2), SGU_WIDTH // SGU_GROUPS, axis=2)
    groups = len(POOL_WINDOWS)
    pool_bd = bf(jnp.einsum("lgcd,gh->lgchd", pool_w, jnp.eye(groups, dtype=pool_w.dtype))
                 .reshape(DEPTH, BRANCH_WIDTH, BRANCH_WIDTH))
    p2 = p.reshape(DEPTH, n, PLE_DIM)

    h = x.reshape(n, dm)
    for i in range(DEPTH):
        h = _ffn_ln(i, h, up1, down1, rows(ln1_g), rows(ln1_b))
        h3 = h.reshape(bsz, seq, dm)
        y_a = _sgu(i, h, w_mix, rows(sgu_ln_g), rows(sgu_ln_b), sgu_w, sgu_bias)
        y_b = _retention(i, h3, w_mix, ret_cos, ret_sin)
        y_c = _pool(i, h3, w_mix, pool_bd, rows(pool_scale))
        q, k, vt = _mla_pre(i, h, w_mix, mla_cos, mla_sin, rows(mla_q_norm), rows(mla_kv_norm),
                            wq1, wq2, wk, wvt)
        y_d = _attention(q.reshape(bsz, seq, -1), k.reshape(bsz, seq, -1),
                         vt.reshape(bsz, seq // ATT_TK, MLA_HEADS * MLA_V, ATT_TK))
        ys = (y_a, y_b.reshape(n, -1), y_c.reshape(n, -1), y_d.reshape(n, -1))
        h = _merge(i, h, ys, w_gate, wb, wo, rows(ln2_g), rows(ln2_b))
        h = _ffn_ple_ln(i, h, p2, up2, down2, wpg, wp, rows(ln3_g), rows(ln3_b))
    return h.reshape(bsz, seq, dm)
```

```python
import math

import numpy as np
import jax
import jax.numpy as jnp
from jax import lax
from jax.experimental import pallas as pl
from jax.experimental.pallas import tpu as pltpu

D_MODEL = 1024
DEPTH = 2
CHUNK = 64
SGU_WIDTH = 256
SGU_BLOCK = 128
SGU_GROUPS = 4
RET_HEADS = 4
RET_KDIM = 64
POOL_WINDOWS = (2, 4, 8, 16)
POOL_HALO = 16
MLA_HEADS = 4
MLA_NOPE = 64
MLA_ROPE = 32
MLA_QK = MLA_NOPE + MLA_ROPE
MLA_V = 64
MLA_Q_RANK = 256
MLA_KV_RANK = 128
HEAD_PAD = 128
ROPE_BASE = 10000.0
N_BRANCH = 4
BRANCH_WIDTH = 256
D_FF = 2816
PLE_DIM = 256
ALPHA = (2 * DEPTH) ** 0.25
LN_EPS = 1e-5
RMS_EPS = 1e-6
GN_EPS = 1e-5

_OFF_SGU = 0
_OFF_RET = 512
_OFF_POOL = 1536
_OFF_CQ = 1792
_OFF_GATE = 2208
MIX_COLS = 2304

TM = 512
TM_WIDE = 1024
SUB = 256
RET_TILE = 256
ATT_TQ = 512
ATT_TK = 256
ATT_ONES = 16
VMEM_LIMIT = 56 * 1024 * 1024

BF16 = jnp.bfloat16
F32 = jnp.float32


def _dot(a, b):
    return jnp.dot(a, b, preferred_element_type=F32)


def _dot_nt(a, b):
    return lax.dot_general(a, b, (((1,), (1,)), ((), ())), preferred_element_type=F32)


def _dot_tn(a, b):
    return lax.dot_general(a, b, (((0,), (0,)), ((), ())), preferred_element_type=F32)


def _layer_norm(r, g, b, eps):
    mu = jnp.mean(r, axis=-1, keepdims=True)
    d = r - mu
    var = jnp.mean(d * d, axis=-1, keepdims=True)
    return d * lax.rsqrt(var + eps) * g + b


def _const_spec(shape):
    zeros = (0,) * len(shape)
    return pl.BlockSpec(shape, lambda *_: zeros, pipeline_mode=pl.Buffered(1))


def _layer_spec(arr, layer):
    index = (layer,) + (0,) * (arr.ndim - 1)
    return pl.BlockSpec((None,) + arr.shape[1:], lambda *_: index, pipeline_mode=pl.Buffered(1))


def _params(n_grid, semantics="parallel"):
    return pltpu.CompilerParams(
        dimension_semantics=(semantics,) * n_grid, vmem_limit_bytes=VMEM_LIMIT)


def _pipelined_rows(o_ref, g_ref, b_ref, residual_fn):
    pending = None
    for s in range(o_ref.shape[0] // SUB):
        rows = slice(s * SUB, (s + 1) * SUB)
        r = residual_fn(rows)
        if pending is not None:
            o_ref[pending[0], :] = _layer_norm(pending[1], g_ref[...], b_ref[...], LN_EPS)
        pending = (rows, r)
    o_ref[pending[0], :] = _layer_norm(pending[1], g_ref[...], b_ref[...], LN_EPS)


def _ffn_body(x, up_ref, down_ref):
    xb = x.astype(BF16)
    a = _dot(xb, up_ref[:, :D_FF])
    b = _dot(xb, up_ref[:, D_FF:])
    h = (a * jax.nn.sigmoid(a) * b).astype(BF16)
    return xb, _dot(h, down_ref[...])


def _ffn_ln_kernel(x_ref, up_ref, down_ref, g_ref, b_ref, o_ref):
    def residual(rows):
        x = x_ref[rows, :]
        _, y = _ffn_body(x, up_ref, down_ref)
        return ALPHA * x + 0.5 * y

    _pipelined_rows(o_ref, g_ref, b_ref, residual)


def _ffn_ple_ln_kernel(x_ref, p_ref, up_ref, down_ref, wpg_ref, wp_ref, g_ref, b_ref, o_ref):
    def residual(rows):
        x = x_ref[rows, :]
        xb, y = _ffn_body(x, up_ref, down_ref)
        ple = jax.nn.sigmoid(_dot(xb, wpg_ref[...])) * _dot(p_ref[rows, :].astype(BF16), wp_ref[...])
        return ALPHA * x + 0.5 * y + ple

    _pipelined_rows(o_ref, g_ref, b_ref, residual)


def _ffn_ln(layer, x, up, down, g, b):
    n = x.shape[0]
    row = pl.BlockSpec((TM_WIDE, D_MODEL), lambda i: (i, 0))
    return pl.pallas_call(
        _ffn_ln_kernel,
        grid=(n // TM_WIDE,),
        in_specs=[row] + [_layer_spec(a, layer) for a in (up, down, g, b)],
        out_specs=row,
        out_shape=jax.ShapeDtypeStruct((n, D_MODEL), F32),
        compiler_params=_params(1),
        name="ffn_ln",
    )(x, up, down, g, b)


def _ffn_ple_ln(layer, x, p, up, down, wpg, wp, g, b):
    n = x.shape[0]
    row = pl.BlockSpec((TM_WIDE, D_MODEL), lambda i: (i, 0))
    return pl.pallas_call(
        _ffn_ple_ln_kernel,
        grid=(n // TM_WIDE,),
        in_specs=[row, pl.BlockSpec((None, TM_WIDE, PLE_DIM), lambda i: (layer, i, 0))]
                 + [_layer_spec(a, layer) for a in (up, down, wpg, wp, g, b)],
        out_specs=row,
        out_shape=jax.ShapeDtypeStruct((n, D_MODEL), F32),
        compiler_params=_params(1),
        name="ffn_ple_ln",
    )(x, p, up, down, wpg, wp, g, b)


def _swap_halves(x, half):
    width = x.shape[-1]
    lane = lax.broadcasted_iota(jnp.int32, x.shape, x.ndim - 1)
    first = (lane % (2 * half)) < half
    return jnp.where(first, pltpu.roll(x, width - half, x.ndim - 1), pltpu.roll(x, half, x.ndim - 1))


def _sgu_part(uv, lng_ref, lnb_ref, ws_ref, bias_ref, o_ref):
    u = jax.nn.gelu(uv[:, :SGU_WIDTH])
    v = _layer_norm(jax.nn.gelu(uv[:, SGU_WIDTH:]), lng_ref[...], lnb_ref[...], LN_EPS)
    vb = v.astype(BF16)
    t_row = lax.broadcasted_iota(jnp.int32, (SGU_BLOCK, SGU_BLOCK), 0)
    t_col = lax.broadcasted_iota(jnp.int32, (SGU_BLOCK, SGU_BLOCK), 1)
    causal = t_row >= t_col
    w_groups = [jnp.where(causal, ws_ref[g], 0.0).astype(BF16) for g in range(SGU_GROUPS)]
    group = lax.broadcasted_iota(jnp.int32, (SGU_BLOCK, SGU_WIDTH), 1) // (SGU_WIDTH // SGU_GROUPS)
    bias = bias_ref[...]
    for blk in range(TM // SGU_BLOCK):
        rows = slice(blk * SGU_BLOCK, (blk + 1) * SGU_BLOCK)
        v_blk = vb[rows]
        mixed = bias
        for g in range(SGU_GROUPS):
            mixed = mixed + jnp.where(group == g, _dot(w_groups[g], v_blk), 0.0)
        o_ref[rows, :] = (u[rows] * mixed).astype(o_ref.dtype)


def _ret_part(proj, cos, sin, dmask_ref, rowdec_ref, keydec_ref, tiledec_ref, o_ref, state_ref):
    width = RET_HEADS * RET_KDIM
    q = proj[:, :width]
    k = proj[:, width:2 * width]
    q = q * cos + _swap_halves(q, RET_KDIM // 2) * sin
    k = (k * cos + _swap_halves(k, RET_KDIM // 2) * sin) * RET_KDIM ** -0.5
    v = proj[:, 2 * width:3 * width]
    gate = proj[:, 3 * width:]
    vb = v.astype(BF16)
    kb = k.astype(BF16)

    head = lax.broadcasted_iota(jnp.int32, (RET_TILE, width), 1) // RET_KDIM
    y = _dot(q.astype(BF16), state_ref[...].astype(BF16)) * rowdec_ref[...]
    for h in range(RET_HEADS):
        qh = jnp.where(head == h, q, 0.0).astype(BF16)
        scores = _dot_nt(qh, kb) * dmask_ref[h]
        y = y + jnp.where(head == h, _dot(scores.astype(BF16), vb), 0.0)

    kd = (k * keydec_ref[...]).astype(BF16)
    row_head = lax.broadcasted_iota(jnp.int32, (width, width), 0) // RET_KDIM
    col_head = lax.broadcasted_iota(jnp.int32, (width, width), 1) // RET_KDIM
    kv = jnp.where(row_head == col_head, _dot_tn(kd, vb), 0.0)
    state_ref[...] = state_ref[...] * tiledec_ref[...] + kv

    inv = 1.0 / RET_KDIM
    mu = jnp.zeros_like(y)
    for h in range(RET_HEADS):
        s = jnp.sum(jnp.where(head == h, y, 0.0), axis=1, keepdims=True) * inv
        mu = jnp.where(head == h, s, mu)
    d = y - mu
    var = jnp.zeros_like(y)
    for h in range(RET_HEADS):
        s = jnp.sum(jnp.where(head == h, d * d, 0.0), axis=1, keepdims=True) * inv
        var = jnp.where(head == h, s, var)
    yn = d * lax.rsqrt(var + GN_EPS)
    o_ref[...] = (gate * jax.nn.sigmoid(gate) * yn).astype(o_ref.dtype)


def _retention_tables():
    heads = np.arange(RET_HEADS, dtype=np.float64)
    log_gamma = np.log1p(-np.exp2(-5.0 - heads))
    t = np.arange(RET_TILE)
    chunk = t // CHUNK
    diff = (t[:, None] - t[None, :]).astype(np.float64)
    same = chunk[:, None] == chunk[None, :]
    earlier = chunk[None, :] < chunk[:, None]
    expo = np.where(same, np.abs(diff), diff)
    dmask = np.where((same | earlier)[None], np.exp(log_gamma[:, None, None] * expo[None]), 0.0)
    rowdec = np.exp(log_gamma[None, :] * (t[:, None] + 1.0))
    keydec = np.exp(log_gamma[None, :] * (RET_TILE - 1.0 - t[:, None]))
    tiledec = np.exp(log_gamma * RET_TILE)
    rep = lambda a: np.repeat(a, RET_KDIM, axis=-1)
    width = RET_HEADS * RET_KDIM
    tiledec_full = np.broadcast_to(rep(tiledec[None, :]).T, (width, width))
    return (jnp.asarray(dmask, F32), jnp.asarray(rep(rowdec), F32), jnp.asarray(rep(keydec), F32),
            jnp.asarray(tiledec_full, F32))


def _pool_part(z, wp_ref, scale_ref, o_ref, zext_ref, seq_tile):
    zext_ref[POOL_HALO:POOL_HALO + TM, :] = z
    group = lax.broadcasted_iota(jnp.int32, (TM, BRANCH_WIDTH), 1) // (BRANCH_WIDTH // len(POOL_WINDOWS))
    acc = z
    win = jnp.zeros_like(z)
    window = jnp.zeros((TM, BRANCH_WIDTH), jnp.int32)
    for d in range(1, POOL_HALO):
        acc = acc + zext_ref[POOL_HALO - d:POOL_HALO - d + TM, :]
        if d + 1 in POOL_WINDOWS:
            gi = POOL_WINDOWS.index(d + 1)
            win = jnp.where(group == gi, acc, win)
            window = jnp.where(group == gi, d + 1, window)
    t = seq_tile * TM + lax.broadcasted_iota(jnp.int32, (TM, BRANCH_WIDTH), 0)
    count = jnp.minimum(t + 1, window).astype(F32)
    pooled = win / count - z
    y = _dot(pooled.astype(BF16), wp_ref[...]) * scale_ref[...]
    o_ref[...] = y.astype(o_ref.dtype)
    zext_ref[0:POOL_HALO, :] = zext_ref[TM:TM + POOL_HALO, :]


def _rms_norm(x, g):
    return x * lax.rsqrt(jnp.mean(x * x, axis=-1, keepdims=True) + RMS_EPS) * g


def _mla_part(proj, ck_ref, sk_ref, qg_ref, kvg_ref, wq1_ref, wq2_ref, wk_ref, wvt_ref,
              q_ref, k_ref, vt_ref):
    cq = _rms_norm(proj[:, :MLA_Q_RANK], qg_ref[...]).astype(BF16)
    ckv = _rms_norm(proj[:, MLA_Q_RANK:MLA_Q_RANK + MLA_KV_RANK], kvg_ref[...]).astype(BF16)
    k_raw = proj[:, MLA_Q_RANK + MLA_KV_RANK:]
    k_pe = pltpu.roll(k_raw * ck_ref[...] + _swap_halves(k_raw, MLA_ROPE // 2) * sk_ref[...], MLA_NOPE, 1)
    ck = pltpu.roll(ck_ref[...], MLA_NOPE, 1)
    sk = pltpu.roll(sk_ref[...], MLA_NOPE, 1)
    lane = lax.broadcasted_iota(jnp.int32, ck.shape, 1)
    scale = MLA_QK ** -0.5 * math.log2(math.e)
    cq_tab = (ck + jnp.where(lane < MLA_NOPE, 1.0, 0.0)) * scale
    sq_tab = sk * scale
    tile4 = lambda a: jnp.concatenate([a] * MLA_HEADS, axis=1)
    q = _dot(cq, wq1_ref[...]) * tile4(cq_tab) + _dot(cq, wq2_ref[...]) * tile4(sq_tab)
    k = _dot(ckv, wk_ref[...]) + tile4(k_pe)
    q_ref[...] = q.astype(q_ref.dtype)
    k_ref[...] = k.astype(k_ref.dtype)
    for t in range(TM // ATT_TK):
        vt_ref[t] = _dot_nt(wvt_ref[...], ckv[t * ATT_TK:(t + 1) * ATT_TK]).astype(vt_ref.dtype)


def _mixers_kernel(x_ref, w_ref, rcos_ref, rsin_ref, ck_ref, sk_ref,
                   lng_ref, lnb_ref, ws_ref, bias_ref, wp_ref, scale_ref,
                   qg_ref, kvg_ref, wq1_ref, wq2_ref, wk_ref, wvt_ref,
                   dmask_ref, rowdec_ref, keydec_ref, tiledec_ref,
                   ya_ref, yb_ref, yc_ref, q_ref, k_ref, vt_ref, state_ref, zext_ref):
    seq_tile = pl.program_id(1)

    @pl.when(seq_tile == 0)
    def _():
        state_ref[...] = jnp.zeros_like(state_ref)
        zext_ref[0:POOL_HALO, :] = jnp.zeros((POOL_HALO, BRANCH_WIDTH), F32)

    xb = x_ref[...].astype(BF16)
    p_sgu = _dot(xb, w_ref[:, _OFF_SGU:_OFF_RET])
    p_pool = _dot(xb, w_ref[:, _OFF_POOL:_OFF_CQ])
    p_mla = _dot(xb, w_ref[:, _OFF_CQ:MIX_COLS])
    p_ret = _dot(xb, w_ref[:, _OFF_RET:_OFF_POOL])
    _sgu_part(p_sgu, lng_ref, lnb_ref, ws_ref, bias_ref, ya_ref)
    _pool_part(p_pool, wp_ref, scale_ref, yc_ref, zext_ref, seq_tile)
    _mla_part(p_mla, ck_ref, sk_ref, qg_ref, kvg_ref, wq1_ref, wq2_ref, wk_ref, wvt_ref,
              q_ref, k_ref, vt_ref)
    for r in range(TM // RET_TILE):
        rows = slice(r * RET_TILE, (r + 1) * RET_TILE)
        cos = jnp.concatenate([rcos_ref[rows, :], rcos_ref[rows, :]], axis=1)
        sin = jnp.concatenate([rsin_ref[rows, :], rsin_ref[rows, :]], axis=1)
        _ret_part(p_ret[rows], cos, sin, dmask_ref, rowdec_ref, keydec_ref, tiledec_ref,
                  yb_ref.at[rows, :], state_ref)


def _mixers(layer, x3, w_mix, tables, sgu_params, pool_params, mla_params):
    bsz, seq, _ = x3.shape
    width = RET_HEADS * RET_KDIM
    ret_tables = _retention_tables()
    tile = lambda last: pl.BlockSpec((None, TM, last), lambda b, s: (b, s, 0))
    layered = (w_mix,) + tuple(sgu_params) + tuple(pool_params) + tuple(mla_params)
    in_specs = ([tile(D_MODEL), _layer_spec(w_mix, layer)] + [tile(HEAD_PAD)] * 4
                + [_layer_spec(a, layer) for a in layered[1:]]
                + [_const_spec(a.shape) for a in ret_tables])
    branch = jax.ShapeDtypeStruct((bsz, seq, BRANCH_WIDTH), BF16)
    qk = jax.ShapeDtypeStruct((bsz, seq, MLA_HEADS * HEAD_PAD), BF16)
    vt = jax.ShapeDtypeStruct((bsz, seq // ATT_TK, MLA_HEADS * MLA_V, ATT_TK), BF16)
    return pl.pallas_call(
        _mixers_kernel,
        grid=(bsz, seq // TM),
        in_specs=in_specs,
        out_specs=[tile(BRANCH_WIDTH)] * 3 + [tile(MLA_HEADS * HEAD_PAD)] * 2
                  + [pl.BlockSpec((None, TM // ATT_TK, MLA_HEADS * MLA_V, ATT_TK), lambda b, s: (b, s, 0, 0))],
        out_shape=[branch, branch, branch, qk, qk, vt],
        scratch_shapes=[pltpu.VMEM((width, width), F32),
                        pltpu.VMEM((TM + POOL_HALO, BRANCH_WIDTH), F32)],
        compiler_params=_params(2, "arbitrary"),
        name="token_mixers",
    )(x3, w_mix, *tables, *sgu_params, *pool_params, *mla_params, *ret_tables)


def _attn_kernel(q_ref, k_ref, vt_ref, o_ref, st_ref, m_ref, acc_ref):
    qi = pl.program_id(1)
    key_chunk = lax.broadcasted_iota(jnp.int32, (ATT_TK, ATT_TQ), 0) // CHUNK
    qry_chunk = lax.broadcasted_iota(jnp.int32, (ATT_TK, ATT_TQ), 1) // CHUNK
    diag_masks = [key_chunk + t * (ATT_TK // CHUNK) <= qry_chunk for t in range(ATT_TQ // ATT_TK)]
    heads = range(MLA_HEADS)

    def scores(j, slot):
        rows = pl.ds(pl.multiple_of(j * ATT_TK, ATT_TK), ATT_TK)
        for h in heads:
            cols = slice(h * HEAD_PAD, (h + 1) * HEAD_PAD)
            st_ref[slot, h] = _dot_nt(k_ref[rows, cols], q_ref[:, cols])

    ones = jnp.ones((ATT_ONES, ATT_TK), BF16)

    def softmax_pv(j, slot, mask=None):
        for h in heads:
            load = lambda: (st_ref[slot, h] if mask is None
                            else jnp.where(mask, st_ref[slot, h], -jnp.inf))
            m = m_ref[h]
            m_new = jnp.maximum(m, jnp.max(load(), axis=0, keepdims=True))
            p = jnp.exp2((load() - m_new).astype(BF16))
            lhs = jnp.concatenate([vt_ref[j, h * MLA_V:(h + 1) * MLA_V, :], ones], axis=0)
            acc_ref[h] = jnp.exp2(m - m_new) * acc_ref[h] + _dot(lhs, p)
            m_ref[h] = m_new

    def pair(i, carry):
        scores(2 * i + 1, 1)
        softmax_pv(2 * i, 0)
        scores(2 * i + 2, 0)
        softmax_pv(2 * i + 1, 1)
        return carry

    m_ref[...] = jnp.full(m_ref.shape, -jnp.inf, F32)
    acc_ref[...] = jnp.zeros(acc_ref.shape, F32)
    scores(0, 0)
    lax.fori_loop(0, qi, pair, 0)
    scores(2 * qi + 1, 1)
    softmax_pv(2 * qi, 0, diag_masks[0])
    softmax_pv(2 * qi + 1, 1, diag_masks[1])
    out_t = jnp.concatenate([acc_ref[h, :MLA_V] / acc_ref[h, MLA_V:MLA_V + 1] for h in heads], axis=0)
    o_ref[...] = out_t.T.astype(o_ref.dtype)


def _attention(q3, k3, vt4):
    bsz, seq, _ = q3.shape
    return pl.pallas_call(
        _attn_kernel,
        grid=(bsz, seq // ATT_TQ),
        in_specs=[pl.BlockSpec((None, ATT_TQ, MLA_HEADS * HEAD_PAD), lambda b, i: (b, i, 0)),
                  pl.BlockSpec((None, seq, MLA_HEADS * HEAD_PAD), lambda b, i: (b, 0, 0)),
                  pl.BlockSpec((None, seq // ATT_TK, MLA_HEADS * MLA_V, ATT_TK), lambda b, i: (b, 0, 0, 0))],
        out_specs=pl.BlockSpec((None, ATT_TQ, MLA_HEADS * MLA_V), lambda b, i: (b, i, 0)),
        out_shape=jax.ShapeDtypeStruct((bsz, seq, MLA_HEADS * MLA_V), BF16),
        scratch_shapes=[pltpu.VMEM((2, MLA_HEADS, ATT_TK, ATT_TQ), F32),
                        pltpu.VMEM((MLA_HEADS, 1, ATT_TQ), F32),
                        pltpu.VMEM((MLA_HEADS, MLA_V + ATT_ONES, ATT_TQ), F32)],
        compiler_params=_params(2),
        name="mla_attention",
    )(q3, k3, vt4)


def _merge_kernel(x_ref, ya_ref, yb_ref, yc_ref, yd_ref, wg_ref, wb_ref, wo_ref, g_ref, b_ref, o_ref):
    def residual(rows):
        x = x_ref[rows, :]
        xb = x.astype(BF16)
        merged = None
        for n, y_ref in enumerate((ya_ref, yb_ref, yc_ref, yd_ref)):
            gate = jax.nn.sigmoid(_dot(xb, wg_ref[:, n * D_MODEL:(n + 1) * D_MODEL]))
            term = gate * _dot(y_ref[rows, :], wb_ref[n])
            merged = term if merged is None else merged + term
        return ALPHA * x + _dot(merged.astype(BF16), wo_ref[...])

    _pipelined_rows(o_ref, g_ref, b_ref, residual)


def _merge(layer, x, ys, wg, wb, wo, g, b):
    n = x.shape[0]
    row = lambda last: pl.BlockSpec((TM_WIDE, last), lambda i: (i, 0))
    return pl.pallas_call(
        _merge_kernel,
        grid=(n // TM_WIDE,),
        in_specs=[row(D_MODEL)] + [row(BRANCH_WIDTH)] * N_BRANCH
                 + [_layer_spec(a, layer) for a in (wg, wb, wo, g, b)],
        out_specs=row(D_MODEL),
        out_shape=jax.ShapeDtypeStruct((n, D_MODEL), F32),
        compiler_params=_params(1),
        name="merge_ln",
    )(x, *ys, wg, wb, wo, g, b)


def _rope_tables(positions):
    pos = positions.astype(F32)[..., None]
    ang_r = pos * (ROPE_BASE ** (-jnp.arange(0, RET_KDIM, 2, dtype=F32) / RET_KDIM))
    cr, sr = jnp.cos(ang_r), jnp.sin(ang_r)
    ret_cos = jnp.tile(jnp.concatenate([cr, cr], axis=-1), (1, 1, 2))
    ret_sin = jnp.tile(jnp.concatenate([-sr, sr], axis=-1), (1, 1, 2))
    ang_m = pos * (ROPE_BASE ** (-jnp.arange(0, MLA_ROPE, 2, dtype=F32) / MLA_ROPE))
    cm, sm = jnp.cos(ang_m), jnp.sin(ang_m)
    pad = lambda a: jnp.pad(a, ((0, 0), (0, 0), (0, HEAD_PAD - MLA_ROPE)))
    mla_cos = pad(jnp.concatenate([cm, cm], axis=-1))
    mla_sin = pad(jnp.concatenate([-sm, sm], axis=-1))
    return ret_cos, ret_sin, mla_cos, mla_sin


def _swap_cols(w):
    half = w.shape[-1] // 2
    return jnp.concatenate([w[..., half:], w[..., :half]], axis=-1)


def _mla_weights(w_uq, w_ukv):
    depth = w_uq.shape[0]
    uq = w_uq.reshape(depth, MLA_Q_RANK, MLA_HEADS, MLA_QK)
    q_nope, q_rope = uq[..., :MLA_NOPE], uq[..., MLA_NOPE:]
    zq = jnp.zeros((depth, MLA_Q_RANK, MLA_HEADS, HEAD_PAD - MLA_QK), w_uq.dtype)
    wq1 = jnp.concatenate([q_nope, q_rope, zq], axis=-1)
    wq2 = jnp.concatenate([jnp.zeros_like(q_nope), _swap_cols(q_rope), zq], axis=-1)
    ukv = w_ukv.reshape(depth, MLA_KV_RANK, MLA_HEADS, MLA_NOPE + MLA_V)
    k_nope, v = ukv[..., :MLA_NOPE], ukv[..., MLA_NOPE:]
    wk = jnp.concatenate([k_nope, jnp.zeros_like(k_nope)], axis=-1)
    wvt = jnp.swapaxes(v.reshape(depth, MLA_KV_RANK, MLA_HEADS * MLA_V), 1, 2)
    flat = lambda a: a.reshape(depth, a.shape[1], MLA_HEADS * HEAD_PAD).astype(BF16)
    return flat(wq1), flat(wq2), flat(wk), wvt.astype(BF16)


def kernel(x, p, positions, ffn1_up, ffn1_down, ln1_g, ln1_b, w_in, sgu_ln_g, sgu_ln_b, sgu_w, sgu_b,
           pool_w, pool_scale, mla_q_norm, mla_kv_norm, mla_w_uq, mla_w_ukv, w_branch, w_out,
           ln2_g, ln2_b, ffn2_up, ffn2_down, w_ple_gate, w_ple, ln3_g, ln3_b):
    bsz, seq, dm = x.shape
    n = bsz * seq
    tables = _rope_tables(positions)
    bf = lambda a: a.astype(BF16)
    rows = lambda a: a[:, None, :]

    up1, down1, up2, down2 = bf(ffn1_up), bf(ffn1_down), bf(ffn2_up), bf(ffn2_down)
    w_mix = bf(w_in[:, :, :MIX_COLS])
    w_gate = bf(w_in[:, :, _OFF_GATE:])
    wb, wo, wpg, wp = bf(w_branch), bf(w_out), bf(w_ple_gate), bf(w_ple)
    wq1, wq2, wk, wvt = _mla_weights(mla_w_uq, mla_w_ukv)
    sgu_bias = jnp.repeat(jnp.swapaxes(sgu_b, 1, 2), SGU_WIDTH // SGU_GROUPS, axis=2)
    groups = len(POOL_WINDOWS)
    pool_bd = bf(jnp.einsum("lgcd,gh->lgchd", pool_w, jnp.eye(groups, dtype=pool_w.dtype))
                 .reshape(DEPTH, BRANCH_WIDTH, BRANCH_WIDTH))
    p2 = p.reshape(DEPTH, n, PLE_DIM)

    h = x.reshape(n, dm)
    for i in range(DEPTH):
        h = _ffn_ln(i, h, up1, down1, rows(ln1_g), rows(ln1_b))
        y_a, y_b, y_c, q, k, vt = _mixers(
            i, h.reshape(bsz, seq, dm), w_mix, tables,
            (rows(sgu_ln_g), rows(sgu_ln_b), sgu_w, sgu_bias), (pool_bd, rows(pool_scale)),
            (rows(mla_q_norm), rows(mla_kv_norm), wq1, wq2, wk, wvt))
        y_d = _attention(q, k, vt)
        ys = tuple(y.reshape(n, BRANCH_WIDTH) for y in (y_a, y_b, y_c, y_d))
        h = _merge(i, h, ys, w_gate, wb, wo, rows(ln2_g), rows(ln2_b))
        h = _ffn_ple_ln(i, h, p2, up2, down2, wpg, wp, rows(ln3_g), rows(ln3_b))
    return h.reshape(bsz, seq, dm)
```

```python
import math
from typing import NamedTuple

import numpy as np
import jax
import jax.numpy as jnp
from jax import lax
from jax.experimental import pallas as pl
from jax.experimental.pallas import tpu as pltpu

D_MODEL = 1024
DEPTH = 2
CHUNK = 64
SGU_WIDTH = 256
SGU_BLOCK = 128
SGU_GROUPS = 4
RET_HEADS = 4
RET_KDIM = 64
POOL_WINDOWS = (2, 4, 8, 16)
POOL_HALO = 16
MLA_HEADS = 4
MLA_NOPE = 64
MLA_ROPE = 32
MLA_QK = MLA_NOPE + MLA_ROPE
MLA_V = 64
MLA_Q_RANK = 256
MLA_KV_RANK = 128
HEAD_PAD = 128
ROPE_BASE = 10000.0
N_BRANCH = 4
BRANCH_WIDTH = 256
D_FF = 2816
PLE_DIM = 256
ALPHA = (2 * DEPTH) ** 0.25
LN_EPS = 1e-5
RMS_EPS = 1e-6
GN_EPS = 1e-5

_OFF_SGU = 0
_OFF_RET = 512
_OFF_POOL = 1536
_OFF_CQ = 1792
_OFF_GATE = 2208
MIX_COLS = 2304

TM = 512
TM_WIDE = 1024
SUB = 256
RET_TILE = 256
ATT_TQ = 512
ATT_TK = 256
ATT_ONES = 16
VMEM_LIMIT = 56 * 1024 * 1024

BF16 = jnp.bfloat16
F32 = jnp.float32


def _dot(a, b):
    return jnp.dot(a, b, preferred_element_type=F32)


def _dot_nt(a, b):
    return lax.dot_general(a, b, (((1,), (1,)), ((), ())), preferred_element_type=F32)


def _dot_tn(a, b):
    return lax.dot_general(a, b, (((0,), (0,)), ((), ())), preferred_element_type=F32)


def _layer_norm(r, g, b, eps):
    mu = jnp.mean(r, axis=-1, keepdims=True)
    d = r - mu
    var = jnp.mean(d * d, axis=-1, keepdims=True)
    return d * lax.rsqrt(var + eps) * g + b


def _const_spec(shape):
    zeros = (0,) * len(shape)
    return pl.BlockSpec(shape, lambda *_: zeros, pipeline_mode=pl.Buffered(1))


def _layer_spec(arr, layer):
    index = (layer,) + (0,) * (arr.ndim - 1)
    return pl.BlockSpec((None,) + arr.shape[1:], lambda *_: index, pipeline_mode=pl.Buffered(1))


def _params(n_grid, semantics="parallel"):
    return pltpu.CompilerParams(
        dimension_semantics=(semantics,) * n_grid, vmem_limit_bytes=VMEM_LIMIT)


class _Cast(NamedTuple):
    src: jax.Array
    layer: int
    lo: int
    hi: int


def _cast(src, layer, lo=0, hi=None):
    return _Cast(src, layer, lo, src.shape[2] if hi is None else hi)


def _cast_specs(casts, n_steps, step_of):
    in_specs, out_specs, out_shapes = [], [], []
    for c in casts:
        rows = c.src.shape[1] // n_steps
        in_specs.append(pl.BlockSpec((None, rows, c.src.shape[2]),
                                     lambda *g, layer=c.layer: (layer, step_of(*g), 0)))
        out_specs.append(pl.BlockSpec((rows, c.hi - c.lo), lambda *g: (step_of(*g), 0)))
        out_shapes.append(jax.ShapeDtypeStruct((c.src.shape[1], c.hi - c.lo), BF16))
    return in_specs, out_specs, out_shapes


def _with_casts(body, n_in, n_out, casts):
    n_cast = len(casts)

    def kernel(*refs):
        ins, rest = refs[:n_in], refs[n_in:]
        cast_in, rest = rest[:n_cast], rest[n_cast:]
        outs, rest = rest[:n_out], rest[n_out:]
        cast_out, scratch = rest[:n_cast], rest[n_cast:]
        body(*ins, *outs, *scratch)
        for c, src_ref, dst_ref in zip(casts, cast_in, cast_out):
            dst_ref[...] = src_ref[:, c.lo:c.hi].astype(BF16)

    return kernel


def _pipelined_rows(o_ref, g_ref, b_ref, residual_fn):
    pending = None
    for s in range(o_ref.shape[0] // SUB):
        rows = slice(s * SUB, (s + 1) * SUB)
        r = residual_fn(rows)
        if pending is not None:
            o_ref[pending[0], :] = _layer_norm(pending[1], g_ref[...], b_ref[...], LN_EPS)
        pending = (rows, r)
    o_ref[pending[0], :] = _layer_norm(pending[1], g_ref[...], b_ref[...], LN_EPS)


def _ffn_body(x, up_ref, down_ref):
    xb = x.astype(BF16)
    a = _dot(xb, up_ref[:, :D_FF])
    b = _dot(xb, up_ref[:, D_FF:])
    h = (a * jax.nn.sigmoid(a) * b).astype(BF16)
    return xb, _dot(h, down_ref[...])


def _ffn_ln_kernel(x_ref, up_ref, down_ref, g_ref, b_ref, o_ref):
    def residual(rows):
        x = x_ref[rows, :]
        _, y = _ffn_body(x, up_ref, down_ref)
        return ALPHA * x + 0.5 * y

    _pipelined_rows(o_ref, g_ref, b_ref, residual)


def _ffn_ple_ln_kernel(x_ref, p_ref, up_ref, down_ref, wpg_ref, wp_ref, g_ref, b_ref, o_ref):
    def residual(rows):
        x = x_ref[rows, :]
        xb, y = _ffn_body(x, up_ref, down_ref)
        ple = jax.nn.sigmoid(_dot(xb, wpg_ref[...])) * _dot(p_ref[rows, :].astype(BF16), wp_ref[...])
        return ALPHA * x + 0.5 * y + ple

    _pipelined_rows(o_ref, g_ref, b_ref, residual)


def _ffn_ln(layer, x, up, down, g, b, casts=()):
    n = x.shape[0]
    steps = n // TM_WIDE
    row = pl.BlockSpec((TM_WIDE, D_MODEL), lambda i: (i, 0))
    cast_in, cast_out, cast_shapes = _cast_specs(casts, steps, lambda i: i)
    return pl.pallas_call(
        _with_casts(_ffn_ln_kernel, 5, 1, casts),
        grid=(steps,),
        in_specs=[row, _const_spec(up.shape), _const_spec(down.shape), _layer_spec(g, layer),
                  _layer_spec(b, layer)] + cast_in,
        out_specs=[row] + cast_out,
        out_shape=[jax.ShapeDtypeStruct((n, D_MODEL), F32)] + cast_shapes,
        compiler_params=_params(1),
        name="ffn_ln",
    )(x, up, down, g, b, *[c.src for c in casts])


def _ffn_ple_ln(layer, x, p, up, down, wpg, wp, g, b):
    n = x.shape[0]
    row = pl.BlockSpec((TM_WIDE, D_MODEL), lambda i: (i, 0))
    return pl.pallas_call(
        _ffn_ple_ln_kernel,
        grid=(n // TM_WIDE,),
        in_specs=[row, pl.BlockSpec((None, TM_WIDE, PLE_DIM), lambda i: (layer, i, 0)),
                  _const_spec(up.shape), _const_spec(down.shape), _const_spec(wpg.shape)]
                 + [_layer_spec(a, layer) for a in (wp, g, b)],
        out_specs=row,
        out_shape=jax.ShapeDtypeStruct((n, D_MODEL), F32),
        compiler_params=_params(1),
        name="ffn_ple_ln",
    )(x, p, up, down, wpg, wp, g, b)


def _swap_halves(x, half):
    width = x.shape[-1]
    lane = lax.broadcasted_iota(jnp.int32, x.shape, x.ndim - 1)
    first = (lane % (2 * half)) < half
    return jnp.where(first, pltpu.roll(x, width - half, x.ndim - 1), pltpu.roll(x, half, x.ndim - 1))


def _sgu_part(uv, lng_ref, lnb_ref, ws_ref, bias_ref, o_ref):
    u = jax.nn.gelu(uv[:, :SGU_WIDTH])
    v = _layer_norm(jax.nn.gelu(uv[:, SGU_WIDTH:]), lng_ref[...], lnb_ref[...], LN_EPS)
    vb = v.astype(BF16)
    t_row = lax.broadcasted_iota(jnp.int32, (SGU_BLOCK, SGU_BLOCK), 0)
    t_col = lax.broadcasted_iota(jnp.int32, (SGU_BLOCK, SGU_BLOCK), 1)
    causal = t_row >= t_col
    w_groups = [jnp.where(causal, ws_ref[g], 0.0).astype(BF16) for g in range(SGU_GROUPS)]
    group = lax.broadcasted_iota(jnp.int32, (SGU_BLOCK, SGU_WIDTH), 1) // (SGU_WIDTH // SGU_GROUPS)
    bias = bias_ref[...]
    for blk in range(TM // SGU_BLOCK):
        rows = slice(blk * SGU_BLOCK, (blk + 1) * SGU_BLOCK)
        v_blk = vb[rows]
        mixed = bias
        for g in range(SGU_GROUPS):
            mixed = mixed + jnp.where(group == g, _dot(w_groups[g], v_blk), 0.0)
        o_ref[rows, :] = (u[rows] * mixed).astype(o_ref.dtype)


def _ret_part(proj, cos, sin, dmask_ref, rowdec_ref, keydec_ref, tiledec_ref, o_ref, state_ref):
    width = RET_HEADS * RET_KDIM
    q = proj[:, :width]
    k = proj[:, width:2 * width]
    q = q * cos + _swap_halves(q, RET_KDIM // 2) * sin
    k = (k * cos + _swap_halves(k, RET_KDIM // 2) * sin) * RET_KDIM ** -0.5
    v = proj[:, 2 * width:3 * width]
    gate = proj[:, 3 * width:]
    vb = v.astype(BF16)
    kb = k.astype(BF16)

    head = lax.broadcasted_iota(jnp.int32, (RET_TILE, width), 1) // RET_KDIM
    y = _dot(q.astype(BF16), state_ref[...].astype(BF16)) * rowdec_ref[...]
    for h in range(RET_HEADS):
        qh = jnp.where(head == h, q, 0.0).astype(BF16)
        scores = _dot_nt(qh, kb) * dmask_ref[h]
        y = y + jnp.where(head == h, _dot(scores.astype(BF16), vb), 0.0)

    kd = (k * keydec_ref[...]).astype(BF16)
    row_head = lax.broadcasted_iota(jnp.int32, (width, width), 0) // RET_KDIM
    col_head = lax.broadcasted_iota(jnp.int32, (width, width), 1) // RET_KDIM
    kv = jnp.where(row_head == col_head, _dot_tn(kd, vb), 0.0)
    state_ref[...] = state_ref[...] * tiledec_ref[...] + kv

    inv = 1.0 / RET_KDIM
    mu = jnp.zeros_like(y)
    for h in range(RET_HEADS):
        s = jnp.sum(jnp.where(head == h, y, 0.0), axis=1, keepdims=True) * inv
        mu = jnp.where(head == h, s, mu)
    d = y - mu
    var = jnp.zeros_like(y)
    for h in range(RET_HEADS):
        s = jnp.sum(jnp.where(head == h, d * d, 0.0), axis=1, keepdims=True) * inv
        var = jnp.where(head == h, s, var)
    yn = d * lax.rsqrt(var + GN_EPS)
    o_ref[...] = (gate * jax.nn.sigmoid(gate) * yn).astype(o_ref.dtype)


def _retention_tables():
    heads = np.arange(RET_HEADS, dtype=np.float64)
    log_gamma = np.log1p(-np.exp2(-5.0 - heads))
    t = np.arange(RET_TILE)
    chunk = t // CHUNK
    diff = (t[:, None] - t[None, :]).astype(np.float64)
    same = chunk[:, None] == chunk[None, :]
    earlier = chunk[None, :] < chunk[:, None]
    expo = np.where(same, np.abs(diff), diff)
    dmask = np.where((same | earlier)[None], np.exp(log_gamma[:, None, None] * expo[None]), 0.0)
    rowdec = np.exp(log_gamma[None, :] * (t[:, None] + 1.0))
    keydec = np.exp(log_gamma[None, :] * (RET_TILE - 1.0 - t[:, None]))
    tiledec = np.exp(log_gamma * RET_TILE)
    rep = lambda a: np.repeat(a, RET_KDIM, axis=-1)
    width = RET_HEADS * RET_KDIM
    tiledec_full = np.broadcast_to(rep(tiledec[None, :]).T, (width, width))
    return (jnp.asarray(dmask, F32), jnp.asarray(rep(rowdec), F32), jnp.asarray(rep(keydec), F32),
            jnp.asarray(tiledec_full, F32))


def _pool_part(z, wp_ref, scale_ref, o_ref, zext_ref, seq_tile):
    zext_ref[POOL_HALO:POOL_HALO + TM, :] = z
    group = lax.broadcasted_iota(jnp.int32, (TM, BRANCH_WIDTH), 1) // (BRANCH_WIDTH // len(POOL_WINDOWS))
    acc = z
    win = jnp.zeros_like(z)
    window = jnp.zeros((TM, BRANCH_WIDTH), jnp.int32)
    for d in range(1, POOL_HALO):
        acc = acc + zext_ref[POOL_HALO - d:POOL_HALO - d + TM, :]
        if d + 1 in POOL_WINDOWS:
            gi = POOL_WINDOWS.index(d + 1)
            win = jnp.where(group == gi, acc, win)
            window = jnp.where(group == gi, d + 1, window)
    t = seq_tile * TM + lax.broadcasted_iota(jnp.int32, (TM, BRANCH_WIDTH), 0)
    count = jnp.minimum(t + 1, window).astype(F32)
    pooled = win / count - z
    y = _dot(pooled.astype(BF16), wp_ref[...]) * scale_ref[...]
    o_ref[...] = y.astype(o_ref.dtype)
    zext_ref[0:POOL_HALO, :] = zext_ref[TM:TM + POOL_HALO, :]


def _rms_norm(x, g):
    return x * lax.rsqrt(jnp.mean(x * x, axis=-1, keepdims=True) + RMS_EPS) * g


def _mla_part(proj, ck_ref, sk_ref, qg_ref, kvg_ref, wq1_ref, wq2_ref, wk_ref, wvt_ref,
              q_ref, k_ref, vt_ref):
    cq = _rms_norm(proj[:, :MLA_Q_RANK], qg_ref[...]).astype(BF16)
    ckv = _rms_norm(proj[:, MLA_Q_RANK:MLA_Q_RANK + MLA_KV_RANK], kvg_ref[...]).astype(BF16)
    k_raw = proj[:, MLA_Q_RANK + MLA_KV_RANK:]
    k_pe = pltpu.roll(k_raw * ck_ref[...] + _swap_halves(k_raw, MLA_ROPE // 2) * sk_ref[...], MLA_NOPE, 1)
    ck = pltpu.roll(ck_ref[...], MLA_NOPE, 1)
    sk = pltpu.roll(sk_ref[...], MLA_NOPE, 1)
    lane = lax.broadcasted_iota(jnp.int32, ck.shape, 1)
    scale = MLA_QK ** -0.5 * math.log2(math.e)
    cq_tab = (ck + jnp.where(lane < MLA_NOPE, 1.0, 0.0)) * scale
    sq_tab = sk * scale
    tile4 = lambda a: jnp.concatenate([a] * MLA_HEADS, axis=1)
    q = _dot(cq, wq1_ref[...]) * tile4(cq_tab) + _dot(cq, wq2_ref[...]) * tile4(sq_tab)
    k = _dot(ckv, wk_ref[...]) + tile4(k_pe)
    q_ref[...] = q.astype(q_ref.dtype)
    k_ref[...] = k.astype(k_ref.dtype)
    for t in range(TM // ATT_TK):
        vt_ref[t] = _dot_nt(wvt_ref[...], ckv[t * ATT_TK:(t + 1) * ATT_TK]).astype(vt_ref.dtype)


def _mixers_kernel(x_ref, w_ref, rcos_ref, rsin_ref, ck_ref, sk_ref,
                   lng_ref, lnb_ref, ws_ref, bias_ref, wp_ref, scale_ref,
                   qg_ref, kvg_ref, wq1_ref, wq2_ref, wk_ref, wvt_ref,
                   dmask_ref, rowdec_ref, keydec_ref, tiledec_ref,
                   ya_ref, yb_ref, yc_ref, q_ref, k_ref, vt_ref, state_ref, zext_ref):
    seq_tile = pl.program_id(1)

    @pl.when(seq_tile == 0)
    def _():
        state_ref[...] = jnp.zeros_like(state_ref)
        zext_ref[0:POOL_HALO, :] = jnp.zeros((POOL_HALO, BRANCH_WIDTH), F32)

    xb = x_ref[...].astype(BF16)
    p_sgu = _dot(xb, w_ref[:, _OFF_SGU:_OFF_RET])
    p_pool = _dot(xb, w_ref[:, _OFF_POOL:_OFF_CQ])
    p_mla = _dot(xb, w_ref[:, _OFF_CQ:MIX_COLS])
    p_ret = _dot(xb, w_ref[:, _OFF_RET:_OFF_POOL])
    _sgu_part(p_sgu, lng_ref, lnb_ref, ws_ref, bias_ref, ya_ref)
    _pool_part(p_pool, wp_ref, scale_ref, yc_ref, zext_ref, seq_tile)
    _mla_part(p_mla, ck_ref, sk_ref, qg_ref, kvg_ref, wq1_ref, wq2_ref, wk_ref, wvt_ref,
              q_ref, k_ref, vt_ref)
    for r in range(TM // RET_TILE):
        rows = slice(r * RET_TILE, (r + 1) * RET_TILE)
        cos = jnp.concatenate([rcos_ref[rows, :], rcos_ref[rows, :]], axis=1)
        sin = jnp.concatenate([rsin_ref[rows, :], rsin_ref[rows, :]], axis=1)
        _ret_part(p_ret[rows], cos, sin, dmask_ref, rowdec_ref, keydec_ref, tiledec_ref,
                  yb_ref.at[rows, :], state_ref)


def _mixers(layer, x3, w_mix, tables, sgu_params, pool_params, mla_params, casts=()):
    bsz, seq, _ = x3.shape
    width = RET_HEADS * RET_KDIM
    ret_tables = _retention_tables()
    seq_tiles = seq // TM
    tile = lambda last: pl.BlockSpec((None, TM, last), lambda b, s: (b, s, 0))
    layered = tuple(sgu_params) + tuple(pool_params) + tuple(mla_params)
    in_specs = ([tile(D_MODEL), _const_spec(w_mix.shape)] + [tile(HEAD_PAD)] * 4
                + [_layer_spec(a, layer) for a in layered]
                + [_const_spec(a.shape) for a in ret_tables])
    cast_in, cast_out, cast_shapes = _cast_specs(casts, bsz * seq_tiles, lambda b, s: b * seq_tiles + s)
    branch = jax.ShapeDtypeStruct((bsz, seq, BRANCH_WIDTH), BF16)
    qk = jax.ShapeDtypeStruct((bsz, seq, MLA_HEADS * HEAD_PAD), BF16)
    vt = jax.ShapeDtypeStruct((bsz, seq // ATT_TK, MLA_HEADS * MLA_V, ATT_TK), BF16)
    return pl.pallas_call(
        _with_casts(_mixers_kernel, len(in_specs), 6, casts),
        grid=(bsz, seq_tiles),
        in_specs=in_specs + cast_in,
        out_specs=[tile(BRANCH_WIDTH)] * 3 + [tile(MLA_HEADS * HEAD_PAD)] * 2
                  + [pl.BlockSpec((None, TM // ATT_TK, MLA_HEADS * MLA_V, ATT_TK), lambda b, s: (b, s, 0, 0))]
                  + cast_out,
        out_shape=[branch, branch, branch, qk, qk, vt] + cast_shapes,
        scratch_shapes=[pltpu.VMEM((width, width), F32),
                        pltpu.VMEM((TM + POOL_HALO, BRANCH_WIDTH), F32)],
        compiler_params=_params(2, "arbitrary"),
        name="token_mixers",
    )(x3, w_mix, *tables, *sgu_params, *pool_params, *mla_params, *ret_tables, *[c.src for c in casts])


def _attn_kernel(q_ref, k_ref, vt_ref, o_ref, st_ref, m_ref, acc_ref):
    qi = pl.program_id(1)
    key_chunk = lax.broadcasted_iota(jnp.int32, (ATT_TK, ATT_TQ), 0) // CHUNK
    qry_chunk = lax.broadcasted_iota(jnp.int32, (ATT_TK, ATT_TQ), 1) // CHUNK
    diag_masks = [key_chunk + t * (ATT_TK // CHUNK) <= qry_chunk for t in range(ATT_TQ // ATT_TK)]
    heads = range(MLA_HEADS)

    def scores(j, slot):
        rows = pl.ds(pl.multiple_of(j * ATT_TK, ATT_TK), ATT_TK)
        for h in heads:
            cols = slice(h * HEAD_PAD, (h + 1) * HEAD_PAD)
            st_ref[slot, h] = _dot_nt(k_ref[rows, cols], q_ref[:, cols])

    ones = jnp.ones((ATT_ONES, ATT_TK), BF16)

    def softmax_pv(j, slot, mask=None):
        for h in heads:
            load = lambda: (st_ref[slot, h] if mask is None
                            else jnp.where(mask, st_ref[slot, h], -jnp.inf))
            m = m_ref[h]
            m_new = jnp.maximum(m, jnp.max(load(), axis=0, keepdims=True))
            p = jnp.exp2((load() - m_new).astype(BF16))
            lhs = jnp.concatenate([vt_ref[j, h * MLA_V:(h + 1) * MLA_V, :], ones], axis=0)
            acc_ref[h] = jnp.exp2(m - m_new) * acc_ref[h] + _dot(lhs, p)
            m_ref[h] = m_new

    def pair(i, carry):
        scores(2 * i + 1, 1)
        softmax_pv(2 * i, 0)
        scores(2 * i + 2, 0)
        softmax_pv(2 * i + 1, 1)
        return carry

    m_ref[...] = jnp.full(m_ref.shape, -jnp.inf, F32)
    acc_ref[...] = jnp.zeros(acc_ref.shape, F32)
    scores(0, 0)
    lax.fori_loop(0, qi, pair, 0)
    scores(2 * qi + 1, 1)
    softmax_pv(2 * qi, 0, diag_masks[0])
    softmax_pv(2 * qi + 1, 1, diag_masks[1])
    out_t = jnp.concatenate([acc_ref[h, :MLA_V] / acc_ref[h, MLA_V:MLA_V + 1] for h in heads], axis=0)
    o_ref[...] = out_t.T.astype(o_ref.dtype)


def _attention(q3, k3, vt4):
    bsz, seq, _ = q3.shape
    return pl.pallas_call(
        _attn_kernel,
        grid=(bsz, seq // ATT_TQ),
        in_specs=[pl.BlockSpec((None, ATT_TQ, MLA_HEADS * HEAD_PAD), lambda b, i: (b, i, 0)),
                  pl.BlockSpec((None, seq, MLA_HEADS * HEAD_PAD), lambda b, i: (b, 0, 0)),
                  pl.BlockSpec((None, seq // ATT_TK, MLA_HEADS * MLA_V, ATT_TK), lambda b, i: (b, 0, 0, 0))],
        out_specs=pl.BlockSpec((None, ATT_TQ, MLA_HEADS * MLA_V), lambda b, i: (b, i, 0)),
        out_shape=jax.ShapeDtypeStruct((bsz, seq, MLA_HEADS * MLA_V), BF16),
        scratch_shapes=[pltpu.VMEM((2, MLA_HEADS, ATT_TK, ATT_TQ), F32),
                        pltpu.VMEM((MLA_HEADS, 1, ATT_TQ), F32),
                        pltpu.VMEM((MLA_HEADS, MLA_V + ATT_ONES, ATT_TQ), F32)],
        compiler_params=_params(2),
        name="mla_attention",
    )(q3, k3, vt4)


def _merge_kernel(x_ref, ya_ref, yb_ref, yc_ref, yd_ref, wg_ref, wb_ref, wo_ref, g_ref, b_ref, o_ref):
    def residual(rows):
        x = x_ref[rows, :]
        xb = x.astype(BF16)
        merged = None
        for n, y_ref in enumerate((ya_ref, yb_ref, yc_ref, yd_ref)):
            gate = jax.nn.sigmoid(_dot(xb, wg_ref[:, n * D_MODEL:(n + 1) * D_MODEL]))
            term = gate * _dot(y_ref[rows, :], wb_ref[n * BRANCH_WIDTH:(n + 1) * BRANCH_WIDTH, :])
            merged = term if merged is None else merged + term
        return ALPHA * x + _dot(merged.astype(BF16), wo_ref[...])

    _pipelined_rows(o_ref, g_ref, b_ref, residual)


def _merge(layer, x, ys, wg, wb, wo, g, b, casts=()):
    n = x.shape[0]
    steps = n // TM_WIDE
    row = lambda last: pl.BlockSpec((TM_WIDE, last), lambda i: (i, 0))
    cast_in, cast_out, cast_shapes = _cast_specs(casts, steps, lambda i: i)
    return pl.pallas_call(
        _with_casts(_merge_kernel, 10, 1, casts),
        grid=(steps,),
        in_specs=[row(D_MODEL)] + [row(BRANCH_WIDTH)] * N_BRANCH
                 + [_const_spec(a.shape) for a in (wg, wb, wo)]
                 + [_layer_spec(a, layer) for a in (g, b)] + cast_in,
        out_specs=[row(D_MODEL)] + cast_out,
        out_shape=[jax.ShapeDtypeStruct((n, D_MODEL), F32)] + cast_shapes,
        compiler_params=_params(1),
        name="merge_ln",
    )(x, *ys, wg, wb, wo, g, b, *[c.src for c in casts])


def _rope_tables(positions):
    pos = positions.astype(F32)[..., None]
    lane = np.arange(HEAD_PAD)
    inv_r = ROPE_BASE ** (-jnp.arange(0, RET_KDIM, 2, dtype=F32) / RET_KDIM)
    ang_r = pos * jnp.tile(inv_r, HEAD_PAD // inv_r.shape[0])
    sign_r = jnp.asarray(np.where(lane % RET_KDIM < RET_KDIM // 2, -1.0, 1.0), F32)
    inv_m = ROPE_BASE ** (-jnp.arange(0, MLA_ROPE, 2, dtype=F32) / MLA_ROPE)
    ang_m = pos * jnp.tile(inv_m, HEAD_PAD // inv_m.shape[0])
    live = jnp.asarray(lane < MLA_ROPE)
    sign_m = jnp.asarray(np.where(lane < MLA_ROPE // 2, -1.0, 1.0), F32)
    return (jnp.cos(ang_r), jnp.sin(ang_r) * sign_r,
            jnp.where(live, jnp.cos(ang_m), 0.0), jnp.where(live, jnp.sin(ang_m) * sign_m, 0.0))


def _swap_cols(w):
    half = w.shape[-1] // 2
    return jnp.concatenate([w[..., half:], w[..., :half]], axis=-1)


def _mla_weights(w_uq, w_ukv):
    depth = w_uq.shape[0]
    uq = w_uq.reshape(depth, MLA_Q_RANK, MLA_HEADS, MLA_QK)
    q_nope, q_rope = uq[..., :MLA_NOPE], uq[..., MLA_NOPE:]
    zq = jnp.zeros((depth, MLA_Q_RANK, MLA_HEADS, HEAD_PAD - MLA_QK), w_uq.dtype)
    wq1 = jnp.concatenate([q_nope, q_rope, zq], axis=-1)
    wq2 = jnp.concatenate([jnp.zeros_like(q_nope), _swap_cols(q_rope), zq], axis=-1)
    ukv = w_ukv.reshape(depth, MLA_KV_RANK, MLA_HEADS, MLA_NOPE + MLA_V)
    k_nope, v = ukv[..., :MLA_NOPE], ukv[..., MLA_NOPE:]
    wk = jnp.concatenate([k_nope, jnp.zeros_like(k_nope)], axis=-1)
    wvt = jnp.swapaxes(v.reshape(depth, MLA_KV_RANK, MLA_HEADS * MLA_V), 1, 2)
    flat = lambda a: a.reshape(depth, a.shape[1], MLA_HEADS * HEAD_PAD).astype(BF16)
    return flat(wq1), flat(wq2), flat(wk), wvt.astype(BF16)


def kernel(x, p, positions, ffn1_up, ffn1_down, ln1_g, ln1_b, w_in, sgu_ln_g, sgu_ln_b, sgu_w, sgu_b,
           pool_w, pool_scale, mla_q_norm, mla_kv_norm, mla_w_uq, mla_w_ukv, w_branch, w_out,
           ln2_g, ln2_b, ffn2_up, ffn2_down, w_ple_gate, w_ple, ln3_g, ln3_b):
    bsz, seq, dm = x.shape
    n = bsz * seq
    tables = _rope_tables(positions)
    bf = lambda a: a.astype(BF16)
    rows = lambda a: a[:, None, :]

    up1, down1 = bf(ffn1_up[0]), bf(ffn1_down[0])
    w_branch2 = w_branch.reshape(DEPTH, N_BRANCH * BRANCH_WIDTH, dm)
    wp = bf(w_ple)
    wq1, wq2, wk, wvt = _mla_weights(mla_w_uq, mla_w_ukv)
    sgu_bias = jnp.repeat(jnp.swapaxes(sgu_b, 1, 2), SGU_WIDTH // SGU_GROUPS, axis=2)
    groups = len(POOL_WINDOWS)
    pool_bd = bf(jnp.einsum("lgcd,gh->lgchd", pool_w, jnp.eye(groups, dtype=pool_w.dtype))
                 .reshape(DEPTH, BRANCH_WIDTH, BRANCH_WIDTH))
    p2 = p.reshape(DEPTH, n, PLE_DIM)

    h = x.reshape(n, dm)
    for i in range(DEPTH):
        last = i + 1 == DEPTH
        h, w_mix, w_gate, wb, wo = _ffn_ln(
            i, h, up1, down1, rows(ln1_g), rows(ln1_b),
            (_cast(w_in, i, 0, MIX_COLS), _cast(w_in, i, _OFF_GATE), _cast(w_branch2, i), _cast(w_out, i)))
        y_a, y_b, y_c, q, k, vt, up2, wpg, *nxt_up = _mixers(
            i, h.reshape(bsz, seq, dm), w_mix, tables,
            (rows(sgu_ln_g), rows(sgu_ln_b), sgu_w, sgu_bias), (pool_bd, rows(pool_scale)),
            (rows(mla_q_norm), rows(mla_kv_norm), wq1, wq2, wk, wvt),
            (_cast(ffn2_up, i), _cast(w_ple_gate, i)) + (() if last else (_cast(ffn1_up, i + 1),)))
        y_d = _attention(q, k, vt)
        ys = tuple(y.reshape(n, BRANCH_WIDTH) for y in (y_a, y_b, y_c, y_d))
        h, down2, *nxt_down = _merge(
            i, h, ys, w_gate, wb, wo, rows(ln2_g), rows(ln2_b),
            (_cast(ffn2_down, i),) + (() if last else (_cast(ffn1_down, i + 1),)))
        h = _ffn_ple_ln(i, h, p2, up2, down2, wpg, wp, rows(ln3_g), rows(ln3_b))
        if not last:
            (up1,), (down1,) = nxt_up, nxt_down
    return h.reshape(bsz, seq, dm)
```

```python
import math
from typing import NamedTuple

import numpy as np
import jax
import jax.numpy as jnp
from jax import lax
from jax.experimental import pallas as pl
from jax.experimental.pallas import tpu as pltpu

D_MODEL = 1024
DEPTH = 2
CHUNK = 64
SGU_WIDTH = 256
SGU_BLOCK = 128
SGU_GROUPS = 4
RET_HEADS = 4
RET_KDIM = 64
POOL_WINDOWS = (2, 4, 8, 16)
POOL_HALO = 16
MLA_HEADS = 4
MLA_NOPE = 64
MLA_ROPE = 32
MLA_QK = MLA_NOPE + MLA_ROPE
MLA_V = 64
MLA_Q_RANK = 256
MLA_KV_RANK = 128
HEAD_PAD = 128
ROPE_BASE = 10000.0
N_BRANCH = 4
BRANCH_WIDTH = 256
D_FF = 2816
PLE_DIM = 256
ALPHA = (2 * DEPTH) ** 0.25
LN_EPS = 1e-5
RMS_EPS = 1e-6
GN_EPS = 1e-5

_OFF_SGU = 0
_OFF_RET = 512
_OFF_POOL = 1536
_OFF_CQ = 1792
_OFF_GATE = 2208
MIX_COLS = 2304

TM = 512
TM_WIDE = 1024
SUB = 256
RET_TILE = 256
ATT_TQ = 512
ATT_TK = 256
ATT_ONES = 16
VMEM_LIMIT = 56 * 1024 * 1024

BF16 = jnp.bfloat16
F32 = jnp.float32


def _dot(a, b):
    return jnp.dot(a, b, preferred_element_type=F32)


def _dot_nt(a, b):
    return lax.dot_general(a, b, (((1,), (1,)), ((), ())), preferred_element_type=F32)


def _dot_tn(a, b):
    return lax.dot_general(a, b, (((0,), (0,)), ((), ())), preferred_element_type=F32)


def _layer_norm(r, g, b, eps):
    mu = jnp.mean(r, axis=-1, keepdims=True)
    d = r - mu
    var = jnp.mean(d * d, axis=-1, keepdims=True)
    return d * lax.rsqrt(var + eps) * g + b


def _const_spec(shape):
    zeros = (0,) * len(shape)
    return pl.BlockSpec(shape, lambda *_: zeros, pipeline_mode=pl.Buffered(1))


def _layer_spec(arr, layer):
    index = (layer,) + (0,) * (arr.ndim - 1)
    return pl.BlockSpec((None,) + arr.shape[1:], lambda *_: index, pipeline_mode=pl.Buffered(1))


def _params(n_grid, semantics="parallel"):
    return pltpu.CompilerParams(
        dimension_semantics=(semantics,) * n_grid, vmem_limit_bytes=VMEM_LIMIT)


class _Cast(NamedTuple):
    src: jax.Array
    layer: int
    rows: int


class _CastT(NamedTuple):
    src: jax.Array
    layer: int
    row0: int
    rows: int


def _cast(src, layer, rows=None):
    return _Cast(src, layer, src.shape[1] if rows is None else rows)


def _cast_specs(casts, n_steps, step_of):
    in_specs, out_specs, out_shapes = [], [], []
    for c in casts:
        blk, cols = c.rows // n_steps, c.src.shape[2]
        if isinstance(c, _Cast):
            in_specs.append(pl.BlockSpec((None, blk, cols), lambda *g, c=c: (c.layer, step_of(*g), 0)))
            out_specs.append(pl.BlockSpec((blk, cols), lambda *g: (step_of(*g), 0)))
            out_shapes.append(jax.ShapeDtypeStruct((c.rows, cols), BF16))
        else:
            first = c.row0 // blk
            in_specs += [pl.BlockSpec((None, blk, cols),
                                      lambda *g, c=c, k=k: (c.layer, first + step_of(*g) + k, 0))
                         for k in range(2)]
            out_specs.append(pl.BlockSpec((cols, blk), lambda *g: (0, step_of(*g))))
            out_shapes.append(jax.ShapeDtypeStruct((cols, c.rows), BF16))
    return in_specs, out_specs, out_shapes


def _cast_operands(casts):
    return [c.src for c in casts for _ in range(1 if isinstance(c, _Cast) else 2)]


def _with_casts(body, n_in, n_out, casts, n_steps):
    n_src = sum(1 if isinstance(c, _Cast) else 2 for c in casts)

    def kernel(*refs):
        ins, rest = refs[:n_in], refs[n_in:]
        cast_in, rest = list(rest[:n_src]), rest[n_src:]
        outs, rest = rest[:n_out], rest[n_out:]
        cast_out, scratch = rest[:len(casts)], rest[len(casts):]
        body(*ins, *outs, *scratch)
        for c, dst_ref in zip(casts, cast_out):
            if isinstance(c, _Cast):
                dst_ref[...] = cast_in.pop(0)[...].astype(BF16)
            else:
                lo_ref, hi_ref = cast_in.pop(0), cast_in.pop(0)
                off = c.row0 % (c.rows // n_steps)
                window = jnp.concatenate([lo_ref[off:, :], hi_ref[:off, :]], axis=0)
                dst_ref[...] = window.T.astype(BF16)

    return kernel


def _pipelined_rows(o_ref, g_ref, b_ref, residual_fn):
    pending = None
    for s in range(o_ref.shape[0] // SUB):
        rows = slice(s * SUB, (s + 1) * SUB)
        r = residual_fn(rows)
        if pending is not None:
            o_ref[pending[0], :] = _layer_norm(pending[1], g_ref[...], b_ref[...], LN_EPS)
        pending = (rows, r)
    o_ref[pending[0], :] = _layer_norm(pending[1], g_ref[...], b_ref[...], LN_EPS)


def _ffn_body(x, up_ref, down_ref):
    xb = x.astype(BF16)
    a = _dot(xb, up_ref[:, :D_FF])
    b = _dot(xb, up_ref[:, D_FF:])
    h = (a * jax.nn.sigmoid(a) * b).astype(BF16)
    return xb, _dot(h, down_ref[...])


def _ffn_ln_kernel(x_ref, up_ref, down_ref, g_ref, b_ref, o_ref):
    def residual(rows):
        x = x_ref[rows, :]
        _, y = _ffn_body(x, up_ref, down_ref)
        return ALPHA * x + 0.5 * y

    _pipelined_rows(o_ref, g_ref, b_ref, residual)


def _ffn_ple_ln_kernel(x_ref, p_ref, up_ref, down_ref, wpg_ref, wp_ref, g_ref, b_ref, o_ref):
    def residual(rows):
        x = x_ref[rows, :]
        xb, y = _ffn_body(x, up_ref, down_ref)
        ple = jax.nn.sigmoid(_dot(xb, wpg_ref[...])) * _dot(p_ref[rows, :].astype(BF16), wp_ref[...])
        return ALPHA * x + 0.5 * y + ple

    _pipelined_rows(o_ref, g_ref, b_ref, residual)


def _ffn_ln(layer, x, up, down, g, b, casts=()):
    n = x.shape[0]
    steps = n // TM_WIDE
    row = pl.BlockSpec((TM_WIDE, D_MODEL), lambda i: (i, 0))
    cast_in, cast_out, cast_shapes = _cast_specs(casts, steps, lambda i: i)
    return pl.pallas_call(
        _with_casts(_ffn_ln_kernel, 5, 1, casts, steps),
        grid=(steps,),
        in_specs=[row, _const_spec(up.shape), _const_spec(down.shape), _layer_spec(g, layer),
                  _layer_spec(b, layer)] + cast_in,
        out_specs=[row] + cast_out,
        out_shape=[jax.ShapeDtypeStruct((n, D_MODEL), F32)] + cast_shapes,
        compiler_params=_params(1),
        name="ffn_ln",
    )(x, up, down, g, b, *_cast_operands(casts))


def _ffn_ple_ln(layer, x, p, up, down, wpg, wp, g, b):
    n = x.shape[0]
    row = pl.BlockSpec((TM_WIDE, D_MODEL), lambda i: (i, 0))
    return pl.pallas_call(
        _ffn_ple_ln_kernel,
        grid=(n // TM_WIDE,),
        in_specs=[row, pl.BlockSpec((None, TM_WIDE, PLE_DIM), lambda i: (layer, i, 0)),
                  _const_spec(up.shape), _const_spec(down.shape), _const_spec(wpg.shape)]
                 + [_layer_spec(a, layer) for a in (wp, g, b)],
        out_specs=row,
        out_shape=jax.ShapeDtypeStruct((n, D_MODEL), F32),
        compiler_params=_params(1),
        name="ffn_ple_ln",
    )(x, p, up, down, wpg, wp, g, b)


def _swap_halves(x, half):
    width = x.shape[-1]
    lane = lax.broadcasted_iota(jnp.int32, x.shape, x.ndim - 1)
    first = (lane % (2 * half)) < half
    return jnp.where(first, pltpu.roll(x, width - half, x.ndim - 1), pltpu.roll(x, half, x.ndim - 1))


def _sgu_part(uv, lng_ref, lnb_ref, ws_ref, bias_ref, o_ref):
    u = jax.nn.gelu(uv[:, :SGU_WIDTH])
    v = _layer_norm(jax.nn.gelu(uv[:, SGU_WIDTH:]), lng_ref[...], lnb_ref[...], LN_EPS)
    vb = v.astype(BF16)
    t_row = lax.broadcasted_iota(jnp.int32, (SGU_BLOCK, SGU_BLOCK), 0)
    t_col = lax.broadcasted_iota(jnp.int32, (SGU_BLOCK, SGU_BLOCK), 1)
    causal = t_row >= t_col
    w_groups = [jnp.where(causal, ws_ref[g], 0.0).astype(BF16) for g in range(SGU_GROUPS)]
    group = lax.broadcasted_iota(jnp.int32, (SGU_BLOCK, SGU_WIDTH), 1) // (SGU_WIDTH // SGU_GROUPS)
    bias = bias_ref[...]
    for blk in range(TM // SGU_BLOCK):
        rows = slice(blk * SGU_BLOCK, (blk + 1) * SGU_BLOCK)
        v_blk = vb[rows]
        mixed = bias
        for g in range(SGU_GROUPS):
            mixed = mixed + jnp.where(group == g, _dot(w_groups[g], v_blk), 0.0)
        o_ref[rows, :] = (u[rows] * mixed).astype(o_ref.dtype)


def _ret_part(proj, cos, sin, dmask_ref, rowdec_ref, keydec_ref, tiledec_ref, o_ref, state_ref):
    width = RET_HEADS * RET_KDIM
    q = proj[:, :width]
    k = proj[:, width:2 * width]
    q = q * cos + _swap_halves(q, RET_KDIM // 2) * sin
    k = (k * cos + _swap_halves(k, RET_KDIM // 2) * sin) * RET_KDIM ** -0.5
    v = proj[:, 2 * width:3 * width]
    gate = proj[:, 3 * width:]
    vb = v.astype(BF16)
    kb = k.astype(BF16)

    head = lax.broadcasted_iota(jnp.int32, (RET_TILE, width), 1) // RET_KDIM
    y = _dot(q.astype(BF16), state_ref[...].astype(BF16)) * rowdec_ref[...]
    for h in range(RET_HEADS):
        qh = jnp.where(head == h, q, 0.0).astype(BF16)
        scores = _dot_nt(qh, kb) * dmask_ref[h]
        y = y + jnp.where(head == h, _dot(scores.astype(BF16), vb), 0.0)

    kd = (k * keydec_ref[...]).astype(BF16)
    row_head = lax.broadcasted_iota(jnp.int32, (width, width), 0) // RET_KDIM
    col_head = lax.broadcasted_iota(jnp.int32, (width, width), 1) // RET_KDIM
    kv = jnp.where(row_head == col_head, _dot_tn(kd, vb), 0.0)
    state_ref[...] = state_ref[...] * tiledec_ref[...] + kv

    inv = 1.0 / RET_KDIM
    mu = jnp.zeros_like(y)
    for h in range(RET_HEADS):
        s = jnp.sum(jnp.where(head == h, y, 0.0), axis=1, keepdims=True) * inv
        mu = jnp.where(head == h, s, mu)
    d = y - mu
    var = jnp.zeros_like(y)
    for h in range(RET_HEADS):
        s = jnp.sum(jnp.where(head == h, d * d, 0.0), axis=1, keepdims=True) * inv
        var = jnp.where(head == h, s, var)
    yn = d * lax.rsqrt(var + GN_EPS)
    o_ref[...] = (gate * jax.nn.sigmoid(gate) * yn).astype(o_ref.dtype)


def _retention_tables():
    heads = np.arange(RET_HEADS, dtype=np.float64)
    log_gamma = np.log1p(-np.exp2(-5.0 - heads))
    t = np.arange(RET_TILE)
    chunk = t // CHUNK
    diff = (t[:, None] - t[None, :]).astype(np.float64)
    same = chunk[:, None] == chunk[None, :]
    earlier = chunk[None, :] < chunk[:, None]
    expo = np.where(same, np.abs(diff), diff)
    dmask = np.where((same | earlier)[None], np.exp(log_gamma[:, None, None] * expo[None]), 0.0)
    rowdec = np.exp(log_gamma[None, :] * (t[:, None] + 1.0))
    keydec = np.exp(log_gamma[None, :] * (RET_TILE - 1.0 - t[:, None]))
    tiledec = np.exp(log_gamma * RET_TILE)
    rep = lambda a: np.repeat(a, RET_KDIM, axis=-1)
    width = RET_HEADS * RET_KDIM
    tiledec_full = np.broadcast_to(rep(tiledec[None, :]).T, (width, width))
    return (jnp.asarray(dmask, F32), jnp.asarray(rep(rowdec), F32), jnp.asarray(rep(keydec), F32),
            jnp.asarray(tiledec_full, F32))


def _pool_part(z, wp_ref, scale_ref, o_ref, zext_ref, seq_tile):
    zext_ref[POOL_HALO:POOL_HALO + TM, :] = z
    group = lax.broadcasted_iota(jnp.int32, (TM, BRANCH_WIDTH), 1) // (BRANCH_WIDTH // len(POOL_WINDOWS))
    acc = z
    win = jnp.zeros_like(z)
    window = jnp.zeros((TM, BRANCH_WIDTH), jnp.int32)
    for d in range(1, POOL_HALO):
        acc = acc + zext_ref[POOL_HALO - d:POOL_HALO - d + TM, :]
        if d + 1 in POOL_WINDOWS:
            gi = POOL_WINDOWS.index(d + 1)
            win = jnp.where(group == gi, acc, win)
            window = jnp.where(group == gi, d + 1, window)
    t = seq_tile * TM + lax.broadcasted_iota(jnp.int32, (TM, BRANCH_WIDTH), 0)
    count = jnp.minimum(t + 1, window).astype(F32)
    pooled = win / count - z
    y = _dot(pooled.astype(BF16), wp_ref[...]) * scale_ref[...]
    o_ref[...] = y.astype(o_ref.dtype)
    zext_ref[0:POOL_HALO, :] = zext_ref[TM:TM + POOL_HALO, :]


def _rms_norm(x, g):
    return x * lax.rsqrt(jnp.mean(x * x, axis=-1, keepdims=True) + RMS_EPS) * g


def _mla_part(proj, cos, sin, qg_ref, kvg_ref, wq1_ref, wq2_ref, wk_ref, wvt_ref, q_ref, k_ref, vt_ref):
    cq = _rms_norm(proj[:, :MLA_Q_RANK], qg_ref[...]).astype(BF16)
    ckv = _rms_norm(proj[:, MLA_Q_RANK:MLA_Q_RANK + MLA_KV_RANK], kvg_ref[...]).astype(BF16)
    k_raw = proj[:, MLA_Q_RANK + MLA_KV_RANK:]
    k_pe = pltpu.roll(k_raw * cos + _swap_halves(k_raw, MLA_ROPE // 2) * sin, MLA_NOPE, 1)
    ck = pltpu.roll(cos, MLA_NOPE, 1)
    sk = pltpu.roll(sin, MLA_NOPE, 1)
    lane = lax.broadcasted_iota(jnp.int32, ck.shape, 1)
    scale = MLA_QK ** -0.5 * math.log2(math.e)
    cq_tab = (ck + jnp.where(lane < MLA_NOPE, 1.0, 0.0)) * scale
    sq_tab = sk * scale
    tile4 = lambda a: jnp.concatenate([a] * MLA_HEADS, axis=1)
    q = _dot(cq, wq1_ref[...]) * tile4(cq_tab) + _dot(cq, wq2_ref[...]) * tile4(sq_tab)
    k = _dot(ckv, wk_ref[...]) + tile4(k_pe)
    q_ref[...] = q.astype(q_ref.dtype)
    k_ref[...] = k.astype(k_ref.dtype)
    for t in range(TM // ATT_TK):
        vt_ref[t] = _dot_nt(wvt_ref[...], ckv[t * ATT_TK:(t + 1) * ATT_TK]).astype(vt_ref.dtype)


def _mixers_kernel(x_ref, w_ref, cs_ref, expand_ref,
                   lng_ref, lnb_ref, ws_ref, bias_ref, wp_ref, scale_ref,
                   qg_ref, kvg_ref, wq1_ref, wq2_ref, wk_ref, wvt_ref,
                   dmask_ref, rowdec_ref, keydec_ref, tiledec_ref,
                   ya_ref, yb_ref, yc_ref, q_ref, k_ref, vt_ref, state_ref, zext_ref):
    seq_tile = pl.program_id(1)

    @pl.when(seq_tile == 0)
    def _():
        state_ref[...] = jnp.zeros_like(state_ref)
        zext_ref[0:POOL_HALO, :] = jnp.zeros((POOL_HALO, BRANCH_WIDTH), F32)

    xb = x_ref[...].astype(BF16)
    p_sgu = _dot_nt(xb, w_ref[_OFF_SGU:_OFF_RET, :])
    p_pool = _dot_nt(xb, w_ref[_OFF_POOL:_OFF_CQ, :])
    p_mla = _dot_nt(xb, w_ref[_OFF_CQ:MIX_COLS, :])
    p_ret = _dot_nt(xb, w_ref[_OFF_RET:_OFF_POOL, :])
    cs = cs_ref[...]
    hi = cs.astype(BF16)
    rest = cs - hi.astype(F32)
    mid = rest.astype(BF16)
    lo = (rest - mid.astype(F32)).astype(BF16)
    rope = _dot(jnp.concatenate([hi, mid, lo], axis=1), expand_ref[...])
    ret_cos, ret_sin, mla_cos, mla_sin = (rope[:, t * HEAD_PAD:(t + 1) * HEAD_PAD] for t in range(4))
    _sgu_part(p_sgu, lng_ref, lnb_ref, ws_ref, bias_ref, ya_ref)
    _pool_part(p_pool, wp_ref, scale_ref, yc_ref, zext_ref, seq_tile)
    _mla_part(p_mla, mla_cos, mla_sin, qg_ref, kvg_ref, wq1_ref, wq2_ref, wk_ref, wvt_ref,
              q_ref, k_ref, vt_ref)
    for r in range(TM // RET_TILE):
        rows = slice(r * RET_TILE, (r + 1) * RET_TILE)
        cos = jnp.concatenate([ret_cos[rows], ret_cos[rows]], axis=1)
        sin = jnp.concatenate([ret_sin[rows], ret_sin[rows]], axis=1)
        _ret_part(p_ret[rows], cos, sin, dmask_ref, rowdec_ref, keydec_ref, tiledec_ref,
                  yb_ref.at[rows, :], state_ref)


def _mixers(layer, x3, w_mix, rope, sgu_params, pool_params, mla_params, casts=()):
    bsz, seq, _ = x3.shape
    width = RET_HEADS * RET_KDIM
    ret_tables = _retention_tables()
    seq_tiles = seq // TM
    tile = lambda last: pl.BlockSpec((None, TM, last), lambda b, s: (b, s, 0))
    layered = tuple(sgu_params) + tuple(pool_params) + tuple(mla_params)
    in_specs = ([tile(D_MODEL), _const_spec(w_mix.shape), tile(HEAD_PAD), _const_spec(rope[1].shape)]
                + [_layer_spec(a, layer) for a in layered]
                + [_const_spec(a.shape) for a in ret_tables])
    cast_in, cast_out, cast_shapes = _cast_specs(casts, bsz * seq_tiles, lambda b, s: b * seq_tiles + s)
    branch = jax.ShapeDtypeStruct((bsz, seq, BRANCH_WIDTH), BF16)
    qk = jax.ShapeDtypeStruct((bsz, seq, MLA_HEADS * HEAD_PAD), BF16)
    vt = jax.ShapeDtypeStruct((bsz, seq // ATT_TK, MLA_HEADS * MLA_V, ATT_TK), BF16)
    return pl.pallas_call(
        _with_casts(_mixers_kernel, len(in_specs), 6, casts, bsz * seq_tiles),
        grid=(bsz, seq_tiles),
        in_specs=in_specs + cast_in,
        out_specs=[tile(BRANCH_WIDTH)] * 3 + [tile(MLA_HEADS * HEAD_PAD)] * 2
                  + [pl.BlockSpec((None, TM // ATT_TK, MLA_HEADS * MLA_V, ATT_TK), lambda b, s: (b, s, 0, 0))]
                  + cast_out,
        out_shape=[branch, branch, branch, qk, qk, vt] + cast_shapes,
        scratch_shapes=[pltpu.VMEM((width, width), F32),
                        pltpu.VMEM((TM + POOL_HALO, BRANCH_WIDTH), F32)],
        compiler_params=_params(2, "arbitrary"),
        name="token_mixers",
    )(x3, w_mix, *rope, *sgu_params, *pool_params, *mla_params, *ret_tables, *_cast_operands(casts))


def _attn_kernel(q_ref, k_ref, vt_ref, o_ref, st_ref, m_ref, acc_ref):
    qi = pl.program_id(1)
    key_chunk = lax.broadcasted_iota(jnp.int32, (ATT_TK, ATT_TQ), 0) // CHUNK
    qry_chunk = lax.broadcasted_iota(jnp.int32, (ATT_TK, ATT_TQ), 1) // CHUNK
    diag_masks = [key_chunk + t * (ATT_TK // CHUNK) <= qry_chunk for t in range(ATT_TQ // ATT_TK)]
    heads = range(MLA_HEADS)

    def scores(j, slot):
        rows = pl.ds(pl.multiple_of(j * ATT_TK, ATT_TK), ATT_TK)
        for h in heads:
            cols = slice(h * HEAD_PAD, (h + 1) * HEAD_PAD)
            st_ref[slot, h] = _dot_nt(k_ref[rows, cols], q_ref[:, cols])

    ones = jnp.ones((ATT_ONES, ATT_TK), BF16)

    def softmax_pv(j, slot, mask=None):
        for h in heads:
            load = lambda: (st_ref[slot, h] if mask is None
                            else jnp.where(mask, st_ref[slot, h], -jnp.inf))
            m = m_ref[h]
            m_new = jnp.maximum(m, jnp.max(load(), axis=0, keepdims=True))
            p = jnp.exp2((load() - m_new).astype(BF16))
            lhs = jnp.concatenate([vt_ref[j, h * MLA_V:(h + 1) * MLA_V, :], ones], axis=0)
            acc_ref[h] = jnp.exp2(m - m_new) * acc_ref[h] + _dot(lhs, p)
            m_ref[h] = m_new

    def pair(i, carry):
        scores(2 * i + 1, 1)
        softmax_pv(2 * i, 0)
        scores(2 * i + 2, 0)
        softmax_pv(2 * i + 1, 1)
        return carry

    m_ref[...] = jnp.full(m_ref.shape, -jnp.inf, F32)
    acc_ref[...] = jnp.zeros(acc_ref.shape, F32)
    scores(0, 0)
    lax.fori_loop(0, qi, pair, 0)
    scores(2 * qi + 1, 1)
    softmax_pv(2 * qi, 0, diag_masks[0])
    softmax_pv(2 * qi + 1, 1, diag_masks[1])
    out_t = jnp.concatenate([acc_ref[h, :MLA_V] / acc_ref[h, MLA_V:MLA_V + 1] for h in heads], axis=0)
    o_ref[...] = out_t.T.astype(o_ref.dtype)


def _attention(q3, k3, vt4):
    bsz, seq, _ = q3.shape
    return pl.pallas_call(
        _attn_kernel,
        grid=(bsz, seq // ATT_TQ),
        in_specs=[pl.BlockSpec((None, ATT_TQ, MLA_HEADS * HEAD_PAD), lambda b, i: (b, i, 0)),
                  pl.BlockSpec((None, seq, MLA_HEADS * HEAD_PAD), lambda b, i: (b, 0, 0)),
                  pl.BlockSpec((None, seq // ATT_TK, MLA_HEADS * MLA_V, ATT_TK), lambda b, i: (b, 0, 0, 0))],
        out_specs=pl.BlockSpec((None, ATT_TQ, MLA_HEADS * MLA_V), lambda b, i: (b, i, 0)),
        out_shape=jax.ShapeDtypeStruct((bsz, seq, MLA_HEADS * MLA_V), BF16),
        scratch_shapes=[pltpu.VMEM((2, MLA_HEADS, ATT_TK, ATT_TQ), F32),
                        pltpu.VMEM((MLA_HEADS, 1, ATT_TQ), F32),
                        pltpu.VMEM((MLA_HEADS, MLA_V + ATT_ONES, ATT_TQ), F32)],
        compiler_params=_params(2),
        name="mla_attention",
    )(q3, k3, vt4)


def _merge_kernel(x_ref, ya_ref, yb_ref, yc_ref, yd_ref, wg_ref, wb_ref, wo_ref, g_ref, b_ref, o_ref):
    def residual(rows):
        x = x_ref[rows, :]
        xb = x.astype(BF16)
        merged = None
        for n, y_ref in enumerate((ya_ref, yb_ref, yc_ref, yd_ref)):
            gate = jax.nn.sigmoid(_dot(xb, wg_ref[:, n * D_MODEL:(n + 1) * D_MODEL]))
            term = gate * _dot(y_ref[rows, :], wb_ref[n * BRANCH_WIDTH:(n + 1) * BRANCH_WIDTH, :])
            merged = term if merged is None else merged + term
        return ALPHA * x + _dot(merged.astype(BF16), wo_ref[...])

    _pipelined_rows(o_ref, g_ref, b_ref, residual)


def _merge(layer, x, ys, wg, wb, wo, g, b, casts=()):
    n = x.shape[0]
    steps = n // TM_WIDE
    row = lambda last: pl.BlockSpec((TM_WIDE, last), lambda i: (i, 0))
    cast_in, cast_out, cast_shapes = _cast_specs(casts, steps, lambda i: i)
    return pl.pallas_call(
        _with_casts(_merge_kernel, 10, 1, casts, steps),
        grid=(steps,),
        in_specs=[row(D_MODEL)] + [row(BRANCH_WIDTH)] * N_BRANCH
                 + [_const_spec(a.shape) for a in (wg, wb, wo)]
                 + [_layer_spec(a, layer) for a in (g, b)] + cast_in,
        out_specs=[row(D_MODEL)] + cast_out,
        out_shape=[jax.ShapeDtypeStruct((n, D_MODEL), F32)] + cast_shapes,
        compiler_params=_params(1),
        name="merge_ln",
    )(x, *ys, wg, wb, wo, g, b, *_cast_operands(casts))


def _rope_tables(positions):
    n_r, n_m = RET_KDIM // 2, MLA_ROPE // 2
    inv_r = ROPE_BASE ** (-jnp.arange(0, RET_KDIM, 2, dtype=F32) / RET_KDIM)
    inv_m = ROPE_BASE ** (-jnp.arange(0, MLA_ROPE, 2, dtype=F32) / MLA_ROPE)
    pad = jnp.zeros((HEAD_PAD - 2 * (n_r + n_m),), F32)
    ang = positions.astype(F32)[..., None] * jnp.concatenate([inv_r, inv_r, inv_m, inv_m, pad])
    lane = np.arange(HEAD_PAD)
    is_cos = jnp.asarray((lane < n_r) | ((lane >= 2 * n_r) & (lane < 2 * n_r + n_m)))
    compact = jnp.where(is_cos, jnp.cos(ang), jnp.sin(ang))
    expand = np.zeros((HEAD_PAD, 4 * HEAD_PAD), np.float32)
    for l in range(HEAD_PAD):
        expand[l % n_r, l] = 1.0
        expand[n_r + l % n_r, HEAD_PAD + l] = -1.0 if l % RET_KDIM < n_r else 1.0
        if l < MLA_ROPE:
            expand[2 * n_r + l % n_m, 2 * HEAD_PAD + l] = 1.0
            expand[2 * n_r + n_m + l % n_m, 3 * HEAD_PAD + l] = -1.0 if l < n_m else 1.0
    return compact, jnp.asarray(np.tile(expand, (3, 1)), BF16)


def _swap_cols(w):
    half = w.shape[-1] // 2
    return jnp.concatenate([w[..., half:], w[..., :half]], axis=-1)


def _mla_weights(w_uq, w_ukv):
    depth = w_uq.shape[0]
    uq = w_uq.reshape(depth, MLA_Q_RANK, MLA_HEADS, MLA_QK)
    q_nope, q_rope = uq[..., :MLA_NOPE], uq[..., MLA_NOPE:]
    zq = jnp.zeros((depth, MLA_Q_RANK, MLA_HEADS, HEAD_PAD - MLA_QK), w_uq.dtype)
    wq1 = jnp.concatenate([q_nope, q_rope, zq], axis=-1)
    wq2 = jnp.concatenate([jnp.zeros_like(q_nope), _swap_cols(q_rope), zq], axis=-1)
    ukv = w_ukv.reshape(depth, MLA_KV_RANK, MLA_HEADS, MLA_NOPE + MLA_V)
    k_nope, v = ukv[..., :MLA_NOPE], ukv[..., MLA_NOPE:]
    wk = jnp.concatenate([k_nope, jnp.zeros_like(k_nope)], axis=-1)
    wvt = jnp.swapaxes(v.reshape(depth, MLA_KV_RANK, MLA_HEADS * MLA_V), 1, 2)
    flat = lambda a: a.reshape(depth, a.shape[1], MLA_HEADS * HEAD_PAD).astype(BF16)
    return flat(wq1), flat(wq2), flat(wk), wvt.astype(BF16)


def kernel(x, p, positions, ffn1_up, ffn1_down, ln1_g, ln1_b, w_in, sgu_ln_g, sgu_ln_b, sgu_w, sgu_b,
           pool_w, pool_scale, mla_q_norm, mla_kv_norm, mla_w_uq, mla_w_ukv, w_branch, w_out,
           ln2_g, ln2_b, ffn2_up, ffn2_down, w_ple_gate, w_ple, ln3_g, ln3_b):
    bsz, seq, dm = x.shape
    n = bsz * seq
    rope = _rope_tables(positions)
    bf = lambda a: a.astype(BF16)
    rows = lambda a: a[:, None, :]

    up1, down1 = bf(ffn1_up[0]), bf(ffn1_down[0])
    w_in_t = jnp.swapaxes(w_in, 1, 2)
    w_branch2 = w_branch.reshape(DEPTH, N_BRANCH * BRANCH_WIDTH, dm)
    wp = bf(w_ple)
    wq1, wq2, wk, wvt = _mla_weights(mla_w_uq, mla_w_ukv)
    sgu_bias = jnp.repeat(jnp.swapaxes(sgu_b, 1, 2), SGU_WIDTH // SGU_GROUPS, axis=2)
    groups = len(POOL_WINDOWS)
    pool_bd = bf(jnp.einsum("lgcd,gh->lgchd", pool_w, jnp.eye(groups, dtype=pool_w.dtype))
                 .reshape(DEPTH, BRANCH_WIDTH, BRANCH_WIDTH))
    p2 = p.reshape(DEPTH, n, PLE_DIM)

    h = x.reshape(n, dm)
    for i in range(DEPTH):
        last = i + 1 == DEPTH
        h, w_mix, w_gate, wb, wo = _ffn_ln(
            i, h, up1, down1, rows(ln1_g), rows(ln1_b),
            (_cast(w_in_t, i, MIX_COLS), _CastT(w_in_t, i, _OFF_GATE, N_BRANCH * dm), _cast(w_branch2, i),
             _cast(w_out, i)))
        y_a, y_b, y_c, q, k, vt, up2, wpg, *nxt_up = _mixers(
            i, h.reshape(bsz, seq, dm), w_mix, rope,
            (rows(sgu_ln_g), rows(sgu_ln_b), sgu_w, sgu_bias), (pool_bd, rows(pool_scale)),
            (rows(mla_q_norm), rows(mla_kv_norm), wq1, wq2, wk, wvt),
            (_cast(ffn2_up, i), _cast(w_ple_gate, i)) + (() if last else (_cast(ffn1_up, i + 1),)))
        y_d = _attention(q, k, vt)
        ys = tuple(y.reshape(n, BRANCH_WIDTH) for y in (y_a, y_b, y_c, y_d))
        h, down2, *nxt_down = _merge(
            i, h, ys, w_gate, wb, wo, rows(ln2_g), rows(ln2_b),
            (_cast(ffn2_down, i),) + (() if last else (_cast(ffn1_down, i + 1),)))
        h = _ffn_ple_ln(i, h, p2, up2, down2, wpg, wp, rows(ln3_g), rows(ln3_b))
        if not last:
            (up1,), (down1,) = nxt_up, nxt_down
    return h.reshape(bsz, seq, dm)
```

```python
import math
from typing import NamedTuple

import numpy as np
import jax
import jax.numpy as jnp
from jax import lax
from jax.experimental import pallas as pl
from jax.experimental.pallas import tpu as pltpu

D_MODEL = 1024
DEPTH = 2
CHUNK = 64
SGU_WIDTH = 256
SGU_BLOCK = 128
SGU_GROUPS = 4
RET_HEADS = 4
RET_KDIM = 64
POOL_WINDOWS = (2, 4, 8, 16)
POOL_HALO = 16
POOL_PAD = 32
MLA_HEADS = 4
MLA_NOPE = 64
MLA_ROPE = 32
MLA_QK = MLA_NOPE + MLA_ROPE
MLA_V = 64
MLA_Q_RANK = 256
MLA_KV_RANK = 128
HEAD_PAD = 128
ROPE_BASE = 10000.0
N_BRANCH = 4
BRANCH_WIDTH = 256
D_FF = 2816
PLE_DIM = 256
ALPHA = (2 * DEPTH) ** 0.25
LN_EPS = 1e-5
RMS_EPS = 1e-6
GN_EPS = 1e-5

_OFF_SGU = 0
_OFF_RET = 512
_OFF_POOL = 1536
_OFF_CQ = 1792
_OFF_GATE = 2208
MIX_COLS = 2304

TM = 512
TM_WIDE = 1024
SUB = 256
RET_TILE = 256
ATT_TQ = 512
ATT_TK = 256
ATT_ONES = 16
VMEM_LIMIT = 56 * 1024 * 1024

BF16 = jnp.bfloat16
F32 = jnp.float32


def _dot(a, b):
    return jnp.dot(a, b, preferred_element_type=F32)


def _dot_nt(a, b):
    return lax.dot_general(a, b, (((1,), (1,)), ((), ())), preferred_element_type=F32)


def _dot_tn(a, b):
    return lax.dot_general(a, b, (((0,), (0,)), ((), ())), preferred_element_type=F32)


def _layer_norm(r, g, b, eps):
    mu = jnp.mean(r, axis=-1, keepdims=True)
    d = r - mu
    var = jnp.mean(d * d, axis=-1, keepdims=True)
    return d * lax.rsqrt(var + eps) * g + b


def _const_spec(shape):
    zeros = (0,) * len(shape)
    return pl.BlockSpec(shape, lambda *_: zeros, pipeline_mode=pl.Buffered(1))


def _layer_spec(arr, layer):
    index = (layer,) + (0,) * (arr.ndim - 1)
    return pl.BlockSpec((None,) + arr.shape[1:], lambda *_: index, pipeline_mode=pl.Buffered(1))


def _params(n_grid, semantics="parallel"):
    return pltpu.CompilerParams(
        dimension_semantics=(semantics,) * n_grid, vmem_limit_bytes=VMEM_LIMIT)


class _Cast(NamedTuple):
    src: jax.Array
    layer: int
    rows: int


class _CastT(NamedTuple):
    src: jax.Array
    layer: int
    row0: int
    rows: int


def _cast(src, layer, rows=None):
    return _Cast(src, layer, src.shape[1] if rows is None else rows)


def _cast_specs(casts, n_steps, step_of):
    in_specs, out_specs, out_shapes = [], [], []
    for c in casts:
        blk, cols = c.rows // n_steps, c.src.shape[2]
        if isinstance(c, _Cast):
            in_specs.append(pl.BlockSpec((None, blk, cols), lambda *g, c=c: (c.layer, step_of(*g), 0)))
            out_specs.append(pl.BlockSpec((blk, cols), lambda *g: (step_of(*g), 0)))
            out_shapes.append(jax.ShapeDtypeStruct((c.rows, cols), BF16))
        else:
            first = c.row0 // blk
            in_specs += [pl.BlockSpec((None, blk, cols),
                                      lambda *g, c=c, k=k: (c.layer, first + step_of(*g) + k, 0))
                         for k in range(2)]
            out_specs.append(pl.BlockSpec((cols, blk), lambda *g: (0, step_of(*g))))
            out_shapes.append(jax.ShapeDtypeStruct((cols, c.rows), BF16))
    return in_specs, out_specs, out_shapes


def _cast_operands(casts):
    return [c.src for c in casts for _ in range(1 if isinstance(c, _Cast) else 2)]


def _with_casts(body, n_in, n_out, casts, n_steps):
    n_src = sum(1 if isinstance(c, _Cast) else 2 for c in casts)

    def kernel(*refs):
        ins, rest = refs[:n_in], refs[n_in:]
        cast_in, rest = list(rest[:n_src]), rest[n_src:]
        outs, rest = rest[:n_out], rest[n_out:]
        cast_out, scratch = rest[:len(casts)], rest[len(casts):]
        body(*ins, *outs, *scratch)
        for c, dst_ref in zip(casts, cast_out):
            if isinstance(c, _Cast):
                dst_ref[...] = cast_in.pop(0)[...].astype(BF16)
            else:
                lo_ref, hi_ref = cast_in.pop(0), cast_in.pop(0)
                off = c.row0 % (c.rows // n_steps)
                window = jnp.concatenate([lo_ref[off:, :], hi_ref[:off, :]], axis=0)
                dst_ref[...] = window.T.astype(BF16)

    return kernel


def _pipelined_rows(o_ref, g_ref, b_ref, residual_fn):
    pending = None
    for s in range(o_ref.shape[0] // SUB):
        rows = slice(s * SUB, (s + 1) * SUB)
        r = residual_fn(rows)
        if pending is not None:
            o_ref[pending[0], :] = _layer_norm(pending[1], g_ref[...], b_ref[...], LN_EPS)
        pending = (rows, r)
    o_ref[pending[0], :] = _layer_norm(pending[1], g_ref[...], b_ref[...], LN_EPS)


def _ffn_body(x, up_ref, down_ref):
    xb = x.astype(BF16)
    a = _dot(xb, up_ref[:, :D_FF])
    b = _dot(xb, up_ref[:, D_FF:])
    h = (a * jax.nn.sigmoid(a) * b).astype(BF16)
    return xb, _dot(h, down_ref[...])


def _ffn_ln_kernel(x_ref, up_ref, down_ref, g_ref, b_ref, o_ref):
    def residual(rows):
        x = x_ref[rows, :]
        _, y = _ffn_body(x, up_ref, down_ref)
        return ALPHA * x + 0.5 * y

    _pipelined_rows(o_ref, g_ref, b_ref, residual)


def _ffn_ple_ln_kernel(x_ref, p_ref, up_ref, down_ref, wpg_ref, wp_ref, g_ref, b_ref, o_ref):
    def residual(rows):
        x = x_ref[rows, :]
        xb, y = _ffn_body(x, up_ref, down_ref)
        ple = jax.nn.sigmoid(_dot(xb, wpg_ref[...])) * _dot(p_ref[rows, :].astype(BF16), wp_ref[...])
        return ALPHA * x + 0.5 * y + ple

    _pipelined_rows(o_ref, g_ref, b_ref, residual)


def _ffn_ln(layer, x, up, down, g, b, casts=()):
    n = x.shape[0]
    steps = n // TM_WIDE
    row = pl.BlockSpec((TM_WIDE, D_MODEL), lambda i: (i, 0))
    cast_in, cast_out, cast_shapes = _cast_specs(casts, steps, lambda i: i)
    return pl.pallas_call(
        _with_casts(_ffn_ln_kernel, 5, 1, casts, steps),
        grid=(steps,),
        in_specs=[row, _const_spec(up.shape), _const_spec(down.shape), _layer_spec(g, layer),
                  _layer_spec(b, layer)] + cast_in,
        out_specs=[row] + cast_out,
        out_shape=[jax.ShapeDtypeStruct((n, D_MODEL), F32)] + cast_shapes,
        compiler_params=_params(1),
        name="ffn_ln",
    )(x, up, down, g, b, *_cast_operands(casts))


def _ffn_ple_ln(layer, x, p, up, down, wpg, wp, g, b):
    n = x.shape[0]
    row = pl.BlockSpec((TM_WIDE, D_MODEL), lambda i: (i, 0))
    return pl.pallas_call(
        _ffn_ple_ln_kernel,
        grid=(n // TM_WIDE,),
        in_specs=[row, pl.BlockSpec((None, TM_WIDE, PLE_DIM), lambda i: (layer, i, 0)),
                  _const_spec(up.shape), _const_spec(down.shape), _const_spec(wpg.shape)]
                 + [_layer_spec(a, layer) for a in (wp, g, b)],
        out_specs=row,
        out_shape=jax.ShapeDtypeStruct((n, D_MODEL), F32),
        compiler_params=_params(1),
        name="ffn_ple_ln",
    )(x, p, up, down, wpg, wp, g, b)


def _swap_halves(x, half):
    width = x.shape[-1]
    lane = lax.broadcasted_iota(jnp.int32, x.shape, x.ndim - 1)
    first = (lane % (2 * half)) < half
    return jnp.where(first, pltpu.roll(x, width - half, x.ndim - 1), pltpu.roll(x, half, x.ndim - 1))


def _sgu_part(uv, lng_ref, lnb_ref, ws_ref, bias_ref, o_ref):
    u = jax.nn.gelu(uv[:, :SGU_WIDTH])
    v = _layer_norm(jax.nn.gelu(uv[:, SGU_WIDTH:]), lng_ref[...], lnb_ref[...], LN_EPS)
    vb = v.astype(BF16)
    t_row = lax.broadcasted_iota(jnp.int32, (SGU_BLOCK, SGU_BLOCK), 0)
    t_col = lax.broadcasted_iota(jnp.int32, (SGU_BLOCK, SGU_BLOCK), 1)
    causal = t_row >= t_col
    w_groups = [jnp.where(causal, ws_ref[g], 0.0).astype(BF16) for g in range(SGU_GROUPS)]
    group = lax.broadcasted_iota(jnp.int32, (SGU_BLOCK, SGU_WIDTH), 1) // (SGU_WIDTH // SGU_GROUPS)
    bias = bias_ref[...]
    for blk in range(TM // SGU_BLOCK):
        rows = slice(blk * SGU_BLOCK, (blk + 1) * SGU_BLOCK)
        v_blk = vb[rows]
        mixed = bias
        for g in range(SGU_GROUPS):
            mixed = mixed + jnp.where(group == g, _dot(w_groups[g], v_blk), 0.0)
        o_ref[rows, :] = (u[rows] * mixed).astype(o_ref.dtype)


def _ret_part(proj, cos, sin, dmask_ref, rowdec_ref, keydec_ref, tiledec_ref, o_ref, state_ref):
    width = RET_HEADS * RET_KDIM
    q = proj[:, :width]
    k = proj[:, width:2 * width]
    q = q * cos + _swap_halves(q, RET_KDIM // 2) * sin
    k = (k * cos + _swap_halves(k, RET_KDIM // 2) * sin) * RET_KDIM ** -0.5
    v = proj[:, 2 * width:3 * width]
    gate = proj[:, 3 * width:]
    vb = v.astype(BF16)
    kb = k.astype(BF16)

    head = lax.broadcasted_iota(jnp.int32, (RET_TILE, width), 1) // RET_KDIM
    y = _dot(q.astype(BF16), state_ref[...].astype(BF16)) * rowdec_ref[...]
    for h in range(RET_HEADS):
        qh = jnp.where(head == h, q, 0.0).astype(BF16)
        scores = _dot_nt(qh, kb) * dmask_ref[h]
        y = y + jnp.where(head == h, _dot(scores.astype(BF16), vb), 0.0)

    kd = (k * keydec_ref[...]).astype(BF16)
    row_head = lax.broadcasted_iota(jnp.int32, (width, width), 0) // RET_KDIM
    col_head = lax.broadcasted_iota(jnp.int32, (width, width), 1) // RET_KDIM
    kv = jnp.where(row_head == col_head, _dot_tn(kd, vb), 0.0)
    state_ref[...] = state_ref[...] * tiledec_ref[...] + kv

    inv = 1.0 / RET_KDIM
    mu = jnp.zeros_like(y)
    for h in range(RET_HEADS):
        s = jnp.sum(jnp.where(head == h, y, 0.0), axis=1, keepdims=True) * inv
        mu = jnp.where(head == h, s, mu)
    d = y - mu
    var = jnp.zeros_like(y)
    for h in range(RET_HEADS):
        s = jnp.sum(jnp.where(head == h, d * d, 0.0), axis=1, keepdims=True) * inv
        var = jnp.where(head == h, s, var)
    yn = d * lax.rsqrt(var + GN_EPS)
    o_ref[...] = (gate * jax.nn.sigmoid(gate) * yn).astype(o_ref.dtype)


def _retention_tables():
    heads = np.arange(RET_HEADS, dtype=np.float64)
    log_gamma = np.log1p(-np.exp2(-5.0 - heads))
    t = np.arange(RET_TILE)
    chunk = t // CHUNK
    diff = (t[:, None] - t[None, :]).astype(np.float64)
    same = chunk[:, None] == chunk[None, :]
    earlier = chunk[None, :] < chunk[:, None]
    expo = np.where(same, np.abs(diff), diff)
    dmask = np.where((same | earlier)[None], np.exp(log_gamma[:, None, None] * expo[None]), 0.0)
    rowdec = np.exp(log_gamma[None, :] * (t[:, None] + 1.0))
    keydec = np.exp(log_gamma[None, :] * (RET_TILE - 1.0 - t[:, None]))
    tiledec = np.exp(log_gamma * RET_TILE)
    rep = lambda a: np.repeat(a, RET_KDIM, axis=-1)
    width = RET_HEADS * RET_KDIM
    tiledec_full = np.broadcast_to(rep(tiledec[None, :]).T, (width, width))
    return (jnp.asarray(dmask, F32), jnp.asarray(rep(rowdec), F32), jnp.asarray(rep(keydec), F32),
            jnp.asarray(tiledec_full, F32))


def _pool_part(z, wp_ref, scale_ref, o_ref, pool_ref, seq_tile):
    n_rows = POOL_PAD + TM
    pool_ref[0, POOL_PAD:, :] = z
    sums = []
    for level in range(len(POOL_WINDOWS)):
        lo, shift = 8 * (level + 1), 2 ** level
        s = pool_ref[level, lo:n_rows, :] + pool_ref[level, lo - shift:n_rows - shift, :]
        if level + 1 < len(POOL_WINDOWS):
            pool_ref[level + 1, lo:n_rows, :] = s
        sums.append(s[POOL_PAD - lo:])
    group = lax.broadcasted_iota(jnp.int32, (TM, BRANCH_WIDTH), 1) // (BRANCH_WIDTH // len(POOL_WINDOWS))
    win = sums[-1]
    for gi in range(len(POOL_WINDOWS) - 1):
        win = jnp.where(group == gi, sums[gi], win)
    t = seq_tile * TM + lax.broadcasted_iota(jnp.int32, (TM, BRANCH_WIDTH), 0)
    count = jnp.minimum(t + 1, jnp.left_shift(2, group)).astype(F32)
    pooled = win / count - z
    y = _dot(pooled.astype(BF16), wp_ref[...]) * scale_ref[...]
    o_ref[...] = y.astype(o_ref.dtype)
    pool_ref[0, POOL_PAD - POOL_HALO:POOL_PAD, :] = pool_ref[0, n_rows - POOL_HALO:n_rows, :]


def _rms_norm(x, g):
    return x * lax.rsqrt(jnp.mean(x * x, axis=-1, keepdims=True) + RMS_EPS) * g


def _mla_part(proj, cos, sin, qg_ref, kvg_ref, wq1_ref, wq2_ref, wk_ref, wvt_ref, q_ref, k_ref, vt_ref):
    cq = _rms_norm(proj[:, :MLA_Q_RANK], qg_ref[...]).astype(BF16)
    ckv = _rms_norm(proj[:, MLA_Q_RANK:MLA_Q_RANK + MLA_KV_RANK], kvg_ref[...]).astype(BF16)
    k_raw = proj[:, MLA_Q_RANK + MLA_KV_RANK:]
    k_pe = pltpu.roll(k_raw * cos + _swap_halves(k_raw, MLA_ROPE // 2) * sin, MLA_NOPE, 1)
    ck = pltpu.roll(cos, MLA_NOPE, 1)
    sk = pltpu.roll(sin, MLA_NOPE, 1)
    lane = lax.broadcasted_iota(jnp.int32, ck.shape, 1)
    scale = MLA_QK ** -0.5 * math.log2(math.e)
    cq_tab = (ck + jnp.where(lane < MLA_NOPE, 1.0, 0.0)) * scale
    sq_tab = sk * scale
    tile4 = lambda a: jnp.concatenate([a] * MLA_HEADS, axis=1)
    q = _dot(cq, wq1_ref[...]) * tile4(cq_tab) + _dot(cq, wq2_ref[...]) * tile4(sq_tab)
    k = _dot(ckv, wk_ref[...]) + tile4(k_pe)
    q_ref[...] = q.astype(q_ref.dtype)
    k_ref[...] = k.astype(k_ref.dtype)
    for t in range(TM // ATT_TK):
        vt_ref[t] = _dot_nt(wvt_ref[...], ckv[t * ATT_TK:(t + 1) * ATT_TK]).astype(vt_ref.dtype)


def _mixers_kernel(x_ref, w_ref, cs_ref, expand_ref,
                   lng_ref, lnb_ref, ws_ref, bias_ref, wp_ref, scale_ref,
                   qg_ref, kvg_ref, wq1_ref, wq2_ref, wk_ref, wvt_ref,
                   dmask_ref, rowdec_ref, keydec_ref, tiledec_ref,
                   ya_ref, yb_ref, yc_ref, q_ref, k_ref, vt_ref, state_ref, pool_ref):
    seq_tile = pl.program_id(1)

    @pl.when(seq_tile == 0)
    def _():
        state_ref[...] = jnp.zeros_like(state_ref)
        pool_ref[0, 0:POOL_PAD, :] = jnp.zeros((POOL_PAD, BRANCH_WIDTH), F32)

    xb = x_ref[...].astype(BF16)
    p_sgu = _dot_nt(xb, w_ref[_OFF_SGU:_OFF_RET, :])
    p_pool = _dot_nt(xb, w_ref[_OFF_POOL:_OFF_CQ, :])
    p_mla = _dot_nt(xb, w_ref[_OFF_CQ:MIX_COLS, :])
    p_ret = _dot_nt(xb, w_ref[_OFF_RET:_OFF_POOL, :])
    cs = cs_ref[...]
    hi = cs.astype(BF16)
    rest = cs - hi.astype(F32)
    mid = rest.astype(BF16)
    lo = (rest - mid.astype(F32)).astype(BF16)
    rope = _dot(jnp.concatenate([hi, mid, lo], axis=1), expand_ref[...])
    ret_cos, ret_sin, mla_cos, mla_sin = (rope[:, t * HEAD_PAD:(t + 1) * HEAD_PAD] for t in range(4))
    _sgu_part(p_sgu, lng_ref, lnb_ref, ws_ref, bias_ref, ya_ref)
    _pool_part(p_pool, wp_ref, scale_ref, yc_ref, pool_ref, seq_tile)
    _mla_part(p_mla, mla_cos, mla_sin, qg_ref, kvg_ref, wq1_ref, wq2_ref, wk_ref, wvt_ref,
              q_ref, k_ref, vt_ref)
    for r in range(TM // RET_TILE):
        rows = slice(r * RET_TILE, (r + 1) * RET_TILE)
        cos = jnp.concatenate([ret_cos[rows], ret_cos[rows]], axis=1)
        sin = jnp.concatenate([ret_sin[rows], ret_sin[rows]], axis=1)
        _ret_part(p_ret[rows], cos, sin, dmask_ref, rowdec_ref, keydec_ref, tiledec_ref,
                  yb_ref.at[rows, :], state_ref)


def _mixers(layer, x3, w_mix, rope, sgu_params, pool_params, mla_params, casts=()):
    bsz, seq, _ = x3.shape
    width = RET_HEADS * RET_KDIM
    ret_tables = _retention_tables()
    seq_tiles = seq // TM
    tile = lambda last: pl.BlockSpec((None, TM, last), lambda b, s: (b, s, 0))
    layered = tuple(sgu_params) + tuple(pool_params) + tuple(mla_params)
    in_specs = ([tile(D_MODEL), _const_spec(w_mix.shape), tile(HEAD_PAD), _const_spec(rope[1].shape)]
                + [_layer_spec(a, layer) for a in layered]
                + [_const_spec(a.shape) for a in ret_tables])
    cast_in, cast_out, cast_shapes = _cast_specs(casts, bsz * seq_tiles, lambda b, s: b * seq_tiles + s)
    branch = jax.ShapeDtypeStruct((bsz, seq, BRANCH_WIDTH), BF16)
    qk = jax.ShapeDtypeStruct((bsz, seq, MLA_HEADS * HEAD_PAD), BF16)
    vt = jax.ShapeDtypeStruct((bsz, seq // ATT_TK, MLA_HEADS * MLA_V, ATT_TK), BF16)
    return pl.pallas_call(
        _with_casts(_mixers_kernel, len(in_specs), 6, casts, bsz * seq_tiles),
        grid=(bsz, seq_tiles),
        in_specs=in_specs + cast_in,
        out_specs=[tile(BRANCH_WIDTH)] * 3 + [tile(MLA_HEADS * HEAD_PAD)] * 2
                  + [pl.BlockSpec((None, TM // ATT_TK, MLA_HEADS * MLA_V, ATT_TK), lambda b, s: (b, s, 0, 0))]
                  + cast_out,
        out_shape=[branch, branch, branch, qk, qk, vt] + cast_shapes,
        scratch_shapes=[pltpu.VMEM((width, width), F32),
                        pltpu.VMEM((len(POOL_WINDOWS), POOL_PAD + TM, BRANCH_WIDTH), F32)],
        compiler_params=_params(2, "arbitrary"),
        name="token_mixers",
    )(x3, w_mix, *rope, *sgu_params, *pool_params, *mla_params, *ret_tables, *_cast_operands(casts))


def _attn_kernel(q_ref, k_ref, vt_ref, o_ref, st_ref, m_ref, acc_ref):
    qi = pl.program_id(1)
    key_chunk = lax.broadcasted_iota(jnp.int32, (ATT_TK, ATT_TQ), 0) // CHUNK
    qry_chunk = lax.broadcasted_iota(jnp.int32, (ATT_TK, ATT_TQ), 1) // CHUNK
    diag_masks = [key_chunk + t * (ATT_TK // CHUNK) <= qry_chunk for t in range(ATT_TQ // ATT_TK)]
    heads = range(MLA_HEADS)

    every_query = slice(0, ATT_TQ)

    def scores(j, slot, queries=every_query):
        rows = pl.ds(pl.multiple_of(j * ATT_TK, ATT_TK), ATT_TK)
        for h in heads:
            cols = slice(h * HEAD_PAD, (h + 1) * HEAD_PAD)
            st_ref[slot, h, :, queries] = _dot_nt(k_ref[rows, cols], q_ref[queries, cols])

    ones = jnp.ones((ATT_ONES, ATT_TK), BF16)

    def softmax_pv(j, slot, mask=None, queries=every_query):
        for h in heads:
            load = lambda: (st_ref[slot, h, :, queries] if mask is None
                            else jnp.where(mask[:, queries], st_ref[slot, h, :, queries], -jnp.inf))
            m = m_ref[h, :, queries]
            m_new = jnp.maximum(m, jnp.max(load(), axis=0, keepdims=True))
            p = jnp.exp2((load() - m_new).astype(BF16))
            lhs = jnp.concatenate([vt_ref[j, h * MLA_V:(h + 1) * MLA_V, :], ones], axis=0)
            acc_ref[h, :, queries] = jnp.exp2(m - m_new) * acc_ref[h, :, queries] + _dot(lhs, p)
            m_ref[h, :, queries] = m_new

    def pair(i, carry):
        scores(2 * i + 1, 1)
        softmax_pv(2 * i, 0)
        scores(2 * i + 2, 0)
        softmax_pv(2 * i + 1, 1)
        return carry

    m_ref[...] = jnp.full(m_ref.shape, -jnp.inf, F32)
    acc_ref[...] = jnp.zeros(acc_ref.shape, F32)
    scores(0, 0)
    lax.fori_loop(0, qi, pair, 0)
    upper = slice(ATT_TK, ATT_TQ)
    scores(2 * qi + 1, 1, upper)
    softmax_pv(2 * qi, 0, diag_masks[0])
    softmax_pv(2 * qi + 1, 1, diag_masks[1], upper)
    out_t = jnp.concatenate([acc_ref[h, :MLA_V] / acc_ref[h, MLA_V:MLA_V + 1] for h in heads], axis=0)
    o_ref[...] = out_t.T.astype(o_ref.dtype)


def _attention(q3, k3, vt4):
    bsz, seq, _ = q3.shape
    return pl.pallas_call(
        _attn_kernel,
        grid=(bsz, seq // ATT_TQ),
        in_specs=[pl.BlockSpec((None, ATT_TQ, MLA_HEADS * HEAD_PAD), lambda b, i: (b, i, 0)),
                  pl.BlockSpec((None, seq, MLA_HEADS * HEAD_PAD), lambda b, i: (b, 0, 0)),
                  pl.BlockSpec((None, seq // ATT_TK, MLA_HEADS * MLA_V, ATT_TK), lambda b, i: (b, 0, 0, 0))],
        out_specs=pl.BlockSpec((None, ATT_TQ, MLA_HEADS * MLA_V), lambda b, i: (b, i, 0)),
        out_shape=jax.ShapeDtypeStruct((bsz, seq, MLA_HEADS * MLA_V), BF16),
        scratch_shapes=[pltpu.VMEM((2, MLA_HEADS, ATT_TK, ATT_TQ), F32),
                        pltpu.VMEM((MLA_HEADS, 1, ATT_TQ), F32),
                        pltpu.VMEM((MLA_HEADS, MLA_V + ATT_ONES, ATT_TQ), F32)],
        compiler_params=_params(2),
        name="mla_attention",
    )(q3, k3, vt4)


def _merge_kernel(x_ref, ya_ref, yb_ref, yc_ref, yd_ref, wg_ref, wb_ref, wo_ref, g_ref, b_ref, o_ref):
    def residual(rows):
        x = x_ref[rows, :]
        xb = x.astype(BF16)
        merged = None
        for n, y_ref in enumerate((ya_ref, yb_ref, yc_ref, yd_ref)):
            gate = jax.nn.sigmoid(_dot(xb, wg_ref[:, n * D_MODEL:(n + 1) * D_MODEL]))
            term = gate * _dot(y_ref[rows, :], wb_ref[n * BRANCH_WIDTH:(n + 1) * BRANCH_WIDTH, :])
            merged = term if merged is None else merged + term
        return ALPHA * x + _dot(merged.astype(BF16), wo_ref[...])

    _pipelined_rows(o_ref, g_ref, b_ref, residual)


def _merge(layer, x, ys, wg, wb, wo, g, b, casts=()):
    n = x.shape[0]
    steps = n // TM_WIDE
    row = lambda last: pl.BlockSpec((TM_WIDE, last), lambda i: (i, 0))
    cast_in, cast_out, cast_shapes = _cast_specs(casts, steps, lambda i: i)
    return pl.pallas_call(
        _with_casts(_merge_kernel, 10, 1, casts, steps),
        grid=(steps,),
        in_specs=[row(D_MODEL)] + [row(BRANCH_WIDTH)] * N_BRANCH
                 + [_const_spec(a.shape) for a in (wg, wb, wo)]
                 + [_layer_spec(a, layer) for a in (g, b)] + cast_in,
        out_specs=[row(D_MODEL)] + cast_out,
        out_shape=[jax.ShapeDtypeStruct((n, D_MODEL), F32)] + cast_shapes,
        compiler_params=_params(1),
        name="merge_ln",
    )(x, *ys, wg, wb, wo, g, b, *_cast_operands(casts))


def _rope_tables(positions):
    n_r, n_m = RET_KDIM // 2, MLA_ROPE // 2
    inv_r = ROPE_BASE ** (-jnp.arange(0, RET_KDIM, 2, dtype=F32) / RET_KDIM)
    inv_m = ROPE_BASE ** (-jnp.arange(0, MLA_ROPE, 2, dtype=F32) / MLA_ROPE)
    pad = jnp.zeros((HEAD_PAD - 2 * (n_r + n_m),), F32)
    ang = positions.astype(F32)[..., None] * jnp.concatenate([inv_r, inv_r, inv_m, inv_m, pad])
    lane = np.arange(HEAD_PAD)
    is_cos = jnp.asarray((lane < n_r) | ((lane >= 2 * n_r) & (lane < 2 * n_r + n_m)))
    compact = jnp.where(is_cos, jnp.cos(ang), jnp.sin(ang))
    expand = np.zeros((HEAD_PAD, 4 * HEAD_PAD), np.float32)
    for l in range(HEAD_PAD):
        expand[l % n_r, l] = 1.0
        expand[n_r + l % n_r, HEAD_PAD + l] = -1.0 if l % RET_KDIM < n_r else 1.0
        if l < MLA_ROPE:
            expand[2 * n_r + l % n_m, 2 * HEAD_PAD + l] = 1.0
            expand[2 * n_r + n_m + l % n_m, 3 * HEAD_PAD + l] = -1.0 if l < n_m else 1.0
    return compact, jnp.asarray(np.tile(expand, (3, 1)), BF16)


def _swap_cols(w):
    half = w.shape[-1] // 2
    return jnp.concatenate([w[..., half:], w[..., :half]], axis=-1)


def _mla_weights(w_uq, w_ukv):
    depth = w_uq.shape[0]
    uq = w_uq.reshape(depth, MLA_Q_RANK, MLA_HEADS, MLA_QK)
    q_nope, q_rope = uq[..., :MLA_NOPE], uq[..., MLA_NOPE:]
    zq = jnp.zeros((depth, MLA_Q_RANK, MLA_HEADS, HEAD_PAD - MLA_QK), w_uq.dtype)
    wq1 = jnp.concatenate([q_nope, q_rope, zq], axis=-1)
    wq2 = jnp.concatenate([jnp.zeros_like(q_nope), _swap_cols(q_rope), zq], axis=-1)
    ukv = w_ukv.reshape(depth, MLA_KV_RANK, MLA_HEADS, MLA_NOPE + MLA_V)
    k_nope, v = ukv[..., :MLA_NOPE], ukv[..., MLA_NOPE:]
    wk = jnp.concatenate([k_nope, jnp.zeros_like(k_nope)], axis=-1)
    wvt = jnp.swapaxes(v.reshape(depth, MLA_KV_RANK, MLA_HEADS * MLA_V), 1, 2)
    flat = lambda a: a.reshape(depth, a.shape[1], MLA_HEADS * HEAD_PAD).astype(BF16)
    return flat(wq1), flat(wq2), flat(wk), wvt.astype(BF16)


def kernel(x, p, positions, ffn1_up, ffn1_down, ln1_g, ln1_b, w_in, sgu_ln_g, sgu_ln_b, sgu_w, sgu_b,
           pool_w, pool_scale, mla_q_norm, mla_kv_norm, mla_w_uq, mla_w_ukv, w_branch, w_out,
           ln2_g, ln2_b, ffn2_up, ffn2_down, w_ple_gate, w_ple, ln3_g, ln3_b):
    bsz, seq, dm = x.shape
    n = bsz * seq
    rope = _rope_tables(positions)
    bf = lambda a: a.astype(BF16)
    rows = lambda a: a[:, None, :]

    up1, down1 = bf(ffn1_up[0]), bf(ffn1_down[0])
    w_in_t = jnp.swapaxes(w_in, 1, 2)
    w_branch2 = w_branch.reshape(DEPTH, N_BRANCH * BRANCH_WIDTH, dm)
    wp = bf(w_ple)
    wq1, wq2, wk, wvt = _mla_weights(mla_w_uq, mla_w_ukv)
    sgu_bias = jnp.repeat(jnp.swapaxes(sgu_b, 1, 2), SGU_WIDTH // SGU_GROUPS, axis=2)
    groups = len(POOL_WINDOWS)
    pool_bd = bf(jnp.einsum("lgcd,gh->lgchd", pool_w, jnp.eye(groups, dtype=pool_w.dtype))
                 .reshape(DEPTH, BRANCH_WIDTH, BRANCH_WIDTH))
    p2 = p.reshape(DEPTH, n, PLE_DIM)

    h = x.reshape(n, dm)
    for i in range(DEPTH):
        last = i + 1 == DEPTH
        h, w_mix, w_gate, wb, wo = _ffn_ln(
            i, h, up1, down1, rows(ln1_g), rows(ln1_b),
            (_cast(w_in_t, i, MIX_COLS), _CastT(w_in_t, i, _OFF_GATE, N_BRANCH * dm), _cast(w_branch2, i),
             _cast(w_out, i)))
        y_a, y_b, y_c, q, k, vt, up2, wpg, *nxt_up = _mixers(
            i, h.reshape(bsz, seq, dm), w_mix, rope,
            (rows(sgu_ln_g), rows(sgu_ln_b), sgu_w, sgu_bias), (pool_bd, rows(pool_scale)),
            (rows(mla_q_norm), rows(mla_kv_norm), wq1, wq2, wk, wvt),
            (_cast(ffn2_up, i), _cast(w_ple_gate, i)) + (() if last else (_cast(ffn1_up, i + 1),)))
        y_d = _attention(q, k, vt)
        ys = tuple(y.reshape(n, BRANCH_WIDTH) for y in (y_a, y_b, y_c, y_d))
        h, down2, *nxt_down = _merge(
            i, h, ys, w_gate, wb, wo, rows(ln2_g), rows(ln2_b),
            (_cast(ffn2_down, i),) + (() if last else (_cast(ffn1_down, i + 1),)))
        h = _ffn_ple_ln(i, h, p2, up2, down2, wpg, wp, rows(ln3_g), rows(ln3_b))
        if not last:
            (up1,), (down1,) = nxt_up, nxt_down
    return h.reshape(bsz, seq, dm)
```

```python
import math
from typing import NamedTuple

import numpy as np
import jax
import jax.numpy as jnp
from jax import lax
from jax.experimental import pallas as pl
from jax.experimental.pallas import tpu as pltpu

D_MODEL = 1024
DEPTH = 2
CHUNK = 64
SGU_WIDTH = 256
SGU_BLOCK = 128
SGU_GROUPS = 4
RET_HEADS = 4
RET_KDIM = 64
POOL_WINDOWS = (2, 4, 8, 16)
POOL_HALO = 16
POOL_PAD = 32
MLA_HEADS = 4
MLA_NOPE = 64
MLA_ROPE = 32
MLA_QK = MLA_NOPE + MLA_ROPE
MLA_V = 64
MLA_Q_RANK = 256
MLA_KV_RANK = 128
HEAD_PAD = 128
ROPE_BASE = 10000.0
N_BRANCH = 4
BRANCH_WIDTH = 256
D_FF = 2816
PLE_DIM = 256
ALPHA = (2 * DEPTH) ** 0.25
LN_EPS = 1e-5
RMS_EPS = 1e-6
GN_EPS = 1e-5

_OFF_SGU = 0
_OFF_RET = 512
_OFF_POOL = 1536
_OFF_CQ = 1792
_OFF_GATE = 2208
MIX_COLS = 2304

TM = 512
TM_WIDE = 1024
SUB = 256
RET_TILE = 256
ATT_TQ = 512
ATT_TK = 256
ATT_ONES = 16
VMEM_LIMIT = 56 * 1024 * 1024

BF16 = jnp.bfloat16
F32 = jnp.float32


def _dot(a, b):
    return jnp.dot(a, b, preferred_element_type=F32)


def _dot_nt(a, b):
    return lax.dot_general(a, b, (((1,), (1,)), ((), ())), preferred_element_type=F32)


def _dot_tn(a, b):
    return lax.dot_general(a, b, (((0,), (0,)), ((), ())), preferred_element_type=F32)


def _layer_norm(r, g, b, eps):
    mu = jnp.mean(r, axis=-1, keepdims=True)
    d = r - mu
    var = jnp.mean(d * d, axis=-1, keepdims=True)
    return d * lax.rsqrt(var + eps) * g + b


def _const_spec(shape):
    zeros = (0,) * len(shape)
    return pl.BlockSpec(shape, lambda *_: zeros, pipeline_mode=pl.Buffered(1))


def _layer_spec(arr, layer):
    index = (layer,) + (0,) * (arr.ndim - 1)
    return pl.BlockSpec((None,) + arr.shape[1:], lambda *_: index, pipeline_mode=pl.Buffered(1))


def _params(n_grid, semantics="parallel"):
    return pltpu.CompilerParams(
        dimension_semantics=(semantics,) * n_grid, vmem_limit_bytes=VMEM_LIMIT)


class _Cast(NamedTuple):
    src: jax.Array
    layer: int
    rows: int


class _CastT(NamedTuple):
    src: jax.Array
    layer: int
    row0: int
    rows: int


def _cast(src, layer, rows=None):
    return _Cast(src, layer, src.shape[1] if rows is None else rows)


def _cast_specs(casts, n_steps, step_of):
    in_specs, out_specs, out_shapes = [], [], []
    for c in casts:
        blk, cols = c.rows // n_steps, c.src.shape[2]
        if isinstance(c, _Cast):
            in_specs.append(pl.BlockSpec((None, blk, cols), lambda *g, c=c: (c.layer, step_of(*g), 0)))
            out_specs.append(pl.BlockSpec((blk, cols), lambda *g: (step_of(*g), 0)))
            out_shapes.append(jax.ShapeDtypeStruct((c.rows, cols), BF16))
        else:
            first = c.row0 // blk
            in_specs += [pl.BlockSpec((None, blk, cols),
                                      lambda *g, c=c, k=k: (c.layer, first + step_of(*g) + k, 0))
                         for k in range(2)]
            out_specs.append(pl.BlockSpec((cols, blk), lambda *g: (0, step_of(*g))))
            out_shapes.append(jax.ShapeDtypeStruct((cols, c.rows), BF16))
    return in_specs, out_specs, out_shapes


def _cast_operands(casts):
    return [c.src for c in casts for _ in range(1 if isinstance(c, _Cast) else 2)]


def _with_casts(body, n_in, n_out, casts, n_steps):
    n_src = sum(1 if isinstance(c, _Cast) else 2 for c in casts)

    def kernel(*refs):
        ins, rest = refs[:n_in], refs[n_in:]
        cast_in, rest = list(rest[:n_src]), rest[n_src:]
        outs, rest = rest[:n_out], rest[n_out:]
        cast_out, scratch = rest[:len(casts)], rest[len(casts):]
        body(*ins, *outs, *scratch)
        for c, dst_ref in zip(casts, cast_out):
            if isinstance(c, _Cast):
                dst_ref[...] = cast_in.pop(0)[...].astype(BF16)
            else:
                lo_ref, hi_ref = cast_in.pop(0), cast_in.pop(0)
                off = c.row0 % (c.rows // n_steps)
                window = jnp.concatenate([lo_ref[off:, :], hi_ref[:off, :]], axis=0)
                dst_ref[...] = window.T.astype(BF16)

    return kernel


def _pipelined_rows(o_ref, g_ref, b_ref, residual_fn):
    pending = None
    for s in range(o_ref.shape[0] // SUB):
        rows = slice(s * SUB, (s + 1) * SUB)
        r = residual_fn(rows)
        if pending is not None:
            o_ref[pending[0], :] = _layer_norm(pending[1], g_ref[...], b_ref[...], LN_EPS)
        pending = (rows, r)
    o_ref[pending[0], :] = _layer_norm(pending[1], g_ref[...], b_ref[...], LN_EPS)


def _ffn_body(x, up_ref, down_ref):
    xb = x.astype(BF16)
    a = _dot(xb, up_ref[:, :D_FF])
    b = _dot(xb, up_ref[:, D_FF:])
    h = (a * jax.nn.sigmoid(a) * b).astype(BF16)
    return xb, _dot(h, down_ref[...])


def _ffn_ln_kernel(x_ref, up_ref, down_ref, g_ref, b_ref, o_ref):
    def residual(rows):
        x = x_ref[rows, :]
        _, y = _ffn_body(x, up_ref, down_ref)
        return ALPHA * x + 0.5 * y

    _pipelined_rows(o_ref, g_ref, b_ref, residual)


def _ffn_ple_ln_kernel(x_ref, p_ref, up_ref, down_ref, wpg_ref, wp_ref, g_ref, b_ref, o_ref):
    def residual(rows):
        x = x_ref[rows, :]
        xb, y = _ffn_body(x, up_ref, down_ref)
        ple = jax.nn.sigmoid(_dot(xb, wpg_ref[...])) * _dot(p_ref[rows, :].astype(BF16), wp_ref[...])
        return ALPHA * x + 0.5 * y + ple

    _pipelined_rows(o_ref, g_ref, b_ref, residual)


def _ffn_ln(layer, x, up, down, g, b, casts=()):
    n = x.shape[0]
    steps = n // TM_WIDE
    row = pl.BlockSpec((TM_WIDE, D_MODEL), lambda i: (i, 0))
    cast_in, cast_out, cast_shapes = _cast_specs(casts, steps, lambda i: i)
    return pl.pallas_call(
        _with_casts(_ffn_ln_kernel, 5, 1, casts, steps),
        grid=(steps,),
        in_specs=[row, _const_spec(up.shape), _const_spec(down.shape), _layer_spec(g, layer),
                  _layer_spec(b, layer)] + cast_in,
        out_specs=[row] + cast_out,
        out_shape=[jax.ShapeDtypeStruct((n, D_MODEL), F32)] + cast_shapes,
        compiler_params=_params(1),
        name="ffn_ln",
    )(x, up, down, g, b, *_cast_operands(casts))


def _ffn_ple_ln(layer, x, p, up, down, wpg, wp, g, b):
    n = x.shape[0]
    row = pl.BlockSpec((TM_WIDE, D_MODEL), lambda i: (i, 0))
    return pl.pallas_call(
        _ffn_ple_ln_kernel,
        grid=(n // TM_WIDE,),
        in_specs=[row, pl.BlockSpec((None, TM_WIDE, PLE_DIM), lambda i: (layer, i, 0)),
                  _const_spec(up.shape), _const_spec(down.shape), _const_spec(wpg.shape)]
                 + [_layer_spec(a, layer) for a in (wp, g, b)],
        out_specs=row,
        out_shape=jax.ShapeDtypeStruct((n, D_MODEL), F32),
        compiler_params=_params(1),
        name="ffn_ple_ln",
    )(x, p, up, down, wpg, wp, g, b)


def _swap_halves(x, half):
    width = x.shape[-1]
    lane = lax.broadcasted_iota(jnp.int32, x.shape, x.ndim - 1)
    first = (lane % (2 * half)) < half
    return jnp.where(first, pltpu.roll(x, width - half, x.ndim - 1), pltpu.roll(x, half, x.ndim - 1))


def _sgu_part(uv, lng_ref, lnb_ref, ws_ref, bias_ref, o_ref):
    u = jax.nn.gelu(uv[:, :SGU_WIDTH])
    v = _layer_norm(jax.nn.gelu(uv[:, SGU_WIDTH:]), lng_ref[...], lnb_ref[...], LN_EPS)
    vb = v.astype(BF16)
    t_row = lax.broadcasted_iota(jnp.int32, (SGU_BLOCK, SGU_BLOCK), 0)
    t_col = lax.broadcasted_iota(jnp.int32, (SGU_BLOCK, SGU_BLOCK), 1)
    causal = t_row >= t_col
    w_groups = [jnp.where(causal, ws_ref[g], 0.0).astype(BF16) for g in range(SGU_GROUPS)]
    group = lax.broadcasted_iota(jnp.int32, (SGU_BLOCK, SGU_WIDTH), 1) // (SGU_WIDTH // SGU_GROUPS)
    bias = bias_ref[...]
    for blk in range(TM // SGU_BLOCK):
        rows = slice(blk * SGU_BLOCK, (blk + 1) * SGU_BLOCK)
        v_blk = vb[rows]
        mixed = bias
        for g in range(SGU_GROUPS):
            mixed = mixed + jnp.where(group == g, _dot(w_groups[g], v_blk), 0.0)
        o_ref[rows, :] = (u[rows] * mixed).astype(o_ref.dtype)


def _ret_part(proj, cos, sin, dmask_ref, rowdec_ref, keydec_ref, tiledec_ref, o_ref, state_ref):
    width = RET_HEADS * RET_KDIM
    q = proj[:, :width]
    k = proj[:, width:2 * width]
    q = q * cos + _swap_halves(q, RET_KDIM // 2) * sin
    k = (k * cos + _swap_halves(k, RET_KDIM // 2) * sin) * RET_KDIM ** -0.5
    v = proj[:, 2 * width:3 * width]
    gate = proj[:, 3 * width:]
    vb = v.astype(BF16)
    kb = k.astype(BF16)

    head = lax.broadcasted_iota(jnp.int32, (RET_TILE, width), 1) // RET_KDIM
    y = _dot(q.astype(BF16), state_ref[...].astype(BF16)) * rowdec_ref[...]
    for h in range(RET_HEADS):
        qh = jnp.where(head == h, q, 0.0).astype(BF16)
        scores = _dot_nt(qh, kb) * dmask_ref[h]
        y = y + jnp.where(head == h, _dot(scores.astype(BF16), vb), 0.0)

    kd = (k * keydec_ref[...]).astype(BF16)
    row_head = lax.broadcasted_iota(jnp.int32, (width, width), 0) // RET_KDIM
    col_head = lax.broadcasted_iota(jnp.int32, (width, width), 1) // RET_KDIM
    kv = jnp.where(row_head == col_head, _dot_tn(kd, vb), 0.0)
    state_ref[...] = state_ref[...] * tiledec_ref[...] + kv

    inv = 1.0 / RET_KDIM
    mu = jnp.zeros_like(y)
    for h in range(RET_HEADS):
        s = jnp.sum(jnp.where(head == h, y, 0.0), axis=1, keepdims=True) * inv
        mu = jnp.where(head == h, s, mu)
    d = y - mu
    var = jnp.zeros_like(y)
    for h in range(RET_HEADS):
        s = jnp.sum(jnp.where(head == h, d * d, 0.0), axis=1, keepdims=True) * inv
        var = jnp.where(head == h, s, var)
    yn = d * lax.rsqrt(var + GN_EPS)
    o_ref[...] = (gate * jax.nn.sigmoid(gate) * yn).astype(o_ref.dtype)


def _retention_tables():
    heads = np.arange(RET_HEADS, dtype=np.float64)
    log_gamma = np.log1p(-np.exp2(-5.0 - heads))
    t = np.arange(RET_TILE)
    chunk = t // CHUNK
    diff = (t[:, None] - t[None, :]).astype(np.float64)
    same = chunk[:, None] == chunk[None, :]
    earlier = chunk[None, :] < chunk[:, None]
    expo = np.where(same, np.abs(diff), diff)
    dmask = np.where((same | earlier)[None], np.exp(log_gamma[:, None, None] * expo[None]), 0.0)
    rowdec = np.exp(log_gamma[None, :] * (t[:, None] + 1.0))
    keydec = np.exp(log_gamma[None, :] * (RET_TILE - 1.0 - t[:, None]))
    tiledec = np.exp(log_gamma * RET_TILE)
    rep = lambda a: np.repeat(a, RET_KDIM, axis=-1)
    width = RET_HEADS * RET_KDIM
    tiledec_full = np.broadcast_to(rep(tiledec[None, :]).T, (width, width))
    return (jnp.asarray(dmask, F32), jnp.asarray(rep(rowdec), F32), jnp.asarray(rep(keydec), F32),
            jnp.asarray(tiledec_full, F32))


def _pool_part(z, wp_ref, scale_ref, o_ref, pool_ref, seq_tile):
    n_rows = POOL_PAD + TM
    pool_ref[0, POOL_PAD:, :] = z
    sums = []
    for level in range(len(POOL_WINDOWS)):
        lo, shift = 8 * (level + 1), 2 ** level
        s = pool_ref[level, lo:n_rows, :] + pool_ref[level, lo - shift:n_rows - shift, :]
        if level + 1 < len(POOL_WINDOWS):
            pool_ref[level + 1, lo:n_rows, :] = s
        sums.append(s[POOL_PAD - lo:])
    group = lax.broadcasted_iota(jnp.int32, (TM, BRANCH_WIDTH), 1) // (BRANCH_WIDTH // len(POOL_WINDOWS))
    win = sums[-1]
    for gi in range(len(POOL_WINDOWS) - 1):
        win = jnp.where(group == gi, sums[gi], win)
    t = seq_tile * TM + lax.broadcasted_iota(jnp.int32, (TM, BRANCH_WIDTH), 0)
    count = jnp.minimum(t + 1, jnp.left_shift(2, group)).astype(F32)
    pooled = win / count - z
    y = _dot(pooled.astype(BF16), wp_ref[...]) * scale_ref[...]
    o_ref[...] = y.astype(o_ref.dtype)
    pool_ref[0, POOL_PAD - POOL_HALO:POOL_PAD, :] = pool_ref[0, n_rows - POOL_HALO:n_rows, :]


def _rms_norm(x, g):
    return x * lax.rsqrt(jnp.mean(x * x, axis=-1, keepdims=True) + RMS_EPS) * g


def _mla_part(proj, cos, sin, qg_ref, kvg_ref, wq1_ref, wq2_ref, wk_ref, wvt_ref, qt_ref, k_ref, vt_ref):
    cq = _rms_norm(proj[:, :MLA_Q_RANK], qg_ref[...]).astype(BF16)
    ckv = _rms_norm(proj[:, MLA_Q_RANK:MLA_Q_RANK + MLA_KV_RANK], kvg_ref[...]).astype(BF16)
    k_raw = proj[:, MLA_Q_RANK + MLA_KV_RANK:]
    k_pe = pltpu.roll(k_raw * cos + _swap_halves(k_raw, MLA_ROPE // 2) * sin, MLA_NOPE, 1)
    ck = pltpu.roll(cos, MLA_NOPE, 1)
    sk = pltpu.roll(sin, MLA_NOPE, 1)
    lane = lax.broadcasted_iota(jnp.int32, ck.shape, 1)
    scale = MLA_QK ** -0.5 * math.log2(math.e)
    cq_tab = (ck + jnp.where(lane < MLA_NOPE, 1.0, 0.0)) * scale
    sq_tab = sk * scale
    tile4 = lambda a: jnp.concatenate([a] * MLA_HEADS, axis=1)
    q = _dot(cq, wq1_ref[...]) * tile4(cq_tab) + _dot(cq, wq2_ref[...]) * tile4(sq_tab)
    k = _dot(ckv, wk_ref[...]) + tile4(k_pe)
    for h in range(MLA_HEADS):
        rows = slice(h * HEAD_PAD, (h + 1) * HEAD_PAD)
        qt_ref[rows, :] = q[:, rows].T.astype(qt_ref.dtype)
    k_ref[...] = k.astype(k_ref.dtype)
    for t in range(TM // ATT_TK):
        vt_ref[t] = _dot_nt(wvt_ref[...], ckv[t * ATT_TK:(t + 1) * ATT_TK]).astype(vt_ref.dtype)


def _mixers_kernel(x_ref, w_ref, cs_ref, expand_ref,
                   lng_ref, lnb_ref, ws_ref, bias_ref, wp_ref, scale_ref,
                   qg_ref, kvg_ref, wq1_ref, wq2_ref, wk_ref, wvt_ref,
                   dmask_ref, rowdec_ref, keydec_ref, tiledec_ref,
                   ya_ref, yb_ref, yc_ref, qt_ref, k_ref, vt_ref, state_ref, pool_ref):
    seq_tile = pl.program_id(1)

    @pl.when(seq_tile == 0)
    def _():
        state_ref[...] = jnp.zeros_like(state_ref)
        pool_ref[0, 0:POOL_PAD, :] = jnp.zeros((POOL_PAD, BRANCH_WIDTH), F32)

    xb = x_ref[...].astype(BF16)
    p_sgu = _dot_nt(xb, w_ref[_OFF_SGU:_OFF_RET, :])
    p_pool = _dot_nt(xb, w_ref[_OFF_POOL:_OFF_CQ, :])
    p_mla = _dot_nt(xb, w_ref[_OFF_CQ:MIX_COLS, :])
    p_ret = _dot_nt(xb, w_ref[_OFF_RET:_OFF_POOL, :])
    cs = cs_ref[...]
    hi = cs.astype(BF16)
    rest = cs - hi.astype(F32)
    mid = rest.astype(BF16)
    lo = (rest - mid.astype(F32)).astype(BF16)
    rope = _dot(jnp.concatenate([hi, mid, lo], axis=1), expand_ref[...])
    ret_cos, ret_sin, mla_cos, mla_sin = (rope[:, t * HEAD_PAD:(t + 1) * HEAD_PAD] for t in range(4))
    _sgu_part(p_sgu, lng_ref, lnb_ref, ws_ref, bias_ref, ya_ref)
    _pool_part(p_pool, wp_ref, scale_ref, yc_ref, pool_ref, seq_tile)
    _mla_part(p_mla, mla_cos, mla_sin, qg_ref, kvg_ref, wq1_ref, wq2_ref, wk_ref, wvt_ref,
              qt_ref, k_ref, vt_ref)
    for r in range(TM // RET_TILE):
        rows = slice(r * RET_TILE, (r + 1) * RET_TILE)
        cos = jnp.concatenate([ret_cos[rows], ret_cos[rows]], axis=1)
        sin = jnp.concatenate([ret_sin[rows], ret_sin[rows]], axis=1)
        _ret_part(p_ret[rows], cos, sin, dmask_ref, rowdec_ref, keydec_ref, tiledec_ref,
                  yb_ref.at[rows, :], state_ref)


def _mixers(layer, x3, w_mix, rope, sgu_params, pool_params, mla_params, casts=()):
    bsz, seq, _ = x3.shape
    width = RET_HEADS * RET_KDIM
    ret_tables = _retention_tables()
    seq_tiles = seq // TM
    tile = lambda last: pl.BlockSpec((None, TM, last), lambda b, s: (b, s, 0))
    layered = tuple(sgu_params) + tuple(pool_params) + tuple(mla_params)
    in_specs = ([tile(D_MODEL), _const_spec(w_mix.shape), tile(HEAD_PAD), _const_spec(rope[1].shape)]
                + [_layer_spec(a, layer) for a in layered]
                + [_const_spec(a.shape) for a in ret_tables])
    cast_in, cast_out, cast_shapes = _cast_specs(casts, bsz * seq_tiles, lambda b, s: b * seq_tiles + s)
    branch = jax.ShapeDtypeStruct((bsz, seq, BRANCH_WIDTH), BF16)
    assert TM == ATT_TQ
    qt = jax.ShapeDtypeStruct((bsz, seq_tiles, MLA_HEADS * HEAD_PAD, TM), BF16)
    kk = jax.ShapeDtypeStruct((bsz, seq, MLA_HEADS * HEAD_PAD), BF16)
    vt = jax.ShapeDtypeStruct((bsz, seq // ATT_TK, MLA_HEADS * MLA_V, ATT_TK), BF16)
    return pl.pallas_call(
        _with_casts(_mixers_kernel, len(in_specs), 6, casts, bsz * seq_tiles),
        grid=(bsz, seq_tiles),
        in_specs=in_specs + cast_in,
        out_specs=[tile(BRANCH_WIDTH)] * 3
                  + [pl.BlockSpec((None, None, MLA_HEADS * HEAD_PAD, TM), lambda b, s: (b, s, 0, 0)),
                     tile(MLA_HEADS * HEAD_PAD),
                     pl.BlockSpec((None, TM // ATT_TK, MLA_HEADS * MLA_V, ATT_TK), lambda b, s: (b, s, 0, 0))]
                  + cast_out,
        out_shape=[branch, branch, branch, qt, kk, vt] + cast_shapes,
        scratch_shapes=[pltpu.VMEM((width, width), F32),
                        pltpu.VMEM((len(POOL_WINDOWS), POOL_PAD + TM, BRANCH_WIDTH), F32)],
        compiler_params=_params(2, "arbitrary"),
        name="token_mixers",
    )(x3, w_mix, *rope, *sgu_params, *pool_params, *mla_params, *ret_tables, *_cast_operands(casts))


def _attn_kernel(qt_ref, k_ref, vt_ref, o_ref, st_ref, m_ref, acc_ref):
    qi = pl.program_id(1)
    key_chunk = lax.broadcasted_iota(jnp.int32, (ATT_TK, ATT_TQ), 0) // CHUNK
    qry_chunk = lax.broadcasted_iota(jnp.int32, (ATT_TK, ATT_TQ), 1) // CHUNK
    diag_masks = [key_chunk + t * (ATT_TK // CHUNK) <= qry_chunk for t in range(ATT_TQ // ATT_TK)]
    heads = range(MLA_HEADS)

    every_query = slice(0, ATT_TQ)

    def scores(j, slot, queries=every_query, hs=heads):
        rows = pl.ds(pl.multiple_of(j * ATT_TK, ATT_TK), ATT_TK)
        for h in hs:
            cols = slice(h * HEAD_PAD, (h + 1) * HEAD_PAD)
            st_ref[slot, h, :, queries] = _dot(k_ref[rows, cols], qt_ref[cols, queries])

    ones = jnp.ones((ATT_ONES, ATT_TK), BF16)

    def softmax_pv(j, slot, mask=None, queries=every_query, hs=heads):
        for h in hs:
            load = lambda: (st_ref[slot, h, :, queries] if mask is None
                            else jnp.where(mask[:, queries], st_ref[slot, h, :, queries], -jnp.inf))
            m = m_ref[h, :, queries]
            m_new = jnp.maximum(m, jnp.max(load(), axis=0, keepdims=True))
            p = jnp.exp2((load() - m_new).astype(BF16))
            lhs = jnp.concatenate([vt_ref[j, h * MLA_V:(h + 1) * MLA_V, :], ones], axis=0)
            acc_ref[h, :, queries] = jnp.exp2(m - m_new) * acc_ref[h, :, queries] + _dot(lhs, p)
            m_ref[h, :, queries] = m_new

    def pair(i, carry):
        for h in heads:
            scores(2 * i + 1, 1, hs=(h,))
            softmax_pv(2 * i, 0, hs=(h,))
        for h in heads:
            scores(2 * i + 2, 0, hs=(h,))
            softmax_pv(2 * i + 1, 1, hs=(h,))
        return carry

    m_ref[...] = jnp.full(m_ref.shape, -jnp.inf, F32)
    acc_ref[...] = jnp.zeros(acc_ref.shape, F32)
    scores(0, 0)
    lax.fori_loop(0, qi, pair, 0)
    upper = slice(ATT_TK, ATT_TQ)
    scores(2 * qi + 1, 1, upper)
    softmax_pv(2 * qi, 0, diag_masks[0])
    softmax_pv(2 * qi + 1, 1, diag_masks[1], upper)
    out_t = jnp.concatenate([acc_ref[h, :MLA_V] / acc_ref[h, MLA_V:MLA_V + 1] for h in heads], axis=0)
    o_ref[...] = out_t.T.astype(o_ref.dtype)


def _attention(qt4, k3, vt4):
    bsz, seq, _ = k3.shape
    return pl.pallas_call(
        _attn_kernel,
        grid=(bsz, seq // ATT_TQ),
        in_specs=[pl.BlockSpec((None, None, MLA_HEADS * HEAD_PAD, ATT_TQ), lambda b, i: (b, i, 0, 0)),
                  pl.BlockSpec((None, seq, MLA_HEADS * HEAD_PAD), lambda b, i: (b, 0, 0)),
                  pl.BlockSpec((None, seq // ATT_TK, MLA_HEADS * MLA_V, ATT_TK), lambda b, i: (b, 0, 0, 0))],
        out_specs=pl.BlockSpec((None, ATT_TQ, MLA_HEADS * MLA_V), lambda b, i: (b, i, 0)),
        out_shape=jax.ShapeDtypeStruct((bsz, seq, MLA_HEADS * MLA_V), BF16),
        scratch_shapes=[pltpu.VMEM((2, MLA_HEADS, ATT_TK, ATT_TQ), F32),
                        pltpu.VMEM((MLA_HEADS, 1, ATT_TQ), F32),
                        pltpu.VMEM((MLA_HEADS, MLA_V + ATT_ONES, ATT_TQ), F32)],
        compiler_params=_params(2),
        name="mla_attention",
    )(qt4, k3, vt4)


def _merge_kernel(x_ref, ya_ref, yb_ref, yc_ref, yd_ref, wg_ref, wb_ref, wo_ref, g_ref, b_ref, o_ref):
    def residual(rows):
        x = x_ref[rows, :]
        xb = x.astype(BF16)
        merged = None
        for n, y_ref in enumerate((ya_ref, yb_ref, yc_ref, yd_ref)):
            gate = jax.nn.sigmoid(_dot(xb, wg_ref[:, n * D_MODEL:(n + 1) * D_MODEL]))
            term = gate * _dot(y_ref[rows, :], wb_ref[n * BRANCH_WIDTH:(n + 1) * BRANCH_WIDTH, :])
            merged = term if merged is None else merged + term
        return ALPHA * x + _dot(merged.astype(BF16), wo_ref[...])

    _pipelined_rows(o_ref, g_ref, b_ref, residual)


def _merge(layer, x, ys, wg, wb, wo, g, b, casts=()):
    n = x.shape[0]
    steps = n // TM_WIDE
    row = lambda last: pl.BlockSpec((TM_WIDE, last), lambda i: (i, 0))
    cast_in, cast_out, cast_shapes = _cast_specs(casts, steps, lambda i: i)
    return pl.pallas_call(
        _with_casts(_merge_kernel, 10, 1, casts, steps),
        grid=(steps,),
        in_specs=[row(D_MODEL)] + [row(BRANCH_WIDTH)] * N_BRANCH
                 + [_const_spec(a.shape) for a in (wg, wb, wo)]
                 + [_layer_spec(a, layer) for a in (g, b)] + cast_in,
        out_specs=[row(D_MODEL)] + cast_out,
        out_shape=[jax.ShapeDtypeStruct((n, D_MODEL), F32)] + cast_shapes,
        compiler_params=_params(1),
        name="merge_ln",
    )(x, *ys, wg, wb, wo, g, b, *_cast_operands(casts))


def _rope_tables(positions):
    n_r, n_m = RET_KDIM // 2, MLA_ROPE // 2
    inv_r = ROPE_BASE ** (-jnp.arange(0, RET_KDIM, 2, dtype=F32) / RET_KDIM)
    inv_m = ROPE_BASE ** (-jnp.arange(0, MLA_ROPE, 2, dtype=F32) / MLA_ROPE)
    pad = jnp.zeros((HEAD_PAD - 2 * (n_r + n_m),), F32)
    ang = positions.astype(F32)[..., None] * jnp.concatenate([inv_r, inv_r, inv_m, inv_m, pad])
    lane = np.arange(HEAD_PAD)
    is_cos = jnp.asarray((lane < n_r) | ((lane >= 2 * n_r) & (lane < 2 * n_r + n_m)))
    compact = jnp.where(is_cos, jnp.cos(ang), jnp.sin(ang))
    expand = np.zeros((HEAD_PAD, 4 * HEAD_PAD), np.float32)
    for l in range(HEAD_PAD):
        expand[l % n_r, l] = 1.0
        expand[n_r + l % n_r, HEAD_PAD + l] = -1.0 if l % RET_KDIM < n_r else 1.0
        if l < MLA_ROPE:
            expand[2 * n_r + l % n_m, 2 * HEAD_PAD + l] = 1.0
            expand[2 * n_r + n_m + l % n_m, 3 * HEAD_PAD + l] = -1.0 if l < n_m else 1.0
    return compact, jnp.asarray(np.tile(expand, (3, 1)), BF16)


def _swap_cols(w):
    half = w.shape[-1] // 2
    return jnp.concatenate([w[..., half:], w[..., :half]], axis=-1)


def _mla_weights(w_uq, w_ukv):
    depth = w_uq.shape[0]
    uq = w_uq.reshape(depth, MLA_Q_RANK, MLA_HEADS, MLA_QK)
    q_nope, q_rope = uq[..., :MLA_NOPE], uq[..., MLA_NOPE:]
    zq = jnp.zeros((depth, MLA_Q_RANK, MLA_HEADS, HEAD_PAD - MLA_QK), w_uq.dtype)
    wq1 = jnp.concatenate([q_nope, q_rope, zq], axis=-1)
    wq2 = jnp.concatenate([jnp.zeros_like(q_nope), _swap_cols(q_rope), zq], axis=-1)
    ukv = w_ukv.reshape(depth, MLA_KV_RANK, MLA_HEADS, MLA_NOPE + MLA_V)
    k_nope, v = ukv[..., :MLA_NOPE], ukv[..., MLA_NOPE:]
    wk = jnp.concatenate([k_nope, jnp.zeros_like(k_nope)], axis=-1)
    wvt = jnp.swapaxes(v.reshape(depth, MLA_KV_RANK, MLA_HEADS * MLA_V), 1, 2)
    flat = lambda a: a.reshape(depth, a.shape[1], MLA_HEADS * HEAD_PAD).astype(BF16)
    return flat(wq1), flat(wq2), flat(wk), wvt.astype(BF16)


def kernel(x, p, positions, ffn1_up, ffn1_down, ln1_g, ln1_b, w_in, sgu_ln_g, sgu_ln_b, sgu_w, sgu_b,
           pool_w, pool_scale, mla_q_norm, mla_kv_norm, mla_w_uq, mla_w_ukv, w_branch, w_out,
           ln2_g, ln2_b, ffn2_up, ffn2_down, w_ple_gate, w_ple, ln3_g, ln3_b):
    bsz, seq, dm = x.shape
    n = bsz * seq
    rope = _rope_tables(positions)
    bf = lambda a: a.astype(BF16)
    rows = lambda a: a[:, None, :]

    up1, down1 = bf(ffn1_up[0]), bf(ffn1_down[0])
    w_in_t = jnp.swapaxes(w_in, 1, 2)
    w_branch2 = w_branch.reshape(DEPTH, N_BRANCH * BRANCH_WIDTH, dm)
    wp = bf(w_ple)
    wq1, wq2, wk, wvt = _mla_weights(mla_w_uq, mla_w_ukv)
    sgu_bias = jnp.repeat(jnp.swapaxes(sgu_b, 1, 2), SGU_WIDTH // SGU_GROUPS, axis=2)
    groups = len(POOL_WINDOWS)
    pool_bd = bf(jnp.einsum("lgcd,gh->lgchd", pool_w, jnp.eye(groups, dtype=pool_w.dtype))
                 .reshape(DEPTH, BRANCH_WIDTH, BRANCH_WIDTH))
    p2 = p.reshape(DEPTH, n, PLE_DIM)

    h = x.reshape(n, dm)
    for i in range(DEPTH):
        last = i + 1 == DEPTH
        h, w_mix, w_gate, wb, wo = _ffn_ln(
            i, h, up1, down1, rows(ln1_g), rows(ln1_b),
            (_cast(w_in_t, i, MIX_COLS), _CastT(w_in_t, i, _OFF_GATE, N_BRANCH * dm), _cast(w_branch2, i),
             _cast(w_out, i)))
        y_a, y_b, y_c, q, k, vt, up2, wpg, *nxt_up = _mixers(
            i, h.reshape(bsz, seq, dm), w_mix, rope,
            (rows(sgu_ln_g), rows(sgu_ln_b), sgu_w, sgu_bias), (pool_bd, rows(pool_scale)),
            (rows(mla_q_norm), rows(mla_kv_norm), wq1, wq2, wk, wvt),
            (_cast(ffn2_up, i), _cast(w_ple_gate, i)) + (() if last else (_cast(ffn1_up, i + 1),)))
        y_d = _attention(q, k, vt)
        ys = tuple(y.reshape(n, BRANCH_WIDTH) for y in (y_a, y_b, y_c, y_d))
        h, down2, *nxt_down = _merge(
            i, h, ys, w_gate, wb, wo, rows(ln2_g), rows(ln2_b),
            (_cast(ffn2_down, i),) + (() if last else (_cast(ffn1_down, i + 1),)))
        h = _ffn_ple_ln(i, h, p2, up2, down2, wpg, wp, rows(ln3_g), rows(ln3_b))
        if not last:
            (up1,), (down1,) = nxt_up, nxt_down
    return h.reshape(bsz, seq, dm)
```

```python
import math
from typing import NamedTuple

import numpy as np
import jax
import jax.numpy as jnp
from jax import lax
from jax.experimental import pallas as pl
from jax.experimental.pallas import tpu as pltpu

D_MODEL = 1024
DEPTH = 2
CHUNK = 64
SGU_WIDTH = 256
SGU_BLOCK = 128
SGU_GROUPS = 4
RET_HEADS = 4
RET_KDIM = 64
POOL_WINDOWS = (2, 4, 8, 16)
POOL_HALO = 16
POOL_PAD = 32
MLA_HEADS = 4
MLA_NOPE = 64
MLA_ROPE = 32
MLA_QK = MLA_NOPE + MLA_ROPE
MLA_V = 64
MLA_Q_RANK = 256
MLA_KV_RANK = 128
HEAD_PAD = 128
ROPE_BASE = 10000.0
N_BRANCH = 4
BRANCH_WIDTH = 256
D_FF = 2816
PLE_DIM = 256
ALPHA = (2 * DEPTH) ** 0.25
LN_EPS = 1e-5
RMS_EPS = 1e-6
GN_EPS = 1e-5

_OFF_SGU = 0
_OFF_RET = 512
_OFF_POOL = 1536
_OFF_CQ = 1792
_OFF_GATE = 2208
MIX_COLS = 2304

TM = 512
TM_WIDE = 1024
SUB = 256
RET_TILE = 256
ATT_TQ = 512
ATT_TK = 256
ATT_ONES = 16
VMEM_LIMIT = 56 * 1024 * 1024

BF16 = jnp.bfloat16
F32 = jnp.float32


def _dot(a, b):
    return jnp.dot(a, b, preferred_element_type=F32)


def _dot_nt(a, b):
    return lax.dot_general(a, b, (((1,), (1,)), ((), ())), preferred_element_type=F32)


def _dot_tn(a, b):
    return lax.dot_general(a, b, (((0,), (0,)), ((), ())), preferred_element_type=F32)


def _layer_norm(r, g, b, eps):
    mu = jnp.mean(r, axis=-1, keepdims=True)
    d = r - mu
    var = jnp.mean(d * d, axis=-1, keepdims=True)
    return d * lax.rsqrt(var + eps) * g + b


def _const_spec(shape):
    zeros = (0,) * len(shape)
    return pl.BlockSpec(shape, lambda *_: zeros, pipeline_mode=pl.Buffered(1))


def _layer_spec(arr, layer):
    index = (layer,) + (0,) * (arr.ndim - 1)
    return pl.BlockSpec((None,) + arr.shape[1:], lambda *_: index, pipeline_mode=pl.Buffered(1))


def _params(n_grid, semantics="parallel"):
    return pltpu.CompilerParams(
        dimension_semantics=(semantics,) * n_grid, vmem_limit_bytes=VMEM_LIMIT)


class _Cast(NamedTuple):
    src: jax.Array
    layer: int
    rows: int


class _CastT(NamedTuple):
    src: jax.Array
    layer: int
    row0: int
    rows: int


def _cast(src, layer, rows=None):
    return _Cast(src, layer, src.shape[1] if rows is None else rows)


def _cast_specs(casts, n_steps, step_of):
    in_specs, out_specs, out_shapes = [], [], []
    for c in casts:
        blk, cols = c.rows // n_steps, c.src.shape[2]
        if isinstance(c, _Cast):
            in_specs.append(pl.BlockSpec((None, blk, cols), lambda *g, c=c: (c.layer, step_of(*g), 0)))
            out_specs.append(pl.BlockSpec((blk, cols), lambda *g: (step_of(*g), 0)))
            out_shapes.append(jax.ShapeDtypeStruct((c.rows, cols), BF16))
        else:
            first = c.row0 // blk
            in_specs += [pl.BlockSpec((None, blk, cols),
                                      lambda *g, c=c, k=k: (c.layer, first + step_of(*g) + k, 0))
                         for k in range(2)]
            out_specs.append(pl.BlockSpec((cols, blk), lambda *g: (0, step_of(*g))))
            out_shapes.append(jax.ShapeDtypeStruct((cols, c.rows), BF16))
    return in_specs, out_specs, out_shapes


def _cast_operands(casts):
    return [c.src for c in casts for _ in range(1 if isinstance(c, _Cast) else 2)]


def _with_casts(body, n_in, n_out, casts, n_steps):
    n_src = sum(1 if isinstance(c, _Cast) else 2 for c in casts)

    def kernel(*refs):
        ins, rest = refs[:n_in], refs[n_in:]
        cast_in, rest = list(rest[:n_src]), rest[n_src:]
        outs, rest = rest[:n_out], rest[n_out:]
        cast_out, scratch = rest[:len(casts)], rest[len(casts):]
        body(*ins, *outs, *scratch)
        for c, dst_ref in zip(casts, cast_out):
            if isinstance(c, _Cast):
                dst_ref[...] = cast_in.pop(0)[...].astype(BF16)
            else:
                lo_ref, hi_ref = cast_in.pop(0), cast_in.pop(0)
                off = c.row0 % (c.rows // n_steps)
                window = jnp.concatenate([lo_ref[off:, :], hi_ref[:off, :]], axis=0)
                dst_ref[...] = window.T.astype(BF16)

    return kernel


def _pipelined_rows(o_ref, g_ref, b_ref, residual_fn):
    pending = None
    for s in range(o_ref.shape[0] // SUB):
        rows = slice(s * SUB, (s + 1) * SUB)
        r = residual_fn(rows)
        if pending is not None:
            o_ref[pending[0], :] = _layer_norm(pending[1], g_ref[...], b_ref[...], LN_EPS)
        pending = (rows, r)
    o_ref[pending[0], :] = _layer_norm(pending[1], g_ref[...], b_ref[...], LN_EPS)


def _ffn_body(x, up_ref, down_ref):
    xb = x.astype(BF16)
    a = _dot(xb, up_ref[:, :D_FF])
    b = _dot(xb, up_ref[:, D_FF:])
    h = (a * jax.nn.sigmoid(a) * b).astype(BF16)
    return xb, _dot(h, down_ref[...])


def _ffn_ln_kernel(x_ref, up_ref, down_ref, g_ref, b_ref, o_ref):
    def residual(rows):
        x = x_ref[rows, :]
        _, y = _ffn_body(x, up_ref, down_ref)
        return ALPHA * x + 0.5 * y

    _pipelined_rows(o_ref, g_ref, b_ref, residual)


def _ffn_ple_ln_kernel(x_ref, p_ref, up_ref, down_ref, wpg_ref, wp_ref, g_ref, b_ref, o_ref):
    def residual(rows):
        x = x_ref[rows, :]
        xb, y = _ffn_body(x, up_ref, down_ref)
        ple = jax.nn.sigmoid(_dot(xb, wpg_ref[...])) * _dot(p_ref[rows, :].astype(BF16), wp_ref[...])
        return ALPHA * x + 0.5 * y + ple

    _pipelined_rows(o_ref, g_ref, b_ref, residual)


def _ffn_ln(layer, x, up, down, g, b, casts=()):
    n = x.shape[0]
    steps = n // TM_WIDE
    row = pl.BlockSpec((TM_WIDE, D_MODEL), lambda i: (i, 0))
    cast_in, cast_out, cast_shapes = _cast_specs(casts, steps, lambda i: i)
    return pl.pallas_call(
        _with_casts(_ffn_ln_kernel, 5, 1, casts, steps),
        grid=(steps,),
        in_specs=[row, _const_spec(up.shape), _const_spec(down.shape), _layer_spec(g, layer),
                  _layer_spec(b, layer)] + cast_in,
        out_specs=[row] + cast_out,
        out_shape=[jax.ShapeDtypeStruct((n, D_MODEL), F32)] + cast_shapes,
        compiler_params=_params(1),
        name="ffn_ln",
    )(x, up, down, g, b, *_cast_operands(casts))


def _ffn_ple_ln(layer, x, p, up, down, wpg, wp, g, b):
    n = x.shape[0]
    row = pl.BlockSpec((TM_WIDE, D_MODEL), lambda i: (i, 0))
    return pl.pallas_call(
        _ffn_ple_ln_kernel,
        grid=(n // TM_WIDE,),
        in_specs=[row, pl.BlockSpec((None, TM_WIDE, PLE_DIM), lambda i: (layer, i, 0)),
                  _const_spec(up.shape), _const_spec(down.shape), _const_spec(wpg.shape)]
                 + [_layer_spec(a, layer) for a in (wp, g, b)],
        out_specs=row,
        out_shape=jax.ShapeDtypeStruct((n, D_MODEL), F32),
        compiler_params=_params(1),
        name="ffn_ple_ln",
    )(x, p, up, down, wpg, wp, g, b)


def _swap_halves(x, half):
    width = x.shape[-1]
    lane = lax.broadcasted_iota(jnp.int32, x.shape, x.ndim - 1)
    first = (lane % (2 * half)) < half
    return jnp.where(first, pltpu.roll(x, width - half, x.ndim - 1), pltpu.roll(x, half, x.ndim - 1))


def _sgu_part(uv, lng_ref, lnb_ref, ws_ref, bias_ref, o_ref):
    u = jax.nn.gelu(uv[:, :SGU_WIDTH])
    v = _layer_norm(jax.nn.gelu(uv[:, SGU_WIDTH:]), lng_ref[...], lnb_ref[...], LN_EPS)
    vb = v.astype(BF16)
    t_row = lax.broadcasted_iota(jnp.int32, (SGU_BLOCK, SGU_BLOCK), 0)
    t_col = lax.broadcasted_iota(jnp.int32, (SGU_BLOCK, SGU_BLOCK), 1)
    causal = t_row >= t_col
    w_groups = [jnp.where(causal, ws_ref[g], 0.0).astype(BF16) for g in range(SGU_GROUPS)]
    group = lax.broadcasted_iota(jnp.int32, (SGU_BLOCK, SGU_WIDTH), 1) // (SGU_WIDTH // SGU_GROUPS)
    bias = bias_ref[...]
    for blk in range(TM // SGU_BLOCK):
        rows = slice(blk * SGU_BLOCK, (blk + 1) * SGU_BLOCK)
        v_blk = vb[rows]
        mixed = bias
        for g in range(SGU_GROUPS):
            mixed = mixed + jnp.where(group == g, _dot(w_groups[g], v_blk), 0.0)
        o_ref[rows, :] = (u[rows] * mixed).astype(o_ref.dtype)


def _ret_part(proj, cos, sin, dmask_ref, rowdec_ref, keydec_ref, tiledec_ref, o_ref, state_ref):
    width = RET_HEADS * RET_KDIM
    q = proj[:, :width]
    k = proj[:, width:2 * width]
    q = q * cos + _swap_halves(q, RET_KDIM // 2) * sin
    k = (k * cos + _swap_halves(k, RET_KDIM // 2) * sin) * RET_KDIM ** -0.5
    v = proj[:, 2 * width:3 * width]
    gate = proj[:, 3 * width:]
    vb = v.astype(BF16)
    kb = k.astype(BF16)

    head = lax.broadcasted_iota(jnp.int32, (RET_TILE, width), 1) // RET_KDIM
    y = _dot(q.astype(BF16), state_ref[...].astype(BF16)) * rowdec_ref[...]
    for h in range(RET_HEADS):
        qh = jnp.where(head == h, q, 0.0).astype(BF16)
        scores = _dot_nt(qh, kb) * dmask_ref[h]
        y = y + jnp.where(head == h, _dot(scores.astype(BF16), vb), 0.0)

    kd = (k * keydec_ref[...]).astype(BF16)
    row_head = lax.broadcasted_iota(jnp.int32, (width, width), 0) // RET_KDIM
    col_head = lax.broadcasted_iota(jnp.int32, (width, width), 1) // RET_KDIM
    kv = jnp.where(row_head == col_head, _dot_tn(kd, vb), 0.0)
    state_ref[...] = state_ref[...] * tiledec_ref[...] + kv

    inv = 1.0 / RET_KDIM
    mu = jnp.zeros_like(y)
    for h in range(RET_HEADS):
        s = jnp.sum(jnp.where(head == h, y, 0.0), axis=1, keepdims=True) * inv
        mu = jnp.where(head == h, s, mu)
    d = y - mu
    var = jnp.zeros_like(y)
    for h in range(RET_HEADS):
        s = jnp.sum(jnp.where(head == h, d * d, 0.0), axis=1, keepdims=True) * inv
        var = jnp.where(head == h, s, var)
    yn = d * lax.rsqrt(var + GN_EPS)
    o_ref[...] = (gate * jax.nn.sigmoid(gate) * yn).astype(o_ref.dtype)


def _retention_tables():
    heads = np.arange(RET_HEADS, dtype=np.float64)
    log_gamma = np.log1p(-np.exp2(-5.0 - heads))
    t = np.arange(RET_TILE)
    chunk = t // CHUNK
    diff = (t[:, None] - t[None, :]).astype(np.float64)
    same = chunk[:, None] == chunk[None, :]
    earlier = chunk[None, :] < chunk[:, None]
    expo = np.where(same, np.abs(diff), diff)
    dmask = np.where((same | earlier)[None], np.exp(log_gamma[:, None, None] * expo[None]), 0.0)
    rowdec = np.exp(log_gamma[None, :] * (t[:, None] + 1.0))
    keydec = np.exp(log_gamma[None, :] * (RET_TILE - 1.0 - t[:, None]))
    tiledec = np.exp(log_gamma * RET_TILE)
    rep = lambda a: np.repeat(a, RET_KDIM, axis=-1)
    width = RET_HEADS * RET_KDIM
    tiledec_full = np.broadcast_to(rep(tiledec[None, :]).T, (width, width))
    return (jnp.asarray(dmask, F32), jnp.asarray(rep(rowdec), F32), jnp.asarray(rep(keydec), F32),
            jnp.asarray(tiledec_full, F32))


def _pool_part(z, wp_ref, scale_ref, o_ref, pool_ref, seq_tile):
    n_rows = POOL_PAD + TM
    pool_ref[0, POOL_PAD:, :] = z
    sums = []
    for level in range(len(POOL_WINDOWS)):
        lo, shift = 8 * (level + 1), 2 ** level
        s = pool_ref[level, lo:n_rows, :] + pool_ref[level, lo - shift:n_rows - shift, :]
        if level + 1 < len(POOL_WINDOWS):
            pool_ref[level + 1, lo:n_rows, :] = s
        sums.append(s[POOL_PAD - lo:])
    group = lax.broadcasted_iota(jnp.int32, (TM, BRANCH_WIDTH), 1) // (BRANCH_WIDTH // len(POOL_WINDOWS))
    win = sums[-1]
    for gi in range(len(POOL_WINDOWS) - 1):
        win = jnp.where(group == gi, sums[gi], win)
    t = seq_tile * TM + lax.broadcasted_iota(jnp.int32, (TM, BRANCH_WIDTH), 0)
    count = jnp.minimum(t + 1, jnp.left_shift(2, group)).astype(F32)
    pooled = win / count - z
    y = _dot(pooled.astype(BF16), wp_ref[...]) * scale_ref[...]
    o_ref[...] = y.astype(o_ref.dtype)
    pool_ref[0, POOL_PAD - POOL_HALO:POOL_PAD, :] = pool_ref[0, n_rows - POOL_HALO:n_rows, :]


def _rms_norm(x, g):
    return x * lax.rsqrt(jnp.mean(x * x, axis=-1, keepdims=True) + RMS_EPS) * g


def _mla_part(proj, cos, sin, qg_ref, kvg_ref, wq1_ref, wq2_ref, wk_ref, wvt_ref, qt_ref, k_ref, vt_ref):
    cq = _rms_norm(proj[:, :MLA_Q_RANK], qg_ref[...]).astype(BF16)
    ckv = _rms_norm(proj[:, MLA_Q_RANK:MLA_Q_RANK + MLA_KV_RANK], kvg_ref[...]).astype(BF16)
    k_raw = proj[:, MLA_Q_RANK + MLA_KV_RANK:]
    k_pe = pltpu.roll(k_raw * cos + _swap_halves(k_raw, MLA_ROPE // 2) * sin, MLA_NOPE, 1)
    ck = pltpu.roll(cos, MLA_NOPE, 1)
    sk = pltpu.roll(sin, MLA_NOPE, 1)
    lane = lax.broadcasted_iota(jnp.int32, ck.shape, 1)
    scale = MLA_QK ** -0.5 * math.log2(math.e)
    cq_tab = (ck + jnp.where(lane < MLA_NOPE, 1.0, 0.0)) * scale
    sq_tab = sk * scale
    tile4 = lambda a: jnp.concatenate([a] * MLA_HEADS, axis=1)
    q = _dot(cq, wq1_ref[...]) * tile4(cq_tab) + _dot(cq, wq2_ref[...]) * tile4(sq_tab)
    k = _dot(ckv, wk_ref[...]) + tile4(k_pe)
    for h in range(MLA_HEADS):
        rows = slice(h * HEAD_PAD, (h + 1) * HEAD_PAD)
        qt_ref[rows, :] = q[:, rows].T.astype(qt_ref.dtype)
    k_ref[...] = k.astype(k_ref.dtype)
    for t in range(TM // ATT_TK):
        vt_ref[t] = _dot_nt(wvt_ref[...], ckv[t * ATT_TK:(t + 1) * ATT_TK]).astype(vt_ref.dtype)


def _mixers_kernel(x_ref, w_ref, cs_ref, expand_ref,
                   lng_ref, lnb_ref, ws_ref, bias_ref, wp_ref, scale_ref,
                   qg_ref, kvg_ref, wq1_ref, wq2_ref, wk_ref, wvt_ref,
                   dmask_ref, rowdec_ref, keydec_ref, tiledec_ref,
                   ya_ref, yb_ref, yc_ref, qt_ref, k_ref, vt_ref, state_ref, pool_ref):
    seq_tile = pl.program_id(1)

    @pl.when(seq_tile == 0)
    def _():
        state_ref[...] = jnp.zeros_like(state_ref)
        pool_ref[0, 0:POOL_PAD, :] = jnp.zeros((POOL_PAD, BRANCH_WIDTH), F32)

    xb = x_ref[...].astype(BF16)
    p_sgu = _dot_nt(xb, w_ref[_OFF_SGU:_OFF_RET, :])
    p_pool = _dot_nt(xb, w_ref[_OFF_POOL:_OFF_CQ, :])
    p_mla = _dot_nt(xb, w_ref[_OFF_CQ:MIX_COLS, :])
    p_ret = _dot_nt(xb, w_ref[_OFF_RET:_OFF_POOL, :])
    cs = cs_ref[...]
    hi = cs.astype(BF16)
    rest = cs - hi.astype(F32)
    mid = rest.astype(BF16)
    lo = (rest - mid.astype(F32)).astype(BF16)
    rope = _dot(jnp.concatenate([hi, mid, lo], axis=1), expand_ref[...])
    ret_cos, ret_sin, mla_cos, mla_sin = (rope[:, t * HEAD_PAD:(t + 1) * HEAD_PAD] for t in range(4))
    _sgu_part(p_sgu, lng_ref, lnb_ref, ws_ref, bias_ref, ya_ref)
    _pool_part(p_pool, wp_ref, scale_ref, yc_ref, pool_ref, seq_tile)
    _mla_part(p_mla, mla_cos, mla_sin, qg_ref, kvg_ref, wq1_ref, wq2_ref, wk_ref, wvt_ref,
              qt_ref, k_ref, vt_ref)
    for r in range(TM // RET_TILE):
        rows = slice(r * RET_TILE, (r + 1) * RET_TILE)
        cos = jnp.concatenate([ret_cos[rows], ret_cos[rows]], axis=1)
        sin = jnp.concatenate([ret_sin[rows], ret_sin[rows]], axis=1)
        _ret_part(p_ret[rows], cos, sin, dmask_ref, rowdec_ref, keydec_ref, tiledec_ref,
                  yb_ref.at[rows, :], state_ref)


def _mixers(layer, x3, w_mix, rope, sgu_params, pool_params, mla_params, casts=()):
    bsz, seq, _ = x3.shape
    width = RET_HEADS * RET_KDIM
    ret_tables = _retention_tables()
    seq_tiles = seq // TM
    tile = lambda last: pl.BlockSpec((None, TM, last), lambda b, s: (b, s, 0))
    layered = tuple(sgu_params) + tuple(pool_params) + tuple(mla_params)
    in_specs = ([tile(D_MODEL), _const_spec(w_mix.shape), tile(HEAD_PAD), _const_spec(rope[1].shape)]
                + [_layer_spec(a, layer) for a in layered]
                + [_const_spec(a.shape) for a in ret_tables])
    cast_in, cast_out, cast_shapes = _cast_specs(casts, bsz * seq_tiles, lambda b, s: b * seq_tiles + s)
    branch = jax.ShapeDtypeStruct((bsz, seq, BRANCH_WIDTH), BF16)
    assert TM == ATT_TQ
    qt = jax.ShapeDtypeStruct((bsz, seq_tiles, MLA_HEADS * HEAD_PAD, TM), BF16)
    kk = jax.ShapeDtypeStruct((bsz, seq, MLA_HEADS * HEAD_PAD), BF16)
    vt = jax.ShapeDtypeStruct((bsz, seq // ATT_TK, MLA_HEADS * MLA_V, ATT_TK), BF16)
    return pl.pallas_call(
        _with_casts(_mixers_kernel, len(in_specs), 6, casts, bsz * seq_tiles),
        grid=(bsz, seq_tiles),
        in_specs=in_specs + cast_in,
        out_specs=[tile(BRANCH_WIDTH)] * 3
                  + [pl.BlockSpec((None, None, MLA_HEADS * HEAD_PAD, TM), lambda b, s: (b, s, 0, 0)),
                     tile(MLA_HEADS * HEAD_PAD),
                     pl.BlockSpec((None, TM // ATT_TK, MLA_HEADS * MLA_V, ATT_TK), lambda b, s: (b, s, 0, 0))]
                  + cast_out,
        out_shape=[branch, branch, branch, qt, kk, vt] + cast_shapes,
        scratch_shapes=[pltpu.VMEM((width, width), F32),
                        pltpu.VMEM((len(POOL_WINDOWS), POOL_PAD + TM, BRANCH_WIDTH), F32)],
        compiler_params=_params(2, "arbitrary"),
        name="token_mixers",
    )(x3, w_mix, *rope, *sgu_params, *pool_params, *mla_params, *ret_tables, *_cast_operands(casts))


def _attn_kernel(qt_ref, k_ref, vt_ref, o_ref, st_ref, m_ref, acc_ref):
    key_chunk = lax.broadcasted_iota(jnp.int32, (ATT_TK, ATT_TQ), 0) // CHUNK
    qry_chunk = lax.broadcasted_iota(jnp.int32, (ATT_TK, ATT_TQ), 1) // CHUNK
    diag_masks = [key_chunk + t * (ATT_TK // CHUNK) <= qry_chunk for t in range(ATT_TQ // ATT_TK)]
    heads = range(MLA_HEADS)
    every_query = slice(0, ATT_TQ)
    upper = slice(ATT_TK, ATT_TQ)
    ones = jnp.ones((ATT_ONES, ATT_TK), BF16)

    def query_tile(qi, carry):
        def scores(j, slot, queries=every_query, hs=heads):
            rows = pl.ds(pl.multiple_of(j * ATT_TK, ATT_TK), ATT_TK)
            for h in hs:
                cols = slice(h * HEAD_PAD, (h + 1) * HEAD_PAD)
                st_ref[slot, h, :, queries] = _dot(k_ref[rows, cols], qt_ref[qi, cols, queries])

        def softmax_pv(j, slot, mask=None, queries=every_query, hs=heads):
            for h in hs:
                load = lambda: (st_ref[slot, h, :, queries] if mask is None
                                else jnp.where(mask[:, queries], st_ref[slot, h, :, queries], -jnp.inf))
                m = m_ref[h, :, queries]
                m_new = jnp.maximum(m, jnp.max(load(), axis=0, keepdims=True))
                p = jnp.exp2((load() - m_new).astype(BF16))
                lhs = jnp.concatenate([vt_ref[j, h * MLA_V:(h + 1) * MLA_V, :], ones], axis=0)
                acc_ref[h, :, queries] = jnp.exp2(m - m_new) * acc_ref[h, :, queries] + _dot(lhs, p)
                m_ref[h, :, queries] = m_new

        def pair(i, carry):
            for h in heads:
                scores(2 * i + 1, 1, hs=(h,))
                softmax_pv(2 * i, 0, hs=(h,))
            for h in heads:
                scores(2 * i + 2, 0, hs=(h,))
                softmax_pv(2 * i + 1, 1, hs=(h,))
            return carry

        m_ref[...] = jnp.full(m_ref.shape, -jnp.inf, F32)
        acc_ref[...] = jnp.zeros(acc_ref.shape, F32)
        scores(0, 0)
        lax.fori_loop(0, qi, pair, 0)
        for h in heads:
            scores(2 * qi + 1, 1, upper, hs=(h,))
            softmax_pv(2 * qi, 0, diag_masks[0], hs=(h,))
        softmax_pv(2 * qi + 1, 1, diag_masks[1], upper)
        out_t = jnp.concatenate([acc_ref[h, :MLA_V] / acc_ref[h, MLA_V:MLA_V + 1] for h in heads], axis=0)
        rows = pl.ds(pl.multiple_of(qi * ATT_TQ, ATT_TQ), ATT_TQ)
        o_ref[rows, :] = out_t.T.astype(o_ref.dtype)
        return carry

    lax.fori_loop(0, qt_ref.shape[0], query_tile, 0)


def _attention(qt4, k3, vt4):
    bsz, seq, _ = k3.shape
    whole = lambda a: pl.BlockSpec((None,) + a.shape[1:], lambda b: (b,) + (0,) * (a.ndim - 1))
    return pl.pallas_call(
        _attn_kernel,
        grid=(bsz,),
        in_specs=[whole(qt4), whole(k3), whole(vt4)],
        out_specs=pl.BlockSpec((None, seq, MLA_HEADS * MLA_V), lambda b: (b, 0, 0)),
        out_shape=jax.ShapeDtypeStruct((bsz, seq, MLA_HEADS * MLA_V), BF16),
        scratch_shapes=[pltpu.VMEM((2, MLA_HEADS, ATT_TK, ATT_TQ), F32),
                        pltpu.VMEM((MLA_HEADS, 1, ATT_TQ), F32),
                        pltpu.VMEM((MLA_HEADS, MLA_V + ATT_ONES, ATT_TQ), F32)],
        compiler_params=_params(1),
        name="mla_attention",
    )(qt4, k3, vt4)


def _merge_kernel(x_ref, ya_ref, yb_ref, yc_ref, yd_ref, wg_ref, wb_ref, wo_ref, g_ref, b_ref, o_ref):
    def residual(rows):
        x = x_ref[rows, :]
        xb = x.astype(BF16)
        merged = None
        for n, y_ref in enumerate((ya_ref, yb_ref, yc_ref, yd_ref)):
            gate = jax.nn.sigmoid(_dot(xb, wg_ref[:, n * D_MODEL:(n + 1) * D_MODEL]))
            term = gate * _dot(y_ref[rows, :], wb_ref[n * BRANCH_WIDTH:(n + 1) * BRANCH_WIDTH, :])
            merged = term if merged is None else merged + term
        return ALPHA * x + _dot(merged.astype(BF16), wo_ref[...])

    _pipelined_rows(o_ref, g_ref, b_ref, residual)


def _merge(layer, x, ys, wg, wb, wo, g, b, casts=()):
    n = x.shape[0]
    steps = n // TM_WIDE
    row = lambda last: pl.BlockSpec((TM_WIDE, last), lambda i: (i, 0))
    cast_in, cast_out, cast_shapes = _cast_specs(casts, steps, lambda i: i)
    return pl.pallas_call(
        _with_casts(_merge_kernel, 10, 1, casts, steps),
        grid=(steps,),
        in_specs=[row(D_MODEL)] + [row(BRANCH_WIDTH)] * N_BRANCH
                 + [_const_spec(a.shape) for a in (wg, wb, wo)]
                 + [_layer_spec(a, layer) for a in (g, b)] + cast_in,
        out_specs=[row(D_MODEL)] + cast_out,
        out_shape=[jax.ShapeDtypeStruct((n, D_MODEL), F32)] + cast_shapes,
        compiler_params=_params(1),
        name="merge_ln",
    )(x, *ys, wg, wb, wo, g, b, *_cast_operands(casts))


def _rope_tables(positions):
    n_r, n_m = RET_KDIM // 2, MLA_ROPE // 2
    inv_r = ROPE_BASE ** (-jnp.arange(0, RET_KDIM, 2, dtype=F32) / RET_KDIM)
    inv_m = ROPE_BASE ** (-jnp.arange(0, MLA_ROPE, 2, dtype=F32) / MLA_ROPE)
    pad = jnp.zeros((HEAD_PAD - 2 * (n_r + n_m),), F32)
    ang = positions.astype(F32)[..., None] * jnp.concatenate([inv_r, inv_r, inv_m, inv_m, pad])
    lane = np.arange(HEAD_PAD)
    is_cos = jnp.asarray((lane < n_r) | ((lane >= 2 * n_r) & (lane < 2 * n_r + n_m)))
    compact = jnp.where(is_cos, jnp.cos(ang), jnp.sin(ang))
    expand = np.zeros((HEAD_PAD, 4 * HEAD_PAD), np.float32)
    for l in range(HEAD_PAD):
        expand[l % n_r, l] = 1.0
        expand[n_r + l % n_r, HEAD_PAD + l] = -1.0 if l % RET_KDIM < n_r else 1.0
        if l < MLA_ROPE:
            expand[2 * n_r + l % n_m, 2 * HEAD_PAD + l] = 1.0
            expand[2 * n_r + n_m + l % n_m, 3 * HEAD_PAD + l] = -1.0 if l < n_m else 1.0
    return compact, jnp.asarray(np.tile(expand, (3, 1)), BF16)


def _swap_cols(w):
    half = w.shape[-1] // 2
    return jnp.concatenate([w[..., half:], w[..., :half]], axis=-1)


def _mla_weights(w_uq, w_ukv):
    depth = w_uq.shape[0]
    uq = w_uq.reshape(depth, MLA_Q_RANK, MLA_HEADS, MLA_QK)
    q_nope, q_rope = uq[..., :MLA_NOPE], uq[..., MLA_NOPE:]
    zq = jnp.zeros((depth, MLA_Q_RANK, MLA_HEADS, HEAD_PAD - MLA_QK), w_uq.dtype)
    wq1 = jnp.concatenate([q_nope, q_rope, zq], axis=-1)
    wq2 = jnp.concatenate([jnp.zeros_like(q_nope), _swap_cols(q_rope), zq], axis=-1)
    ukv = w_ukv.reshape(depth, MLA_KV_RANK, MLA_HEADS, MLA_NOPE + MLA_V)
    k_nope, v = ukv[..., :MLA_NOPE], ukv[..., MLA_NOPE:]
    wk = jnp.concatenate([k_nope, jnp.zeros_like(k_nope)], axis=-1)
    wvt = jnp.swapaxes(v.reshape(depth, MLA_KV_RANK, MLA_HEADS * MLA_V), 1, 2)
    flat = lambda a: a.reshape(depth, a.shape[1], MLA_HEADS * HEAD_PAD).astype(BF16)
    return flat(wq1), flat(wq2), flat(wk), wvt.astype(BF16)


def kernel(x, p, positions, ffn1_up, ffn1_down, ln1_g, ln1_b, w_in, sgu_ln_g, sgu_ln_b, sgu_w, sgu_b,
           pool_w, pool_scale, mla_q_norm, mla_kv_norm, mla_w_uq, mla_w_ukv, w_branch, w_out,
           ln2_g, ln2_b, ffn2_up, ffn2_down, w_ple_gate, w_ple, ln3_g, ln3_b):
    bsz, seq, dm = x.shape
    n = bsz * seq
    rope = _rope_tables(positions)
    bf = lambda a: a.astype(BF16)
    rows = lambda a: a[:, None, :]

    up1, down1 = bf(ffn1_up[0]), bf(ffn1_down[0])
    w_in_t = jnp.swapaxes(w_in, 1, 2)
    w_branch2 = w_branch.reshape(DEPTH, N_BRANCH * BRANCH_WIDTH, dm)
    wp = bf(w_ple)
    wq1, wq2, wk, wvt = _mla_weights(mla_w_uq, mla_w_ukv)
    sgu_bias = jnp.repeat(jnp.swapaxes(sgu_b, 1, 2), SGU_WIDTH // SGU_GROUPS, axis=2)
    groups = len(POOL_WINDOWS)
    pool_bd = bf(jnp.einsum("lgcd,gh->lgchd", pool_w, jnp.eye(groups, dtype=pool_w.dtype))
                 .reshape(DEPTH, BRANCH_WIDTH, BRANCH_WIDTH))
    p2 = p.reshape(DEPTH, n, PLE_DIM)

    h = x.reshape(n, dm)
    for i in range(DEPTH):
        last = i + 1 == DEPTH
        h, w_mix, w_gate, wb, wo = _ffn_ln(
            i, h, up1, down1, rows(ln1_g), rows(ln1_b),
            (_cast(w_in_t, i, MIX_COLS), _CastT(w_in_t, i, _OFF_GATE, N_BRANCH * dm), _cast(w_branch2, i),
             _cast(w_out, i)))
        y_a, y_b, y_c, q, k, vt, up2, wpg, *nxt_up = _mixers(
            i, h.reshape(bsz, seq, dm), w_mix, rope,
            (rows(sgu_ln_g), rows(sgu_ln_b), sgu_w, sgu_bias), (pool_bd, rows(pool_scale)),
            (rows(mla_q_norm), rows(mla_kv_norm), wq1, wq2, wk, wvt),
            (_cast(ffn2_up, i), _cast(w_ple_gate, i)) + (() if last else (_cast(ffn1_up, i + 1),)))
        y_d = _attention(q, k, vt)
        ys = tuple(y.reshape(n, BRANCH_WIDTH) for y in (y_a, y_b, y_c, y_d))
        h, down2, *nxt_down = _merge(
            i, h, ys, w_gate, wb, wo, rows(ln2_g), rows(ln2_b),
            (_cast(ffn2_down, i),) + (() if last else (_cast(ffn1_down, i + 1),)))
        h = _ffn_ple_ln(i, h, p2, up2, down2, wpg, wp, rows(ln3_g), rows(ln3_b))
        if not last:
            (up1,), (down1,) = nxt_up, nxt_down
    return h.reshape(bsz, seq, dm)
```

```python
import math
from typing import NamedTuple

import numpy as np
import jax
import jax.numpy as jnp
from jax import lax
from jax.experimental import pallas as pl
from jax.experimental.pallas import tpu as pltpu

D_MODEL = 1024
DEPTH = 2
CHUNK = 64
SGU_WIDTH = 256
SGU_BLOCK = 128
SGU_GROUPS = 4
RET_HEADS = 4
RET_KDIM = 64
POOL_WINDOWS = (2, 4, 8, 16)
POOL_HALO = 16
POOL_PAD = 32
MLA_HEADS = 4
MLA_NOPE = 64
MLA_ROPE = 32
MLA_QK = MLA_NOPE + MLA_ROPE
MLA_V = 64
MLA_Q_RANK = 256
MLA_KV_RANK = 128
HEAD_PAD = 128
F32_SUBLANES = 8
ROPE_BASE = 10000.0
N_BRANCH = 4
BRANCH_WIDTH = 256
D_FF = 2816
PLE_DIM = 256
ALPHA = (2 * DEPTH) ** 0.25
LN_EPS = 1e-5
RMS_EPS = 1e-6
GN_EPS = 1e-5

_OFF_SGU = 0
_OFF_RET = 512
_OFF_POOL = 1536
_OFF_CQ = 1792
_OFF_GATE = 2208
MIX_COLS = 2304

TM = 512
MIX_TILES = 2
TM_WIDE = 1024
SUB = 256
RET_TILE = 256
ATT_TQ = 512
ATT_TK = 256
ATT_ONES = 16
VMEM_LIMIT = 56 * 1024 * 1024

BF16 = jnp.bfloat16
F32 = jnp.float32


def _dot(a, b):
    return jnp.dot(a, b, preferred_element_type=F32)


def _dot_nt(a, b):
    return lax.dot_general(a, b, (((1,), (1,)), ((), ())), preferred_element_type=F32)


def _dot_tn(a, b):
    return lax.dot_general(a, b, (((0,), (0,)), ((), ())), preferred_element_type=F32)


def _layer_norm(r, g, b, eps):
    mu = jnp.mean(r, axis=-1, keepdims=True)
    d = r - mu
    var = jnp.mean(d * d, axis=-1, keepdims=True)
    return d * lax.rsqrt(var + eps) * g + b


def _const_spec(shape):
    zeros = (0,) * len(shape)
    return pl.BlockSpec(shape, lambda *_: zeros, pipeline_mode=pl.Buffered(1))


def _layer_spec(arr, layer):
    index = (layer,) + (0,) * (arr.ndim - 1)
    return pl.BlockSpec((None,) + arr.shape[1:], lambda *_: index, pipeline_mode=pl.Buffered(1))


def _params(n_grid, semantics="parallel"):
    return pltpu.CompilerParams(
        dimension_semantics=(semantics,) * n_grid, vmem_limit_bytes=VMEM_LIMIT)


class _Cast(NamedTuple):
    src: jax.Array
    layer: int
    rows: int


class _CastT(NamedTuple):
    src: jax.Array
    layer: int
    row0: int
    rows: int


def _cast(src, layer, rows=None):
    return _Cast(src, layer, src.shape[1] if rows is None else rows)


def _cast_specs(casts, n_steps, step_of):
    in_specs, out_specs, out_shapes = [], [], []
    for c in casts:
        blk, cols = c.rows // n_steps, c.src.shape[2]
        if isinstance(c, _Cast):
            in_specs.append(pl.BlockSpec((None, blk, cols), lambda *g, c=c: (c.layer, step_of(*g), 0)))
            out_specs.append(pl.BlockSpec((blk, cols), lambda *g: (step_of(*g), 0)))
            out_shapes.append(jax.ShapeDtypeStruct((c.rows, cols), BF16))
        else:
            first = c.row0 // blk
            in_specs += [pl.BlockSpec((None, blk, cols),
                                      lambda *g, c=c, k=k: (c.layer, first + step_of(*g) + k, 0))
                         for k in range(2)]
            out_specs.append(pl.BlockSpec((cols, blk), lambda *g: (0, step_of(*g))))
            out_shapes.append(jax.ShapeDtypeStruct((cols, c.rows), BF16))
    return in_specs, out_specs, out_shapes


def _cast_operands(casts):
    return [c.src for c in casts for _ in range(1 if isinstance(c, _Cast) else 2)]


def _with_casts(body, n_in, n_out, casts, n_steps):
    n_src = sum(1 if isinstance(c, _Cast) else 2 for c in casts)

    def kernel(*refs):
        ins, rest = refs[:n_in], refs[n_in:]
        cast_in, rest = list(rest[:n_src]), rest[n_src:]
        outs, rest = rest[:n_out], rest[n_out:]
        cast_out, scratch = rest[:len(casts)], rest[len(casts):]
        body(*ins, *outs, *scratch)
        for c, dst_ref in zip(casts, cast_out):
            if isinstance(c, _Cast):
                dst_ref[...] = cast_in.pop(0)[...].astype(BF16)
            else:
                lo_ref, hi_ref = cast_in.pop(0), cast_in.pop(0)
                off = c.row0 % (c.rows // n_steps)
                window = jnp.concatenate([lo_ref[off:, :], hi_ref[:off, :]], axis=0)
                dst_ref[...] = window.T.astype(BF16)

    return kernel


def _pipelined_rows(o_ref, g_ref, b_ref, residual_fn):
    pending = None
    for s in range(o_ref.shape[0] // SUB):
        rows = slice(s * SUB, (s + 1) * SUB)
        r = residual_fn(rows)
        if pending is not None:
            o_ref[pending[0], :] = _layer_norm(pending[1], g_ref[...], b_ref[...], LN_EPS)
        pending = (rows, r)
    o_ref[pending[0], :] = _layer_norm(pending[1], g_ref[...], b_ref[...], LN_EPS)


def _ffn_body(x, up_ref, down_ref):
    xb = x.astype(BF16)
    a = _dot(xb, up_ref[:, :D_FF])
    b = _dot(xb, up_ref[:, D_FF:])
    h = (a * jax.nn.sigmoid(a) * b).astype(BF16)
    return xb, _dot(h, down_ref[...])


def _ffn_ln_kernel(x_ref, up_ref, down_ref, g_ref, b_ref, o_ref):
    def residual(rows):
        x = x_ref[rows, :]
        _, y = _ffn_body(x, up_ref, down_ref)
        return ALPHA * x + 0.5 * y

    _pipelined_rows(o_ref, g_ref, b_ref, residual)


def _ffn_ple_ln_kernel(x_ref, p_ref, up_ref, down_ref, wpg_ref, wp_ref, g_ref, b_ref, o_ref):
    def residual(rows):
        x = x_ref[rows, :]
        xb, y = _ffn_body(x, up_ref, down_ref)
        ple = jax.nn.sigmoid(_dot(xb, wpg_ref[...])) * _dot(p_ref[rows, :].astype(BF16), wp_ref[...])
        return ALPHA * x + 0.5 * y + ple

    _pipelined_rows(o_ref, g_ref, b_ref, residual)


def _ffn_ln(layer, x, up, down, g, b, casts=()):
    n = x.shape[0]
    steps = n // TM_WIDE
    row = pl.BlockSpec((TM_WIDE, D_MODEL), lambda i: (i, 0))
    cast_in, cast_out, cast_shapes = _cast_specs(casts, steps, lambda i: i)
    return pl.pallas_call(
        _with_casts(_ffn_ln_kernel, 5, 1, casts, steps),
        grid=(steps,),
        in_specs=[row, _const_spec(up.shape), _const_spec(down.shape), _layer_spec(g, layer),
                  _layer_spec(b, layer)] + cast_in,
        out_specs=[row] + cast_out,
        out_shape=[jax.ShapeDtypeStruct((n, D_MODEL), F32)] + cast_shapes,
        compiler_params=_params(1),
        name="ffn_ln",
    )(x, up, down, g, b, *_cast_operands(casts))


def _ffn_ple_ln(layer, x, p, up, down, wpg, wp, g, b):
    n = x.shape[0]
    row = pl.BlockSpec((TM_WIDE, D_MODEL), lambda i: (i, 0))
    return pl.pallas_call(
        _ffn_ple_ln_kernel,
        grid=(n // TM_WIDE,),
        in_specs=[row, pl.BlockSpec((None, TM_WIDE, PLE_DIM), lambda i: (layer, i, 0)),
                  _const_spec(up.shape), _const_spec(down.shape), _const_spec(wpg.shape)]
                 + [_layer_spec(a, layer) for a in (wp, g, b)],
        out_specs=row,
        out_shape=jax.ShapeDtypeStruct((n, D_MODEL), F32),
        compiler_params=_params(1),
        name="ffn_ple_ln",
    )(x, p, up, down, wpg, wp, g, b)


def _swap_halves(x, half):
    width = x.shape[-1]
    lane = lax.broadcasted_iota(jnp.int32, x.shape, x.ndim - 1)
    first = (lane % (2 * half)) < half
    return jnp.where(first, pltpu.roll(x, width - half, x.ndim - 1), pltpu.roll(x, half, x.ndim - 1))


def _sgu_part(uv, lng_ref, lnb_ref, ws_ref, bias_ref, o_ref):
    u = jax.nn.gelu(uv[:, :SGU_WIDTH])
    v = _layer_norm(jax.nn.gelu(uv[:, SGU_WIDTH:]), lng_ref[...], lnb_ref[...], LN_EPS)
    vb = v.astype(BF16)
    t_row = lax.broadcasted_iota(jnp.int32, (SGU_BLOCK, SGU_BLOCK), 0)
    t_col = lax.broadcasted_iota(jnp.int32, (SGU_BLOCK, SGU_BLOCK), 1)
    causal = t_row >= t_col
    w_all = jnp.concatenate([jnp.where(causal, ws_ref[g], 0.0).astype(BF16) for g in range(SGU_GROUPS)], axis=1)
    group = lax.broadcasted_iota(jnp.int32, (SGU_BLOCK, SGU_WIDTH), 1) // (SGU_WIDTH // SGU_GROUPS)
    bias = bias_ref[...]
    for blk in range(TM // SGU_BLOCK):
        rows = slice(blk * SGU_BLOCK, (blk + 1) * SGU_BLOCK)
        v_blk = vb[rows]
        v_groups = jnp.concatenate([jnp.where(group == g, v_blk, 0.0) for g in range(SGU_GROUPS)], axis=0)
        o_ref[rows, :] = (u[rows] * (bias + _dot(w_all, v_groups))).astype(o_ref.dtype)


def _ret_part(proj, cos, sin, dmask_ref, rowdec_ref, keydec_ref, tiledec_ref, o_ref, state_ref):
    width = RET_HEADS * RET_KDIM
    q = proj[:, :width]
    k = proj[:, width:2 * width]
    q = q * cos + _swap_halves(q, RET_KDIM // 2) * sin
    k = (k * cos + _swap_halves(k, RET_KDIM // 2) * sin) * RET_KDIM ** -0.5
    v = proj[:, 2 * width:3 * width]
    gate = proj[:, 3 * width:]
    vb = v.astype(BF16)
    kb = k.astype(BF16)

    head = lax.broadcasted_iota(jnp.int32, (RET_TILE, width), 1) // RET_KDIM
    y = _dot(q.astype(BF16), state_ref[...].astype(BF16)) * rowdec_ref[...]
    for h in range(RET_HEADS):
        qh = jnp.where(head == h, q, 0.0).astype(BF16)
        scores = _dot_nt(qh, kb) * dmask_ref[h]
        y = y + jnp.where(head == h, _dot(scores.astype(BF16), vb), 0.0)

    kd = (k * keydec_ref[...]).astype(BF16)
    row_head = lax.broadcasted_iota(jnp.int32, (width, width), 0) // RET_KDIM
    col_head = lax.broadcasted_iota(jnp.int32, (width, width), 1) // RET_KDIM
    kv = jnp.where(row_head == col_head, _dot_tn(kd, vb), 0.0)
    state_ref[...] = state_ref[...] * tiledec_ref[...] + kv

    inv = 1.0 / RET_KDIM
    mu = jnp.zeros_like(y)
    for h in range(RET_HEADS):
        s = jnp.sum(jnp.where(head == h, y, 0.0), axis=1, keepdims=True) * inv
        mu = jnp.where(head == h, s, mu)
    d = y - mu
    var = jnp.zeros_like(y)
    for h in range(RET_HEADS):
        s = jnp.sum(jnp.where(head == h, d * d, 0.0), axis=1, keepdims=True) * inv
        var = jnp.where(head == h, s, var)
    yn = d * lax.rsqrt(var + GN_EPS)
    o_ref[...] = (gate * jax.nn.sigmoid(gate) * yn).astype(o_ref.dtype)


def _retention_tables():
    heads = np.arange(RET_HEADS, dtype=np.float64)
    log_gamma = np.log1p(-np.exp2(-5.0 - heads))
    t = np.arange(RET_TILE)
    chunk = t // CHUNK
    diff = (t[:, None] - t[None, :]).astype(np.float64)
    same = chunk[:, None] == chunk[None, :]
    earlier = chunk[None, :] < chunk[:, None]
    expo = np.where(same, np.abs(diff), diff)
    dmask = np.where((same | earlier)[None], np.exp(log_gamma[:, None, None] * expo[None]), 0.0)
    rowdec = np.exp(log_gamma[None, :] * (t[:, None] + 1.0))
    keydec = np.exp(log_gamma[None, :] * (RET_TILE - 1.0 - t[:, None]))
    tiledec = np.exp(log_gamma * RET_TILE)
    rep = lambda a: np.repeat(a, RET_KDIM, axis=-1)
    width = RET_HEADS * RET_KDIM
    tiledec_full = np.broadcast_to(rep(tiledec[None, :]).T, (width, width))
    return (jnp.asarray(dmask, F32), jnp.asarray(rep(rowdec), F32), jnp.asarray(rep(keydec), F32),
            jnp.asarray(tiledec_full, F32))


def _pool_part(z, wp_ref, scale_ref, o_ref, pool_ref, seq_tile):
    n_rows = POOL_PAD + TM
    pool_ref[0, POOL_PAD:, :] = z
    sums = []
    for level in range(len(POOL_WINDOWS)):
        lo, shift = F32_SUBLANES * (level + 1), 2 ** level
        s = pool_ref[level, lo:n_rows, :] + pool_ref[level, lo - shift:n_rows - shift, :]
        if level + 1 < len(POOL_WINDOWS):
            pool_ref[level + 1, lo:n_rows, :] = s
        sums.append(s[POOL_PAD - lo:])
    group = lax.broadcasted_iota(jnp.int32, (TM, BRANCH_WIDTH), 1) // (BRANCH_WIDTH // len(POOL_WINDOWS))
    win = sums[-1]
    for gi in range(len(POOL_WINDOWS) - 1):
        win = jnp.where(group == gi, sums[gi], win)
    t = seq_tile * TM + lax.broadcasted_iota(jnp.int32, (TM, BRANCH_WIDTH), 0)
    count = jnp.minimum(t + 1, jnp.left_shift(2, group)).astype(F32)
    pooled = win / count - z
    y = _dot(pooled.astype(BF16), wp_ref[...]) * scale_ref[...]
    o_ref[...] = y.astype(o_ref.dtype)
    pool_ref[0, POOL_PAD - POOL_HALO:POOL_PAD, :] = pool_ref[0, n_rows - POOL_HALO:n_rows, :]


def _rms_norm(x, g):
    return x * lax.rsqrt(jnp.mean(x * x, axis=-1, keepdims=True) + RMS_EPS) * g


def _mla_part(proj, cos, sin, qg_ref, kvg_ref, wq1_ref, wq2_ref, wk_ref, wvt_ref, qt_ref, k_ref, vt_ref):
    cq = _rms_norm(proj[:, :MLA_Q_RANK], qg_ref[...]).astype(BF16)
    ckv = _rms_norm(proj[:, MLA_Q_RANK:MLA_Q_RANK + MLA_KV_RANK], kvg_ref[...]).astype(BF16)
    k_raw = proj[:, MLA_Q_RANK + MLA_KV_RANK:]
    k_pe = pltpu.roll(k_raw * cos + _swap_halves(k_raw, MLA_ROPE // 2) * sin, MLA_NOPE, 1)
    ck = pltpu.roll(cos, MLA_NOPE, 1)
    sk = pltpu.roll(sin, MLA_NOPE, 1)
    lane = lax.broadcasted_iota(jnp.int32, ck.shape, 1)
    scale = MLA_QK ** -0.5 * math.log2(math.e)
    cq_tab = (ck + jnp.where(lane < MLA_NOPE, 1.0, 0.0)) * scale
    sq_tab = sk * scale
    tile4 = lambda a: jnp.concatenate([a] * MLA_HEADS, axis=1)
    q = _dot(cq, wq1_ref[...]) * tile4(cq_tab) + _dot(cq, wq2_ref[...]) * tile4(sq_tab)
    k = _dot(ckv, wk_ref[...]) + tile4(k_pe)
    for h in range(MLA_HEADS):
        rows = slice(h * HEAD_PAD, (h + 1) * HEAD_PAD)
        qt_ref[rows, :] = q[:, rows].T.astype(qt_ref.dtype)
    k_ref[...] = k.astype(k_ref.dtype)
    for t in range(TM // ATT_TK):
        vt_ref[t] = _dot_nt(wvt_ref[...], ckv[t * ATT_TK:(t + 1) * ATT_TK]).astype(vt_ref.dtype)


def _mixers_kernel(x_ref, w_ref, cs_ref, expand_ref,
                   lng_ref, lnb_ref, ws_ref, bias_ref, wp_ref, scale_ref,
                   qg_ref, kvg_ref, wq1_ref, wq2_ref, wk_ref, wvt_ref,
                   dmask_ref, rowdec_ref, keydec_ref, tiledec_ref,
                   ya_ref, yb_ref, yc_ref, qt_ref, k_ref, vt_ref, state_ref, pool_ref):
    @pl.when(pl.program_id(1) == 0)
    def _():
        state_ref[...] = jnp.zeros_like(state_ref)
        pool_ref[0, 0:POOL_PAD, :] = jnp.zeros((POOL_PAD, BRANCH_WIDTH), F32)

    values_per_tile = TM // ATT_TK
    for t in range(x_ref.shape[0] // TM):
        rows = slice(t * TM, (t + 1) * TM)
        _mixers_tile(x_ref.at[rows, :], w_ref, cs_ref.at[rows, :], expand_ref,
                     lng_ref, lnb_ref, ws_ref, bias_ref, wp_ref, scale_ref,
                     qg_ref, kvg_ref, wq1_ref, wq2_ref, wk_ref, wvt_ref,
                     dmask_ref, rowdec_ref, keydec_ref, tiledec_ref,
                     ya_ref.at[rows, :], yb_ref.at[rows, :], yc_ref.at[rows, :], qt_ref.at[t],
                     k_ref.at[rows, :], vt_ref.at[t * values_per_tile:(t + 1) * values_per_tile],
                     state_ref, pool_ref, pl.program_id(1) * (x_ref.shape[0] // TM) + t)


def _mixers_tile(x_ref, w_ref, cs_ref, expand_ref,
                 lng_ref, lnb_ref, ws_ref, bias_ref, wp_ref, scale_ref,
                 qg_ref, kvg_ref, wq1_ref, wq2_ref, wk_ref, wvt_ref,
                 dmask_ref, rowdec_ref, keydec_ref, tiledec_ref,
                 ya_ref, yb_ref, yc_ref, qt_ref, k_ref, vt_ref, state_ref, pool_ref, seq_tile):
    xb = x_ref[...].astype(BF16)
    p_sgu = _dot_nt(xb, w_ref[_OFF_SGU:_OFF_RET, :])
    p_pool = _dot_nt(xb, w_ref[_OFF_POOL:_OFF_CQ, :])
    p_mla = _dot_nt(xb, w_ref[_OFF_CQ:MIX_COLS, :])
    p_ret = _dot_nt(xb, w_ref[_OFF_RET:_OFF_POOL, :])
    cs = cs_ref[...]
    hi = cs.astype(BF16)
    rest = cs - hi.astype(F32)
    mid = rest.astype(BF16)
    lo = (rest - mid.astype(F32)).astype(BF16)
    rope = _dot(jnp.concatenate([hi, mid, lo], axis=1), expand_ref[...])
    ret_cos, ret_sin, mla_cos, mla_sin = (rope[:, t * HEAD_PAD:(t + 1) * HEAD_PAD] for t in range(4))
    _sgu_part(p_sgu, lng_ref, lnb_ref, ws_ref, bias_ref, ya_ref)
    _pool_part(p_pool, wp_ref, scale_ref, yc_ref, pool_ref, seq_tile)
    _mla_part(p_mla, mla_cos, mla_sin, qg_ref, kvg_ref, wq1_ref, wq2_ref, wk_ref, wvt_ref,
              qt_ref, k_ref, vt_ref)
    for r in range(TM // RET_TILE):
        rows = slice(r * RET_TILE, (r + 1) * RET_TILE)
        cos = jnp.concatenate([ret_cos[rows], ret_cos[rows]], axis=1)
        sin = jnp.concatenate([ret_sin[rows], ret_sin[rows]], axis=1)
        _ret_part(p_ret[rows], cos, sin, dmask_ref, rowdec_ref, keydec_ref, tiledec_ref,
                  yb_ref.at[rows, :], state_ref)


def _mixers(layer, x3, w_mix, rope, sgu_params, pool_params, mla_params, casts=()):
    bsz, seq, _ = x3.shape
    width = RET_HEADS * RET_KDIM
    ret_tables = _retention_tables()
    block = TM * MIX_TILES
    seq_tiles = seq // block
    tile = lambda last: pl.BlockSpec((None, block, last), lambda b, s: (b, s, 0))
    layered = tuple(sgu_params) + tuple(pool_params) + tuple(mla_params)
    in_specs = ([tile(D_MODEL), _const_spec(w_mix.shape), tile(HEAD_PAD), _const_spec(rope[1].shape)]
                + [_layer_spec(a, layer) for a in layered]
                + [_const_spec(a.shape) for a in ret_tables])
    cast_in, cast_out, cast_shapes = _cast_specs(casts, bsz * seq_tiles, lambda b, s: b * seq_tiles + s)
    branch = jax.ShapeDtypeStruct((bsz, seq, BRANCH_WIDTH), BF16)
    assert TM == ATT_TQ
    qt = jax.ShapeDtypeStruct((bsz, seq // TM, MLA_HEADS * HEAD_PAD, TM), BF16)
    kk = jax.ShapeDtypeStruct((bsz, seq, MLA_HEADS * HEAD_PAD), BF16)
    vt = jax.ShapeDtypeStruct((bsz, seq // ATT_TK, MLA_HEADS * MLA_V, ATT_TK), BF16)
    return pl.pallas_call(
        _with_casts(_mixers_kernel, len(in_specs), 6, casts, bsz * seq_tiles),
        grid=(bsz, seq_tiles),
        in_specs=in_specs + cast_in,
        out_specs=[tile(BRANCH_WIDTH)] * 3
                  + [pl.BlockSpec((None, MIX_TILES, MLA_HEADS * HEAD_PAD, TM), lambda b, s: (b, s, 0, 0)),
                     tile(MLA_HEADS * HEAD_PAD),
                     pl.BlockSpec((None, block // ATT_TK, MLA_HEADS * MLA_V, ATT_TK), lambda b, s: (b, s, 0, 0))]
                  + cast_out,
        out_shape=[branch, branch, branch, qt, kk, vt] + cast_shapes,
        scratch_shapes=[pltpu.VMEM((width, width), F32),
                        pltpu.VMEM((len(POOL_WINDOWS), POOL_PAD + TM, BRANCH_WIDTH), F32)],
        compiler_params=_params(2, "arbitrary"),
        name="token_mixers",
    )(x3, w_mix, *rope, *sgu_params, *pool_params, *mla_params, *ret_tables, *_cast_operands(casts))


def _attn_kernel(qt_ref, k_ref, vt_ref, o_ref, st_ref, m_ref, acc_ref):
    key_chunk = lax.broadcasted_iota(jnp.int32, (ATT_TK, ATT_TQ), 0) // CHUNK
    qry_chunk = lax.broadcasted_iota(jnp.int32, (ATT_TK, ATT_TQ), 1) // CHUNK
    diag_masks = [key_chunk + t * (ATT_TK // CHUNK) <= qry_chunk for t in range(ATT_TQ // ATT_TK)]
    heads = range(MLA_HEADS)
    every_query = slice(0, ATT_TQ)
    upper = slice(ATT_TK, ATT_TQ)
    ones = jnp.ones((ATT_ONES, ATT_TK), BF16)

    def query_tile(qi, carry):
        def scores(j, slot, queries=every_query, hs=heads):
            rows = pl.ds(pl.multiple_of(j * ATT_TK, ATT_TK), ATT_TK)
            for h in hs:
                cols = slice(h * HEAD_PAD, (h + 1) * HEAD_PAD)
                st_ref[slot, h, :, queries] = _dot(k_ref[rows, cols], qt_ref[qi, cols, queries])

        def softmax_pv(j, slot, mask=None, queries=every_query, hs=heads):
            for h in hs:
                load = lambda: (st_ref[slot, h, :, queries] if mask is None
                                else jnp.where(mask[:, queries], st_ref[slot, h, :, queries], -jnp.inf))
                m = m_ref[h, :, queries]
                m_new = jnp.maximum(m, jnp.max(load(), axis=0, keepdims=True))
                p = jnp.exp2((load() - m_new).astype(BF16))
                lhs = jnp.concatenate([vt_ref[j, h * MLA_V:(h + 1) * MLA_V, :], ones], axis=0)
                acc_ref[h, :, queries] = jnp.exp2(m - m_new) * acc_ref[h, :, queries] + _dot(lhs, p)
                m_ref[h, :, queries] = m_new

        def pair(i, carry):
            for h in heads:
                scores(2 * i + 1, 1, hs=(h,))
                softmax_pv(2 * i, 0, hs=(h,))
            for h in heads:
                scores(2 * i + 2, 0, hs=(h,))
                softmax_pv(2 * i + 1, 1, hs=(h,))
            return carry

        m_ref[...] = jnp.full(m_ref.shape, -jnp.inf, F32)
        acc_ref[...] = jnp.zeros(acc_ref.shape, F32)
        scores(0, 0)
        lax.fori_loop(0, qi, pair, 0)
        for h in heads:
            scores(2 * qi + 1, 1, upper, hs=(h,))
            softmax_pv(2 * qi, 0, diag_masks[0], hs=(h,))
        softmax_pv(2 * qi + 1, 1, diag_masks[1], upper)
        out_t = jnp.concatenate([acc_ref[h, :MLA_V] / acc_ref[h, MLA_V:MLA_V + 1] for h in heads], axis=0)
        rows = pl.ds(pl.multiple_of(qi * ATT_TQ, ATT_TQ), ATT_TQ)
        o_ref[rows, :] = out_t.T.astype(o_ref.dtype)
        return carry

    lax.fori_loop(0, qt_ref.shape[0], query_tile, 0)


def _attention(qt4, k3, vt4):
    bsz, seq, _ = k3.shape
    whole = lambda a: pl.BlockSpec((None,) + a.shape[1:], lambda b: (b,) + (0,) * (a.ndim - 1))
    return pl.pallas_call(
        _attn_kernel,
        grid=(bsz,),
        in_specs=[whole(qt4), whole(k3), whole(vt4)],
        out_specs=pl.BlockSpec((None, seq, MLA_HEADS * MLA_V), lambda b: (b, 0, 0)),
        out_shape=jax.ShapeDtypeStruct((bsz, seq, MLA_HEADS * MLA_V), BF16),
        scratch_shapes=[pltpu.VMEM((2, MLA_HEADS, ATT_TK, ATT_TQ), F32),
                        pltpu.VMEM((MLA_HEADS, 1, ATT_TQ), F32),
                        pltpu.VMEM((MLA_HEADS, MLA_V + ATT_ONES, ATT_TQ), F32)],
        compiler_params=_params(1),
        name="mla_attention",
    )(qt4, k3, vt4)


def _merge_kernel(x_ref, ya_ref, yb_ref, yc_ref, yd_ref, wg_ref, wb_ref, wo_ref, g_ref, b_ref, o_ref):
    def residual(rows):
        x = x_ref[rows, :]
        xb = x.astype(BF16)
        merged = None
        for n, y_ref in enumerate((ya_ref, yb_ref, yc_ref, yd_ref)):
            gate = jax.nn.sigmoid(_dot(xb, wg_ref[:, n * D_MODEL:(n + 1) * D_MODEL]))
            term = gate * _dot(y_ref[rows, :], wb_ref[n * BRANCH_WIDTH:(n + 1) * BRANCH_WIDTH, :])
            merged = term if merged is None else merged + term
        return ALPHA * x + _dot(merged.astype(BF16), wo_ref[...])

    _pipelined_rows(o_ref, g_ref, b_ref, residual)


def _merge(layer, x, ys, wg, wb, wo, g, b, casts=()):
    n = x.shape[0]
    steps = n // TM_WIDE
    row = lambda last: pl.BlockSpec((TM_WIDE, last), lambda i: (i, 0))
    cast_in, cast_out, cast_shapes = _cast_specs(casts, steps, lambda i: i)
    return pl.pallas_call(
        _with_casts(_merge_kernel, 10, 1, casts, steps),
        grid=(steps,),
        in_specs=[row(D_MODEL)] + [row(BRANCH_WIDTH)] * N_BRANCH
                 + [_const_spec(a.shape) for a in (wg, wb, wo)]
                 + [_layer_spec(a, layer) for a in (g, b)] + cast_in,
        out_specs=[row(D_MODEL)] + cast_out,
        out_shape=[jax.ShapeDtypeStruct((n, D_MODEL), F32)] + cast_shapes,
        compiler_params=_params(1),
        name="merge_ln",
    )(x, *ys, wg, wb, wo, g, b, *_cast_operands(casts))


def _rope_tables(positions):
    n_r, n_m = RET_KDIM // 2, MLA_ROPE // 2
    inv_r = ROPE_BASE ** (-jnp.arange(0, RET_KDIM, 2, dtype=F32) / RET_KDIM)
    inv_m = ROPE_BASE ** (-jnp.arange(0, MLA_ROPE, 2, dtype=F32) / MLA_ROPE)
    pad = jnp.zeros((HEAD_PAD - 2 * (n_r + n_m),), F32)
    ang = positions.astype(F32)[..., None] * jnp.concatenate([inv_r, inv_r, inv_m, inv_m, pad])
    lane = np.arange(HEAD_PAD)
    is_cos = jnp.asarray((lane < n_r) | ((lane >= 2 * n_r) & (lane < 2 * n_r + n_m)))
    compact = jnp.where(is_cos, jnp.cos(ang), jnp.sin(ang))
    expand = np.zeros((HEAD_PAD, 4 * HEAD_PAD), np.float32)
    for l in range(HEAD_PAD):
        expand[l % n_r, l] = 1.0
        expand[n_r + l % n_r, HEAD_PAD + l] = -1.0 if l % RET_KDIM < n_r else 1.0
        if l < MLA_ROPE:
            expand[2 * n_r + l % n_m, 2 * HEAD_PAD + l] = 1.0
            expand[2 * n_r + n_m + l % n_m, 3 * HEAD_PAD + l] = -1.0 if l < n_m else 1.0
    return compact, jnp.asarray(np.tile(expand, (3, 1)), BF16)


def _swap_cols(w):
    half = w.shape[-1] // 2
    return jnp.concatenate([w[..., half:], w[..., :half]], axis=-1)


def _mla_weights(w_uq, w_ukv):
    depth = w_uq.shape[0]
    uq = w_uq.reshape(depth, MLA_Q_RANK, MLA_HEADS, MLA_QK)
    q_nope, q_rope = uq[..., :MLA_NOPE], uq[..., MLA_NOPE:]
    zq = jnp.zeros((depth, MLA_Q_RANK, MLA_HEADS, HEAD_PAD - MLA_QK), w_uq.dtype)
    wq1 = jnp.concatenate([q_nope, q_rope, zq], axis=-1)
    wq2 = jnp.concatenate([jnp.zeros_like(q_nope), _swap_cols(q_rope), zq], axis=-1)
    ukv = w_ukv.reshape(depth, MLA_KV_RANK, MLA_HEADS, MLA_NOPE + MLA_V)
    k_nope, v = ukv[..., :MLA_NOPE], ukv[..., MLA_NOPE:]
    wk = jnp.concatenate([k_nope, jnp.zeros_like(k_nope)], axis=-1)
    wvt = jnp.swapaxes(v.reshape(depth, MLA_KV_RANK, MLA_HEADS * MLA_V), 1, 2)
    flat = lambda a: a.reshape(depth, a.shape[1], MLA_HEADS * HEAD_PAD).astype(BF16)
    return flat(wq1), flat(wq2), flat(wk), wvt.astype(BF16)


def kernel(x, p, positions, ffn1_up, ffn1_down, ln1_g, ln1_b, w_in, sgu_ln_g, sgu_ln_b, sgu_w, sgu_b,
           pool_w, pool_scale, mla_q_norm, mla_kv_norm, mla_w_uq, mla_w_ukv, w_branch, w_out,
           ln2_g, ln2_b, ffn2_up, ffn2_down, w_ple_gate, w_ple, ln3_g, ln3_b):
    bsz, seq, dm = x.shape
    n = bsz * seq
    rope = _rope_tables(positions)
    bf = lambda a: a.astype(BF16)
    rows = lambda a: a[:, None, :]

    up1, down1 = bf(ffn1_up[0]), bf(ffn1_down[0])
    w_in_t = jnp.swapaxes(w_in, 1, 2)
    w_branch2 = w_branch.reshape(DEPTH, N_BRANCH * BRANCH_WIDTH, dm)
    wp = bf(w_ple)
    wq1, wq2, wk, wvt = _mla_weights(mla_w_uq, mla_w_ukv)
    sgu_bias = jnp.repeat(jnp.swapaxes(sgu_b, 1, 2), SGU_WIDTH // SGU_GROUPS, axis=2)
    groups = len(POOL_WINDOWS)
    pool_bd = bf(jnp.einsum("lgcd,gh->lgchd", pool_w, jnp.eye(groups, dtype=pool_w.dtype))
                 .reshape(DEPTH, BRANCH_WIDTH, BRANCH_WIDTH))
    p2 = p.reshape(DEPTH, n, PLE_DIM)

    h = x.reshape(n, dm)
    for i in range(DEPTH):
        last = i + 1 == DEPTH
        h, w_mix, w_gate, wb, wo = _ffn_ln(
            i, h, up1, down1, rows(ln1_g), rows(ln1_b),
            (_cast(w_in_t, i, MIX_COLS), _CastT(w_in_t, i, _OFF_GATE, N_BRANCH * dm), _cast(w_branch2, i),
             _cast(w_out, i)))
        y_a, y_b, y_c, q, k, vt, up2, wpg, *nxt_up = _mixers(
            i, h.reshape(bsz, seq, dm), w_mix, rope,
            (rows(sgu_ln_g), rows(sgu_ln_b), sgu_w, sgu_bias), (pool_bd, rows(pool_scale)),
            (rows(mla_q_norm), rows(mla_kv_norm), wq1, wq2, wk, wvt),
            (_cast(ffn2_up, i), _cast(w_ple_gate, i)) + (() if last else (_cast(ffn1_up, i + 1),)))
        y_d = _attention(q, k, vt)
        ys = tuple(y.reshape(n, BRANCH_WIDTH) for y in (y_a, y_b, y_c, y_d))
        h, down2, *nxt_down = _merge(
            i, h, ys, w_gate, wb, wo, rows(ln2_g), rows(ln2_b),
            (_cast(ffn2_down, i),) + (() if last else (_cast(ffn1_down, i + 1),)))
        h = _ffn_ple_ln(i, h, p2, up2, down2, wpg, wp, rows(ln3_g), rows(ln3_b))
        if not last:
            (up1,), (down1,) = nxt_up, nxt_down
    return h.reshape(bsz, seq, dm)
```

```python
import math
from typing import NamedTuple

import numpy as np
import jax
import jax.numpy as jnp
from jax import lax
from jax.experimental import pallas as pl
from jax.experimental.pallas import tpu as pltpu

D_MODEL = 1024
DEPTH = 2
CHUNK = 64
SGU_WIDTH = 256
SGU_BLOCK = 128
SGU_GROUPS = 4
RET_HEADS = 4
RET_KDIM = 64
POOL_WINDOWS = (2, 4, 8, 16)
POOL_HALO = 16
POOL_PAD = 32
MLA_HEADS = 4
MLA_NOPE = 64
MLA_ROPE = 32
MLA_QK = MLA_NOPE + MLA_ROPE
MLA_V = 64
MLA_Q_RANK = 256
MLA_KV_RANK = 128
HEAD_PAD = 128
F32_SUBLANES = 8
ROPE_BASE = 10000.0
N_BRANCH = 4
BRANCH_WIDTH = 256
D_FF = 2816
PLE_DIM = 256
ALPHA = (2 * DEPTH) ** 0.25
LN_EPS = 1e-5
RMS_EPS = 1e-6
GN_EPS = 1e-5

_OFF_SGU = 0
_OFF_RET = 512
_OFF_POOL = 1536
_OFF_CQ = 1792
_OFF_GATE = 2208
MIX_COLS = 2304

TM = 512
MIX_TILES = 2
TM_WIDE = 1024
SUB = 256
RET_TILE = 256
ATT_TQ = 512
ATT_TK = 256
ATT_ONES = 16
VMEM_LIMIT = 56 * 1024 * 1024

BF16 = jnp.bfloat16
F32 = jnp.float32


def _dot(a, b):
    return jnp.dot(a, b, preferred_element_type=F32)


def _dot_nt(a, b):
    return lax.dot_general(a, b, (((1,), (1,)), ((), ())), preferred_element_type=F32)


def _dot_tn(a, b):
    return lax.dot_general(a, b, (((0,), (0,)), ((), ())), preferred_element_type=F32)


def _layer_norm(r, g, b, eps):
    mu = jnp.mean(r, axis=-1, keepdims=True)
    d = r - mu
    var = jnp.mean(d * d, axis=-1, keepdims=True)
    return d * lax.rsqrt(var + eps) * g + b


def _const_spec(shape):
    zeros = (0,) * len(shape)
    return pl.BlockSpec(shape, lambda *_: zeros, pipeline_mode=pl.Buffered(1))


def _layer_spec(arr, layer):
    index = (layer,) + (0,) * (arr.ndim - 1)
    return pl.BlockSpec((None,) + arr.shape[1:], lambda *_: index, pipeline_mode=pl.Buffered(1))


def _params(n_grid, semantics="parallel"):
    return pltpu.CompilerParams(
        dimension_semantics=(semantics,) * n_grid, vmem_limit_bytes=VMEM_LIMIT)


class _Cast(NamedTuple):
    src: jax.Array
    layer: int
    rows: int


class _CastT(NamedTuple):
    src: jax.Array
    layer: int
    row0: int
    rows: int


def _cast(src, layer, rows=None):
    return _Cast(src, layer, src.shape[1] if rows is None else rows)


def _cast_specs(casts, n_steps, step_of):
    in_specs, out_specs, out_shapes = [], [], []
    for c in casts:
        blk, cols = c.rows // n_steps, c.src.shape[2]
        if isinstance(c, _Cast):
            in_specs.append(pl.BlockSpec((None, blk, cols), lambda *g, c=c: (c.layer, step_of(*g), 0)))
            out_specs.append(pl.BlockSpec((blk, cols), lambda *g: (step_of(*g), 0)))
            out_shapes.append(jax.ShapeDtypeStruct((c.rows, cols), BF16))
        else:
            first = c.row0 // blk
            in_specs += [pl.BlockSpec((None, blk, cols),
                                      lambda *g, c=c, k=k: (c.layer, first + step_of(*g) + k, 0))
                         for k in range(2)]
            out_specs.append(pl.BlockSpec((cols, blk), lambda *g: (0, step_of(*g))))
            out_shapes.append(jax.ShapeDtypeStruct((cols, c.rows), BF16))
    return in_specs, out_specs, out_shapes


def _cast_operands(casts):
    return [c.src for c in casts for _ in range(1 if isinstance(c, _Cast) else 2)]


def _with_casts(body, n_in, n_out, casts, n_steps):
    n_src = sum(1 if isinstance(c, _Cast) else 2 for c in casts)

    def kernel(*refs):
        ins, rest = refs[:n_in], refs[n_in:]
        cast_in, rest = list(rest[:n_src]), rest[n_src:]
        outs, rest = rest[:n_out], rest[n_out:]
        cast_out, scratch = rest[:len(casts)], rest[len(casts):]
        body(*ins, *outs, *scratch)
        for c, dst_ref in zip(casts, cast_out):
            if isinstance(c, _Cast):
                dst_ref[...] = cast_in.pop(0)[...].astype(BF16)
            else:
                lo_ref, hi_ref = cast_in.pop(0), cast_in.pop(0)
                off = c.row0 % (c.rows // n_steps)
                window = jnp.concatenate([lo_ref[off:, :], hi_ref[:off, :]], axis=0)
                dst_ref[...] = window.T.astype(BF16)

    return kernel


def _pipelined_rows(o_ref, g_ref, b_ref, residual_fn):
    pending = None
    for s in range(o_ref.shape[0] // SUB):
        rows = slice(s * SUB, (s + 1) * SUB)
        r = residual_fn(rows)
        if pending is not None:
            o_ref[pending[0], :] = _layer_norm(pending[1], g_ref[...], b_ref[...], LN_EPS)
        pending = (rows, r)
    o_ref[pending[0], :] = _layer_norm(pending[1], g_ref[...], b_ref[...], LN_EPS)


def _ffn_body(x, up_ref, down_ref):
    xb = x.astype(BF16)
    a = _dot(xb, up_ref[:, :D_FF])
    b = _dot(xb, up_ref[:, D_FF:])
    h = (a * jax.nn.sigmoid(a) * b).astype(BF16)
    return xb, _dot(h, down_ref[...])


def _ffn_ln_kernel(x_ref, up_ref, down_ref, g_ref, b_ref, o_ref):
    def residual(rows):
        x = x_ref[rows, :]
        _, y = _ffn_body(x, up_ref, down_ref)
        return ALPHA * x + 0.5 * y

    _pipelined_rows(o_ref, g_ref, b_ref, residual)


def _ffn_ple_ln_kernel(x_ref, p_ref, up_ref, down_ref, wpg_ref, wp_ref, g_ref, b_ref, o_ref):
    def residual(rows):
        x = x_ref[rows, :]
        xb, y = _ffn_body(x, up_ref, down_ref)
        ple = jax.nn.sigmoid(_dot(xb, wpg_ref[...])) * _dot(p_ref[rows, :].astype(BF16), wp_ref[...])
        return ALPHA * x + 0.5 * y + ple

    _pipelined_rows(o_ref, g_ref, b_ref, residual)


def _ffn_ln(layer, x, up, down, g, b, casts=()):
    n = x.shape[0]
    steps = n // TM_WIDE
    row = pl.BlockSpec((TM_WIDE, D_MODEL), lambda i: (i, 0))
    cast_in, cast_out, cast_shapes = _cast_specs(casts, steps, lambda i: i)
    return pl.pallas_call(
        _with_casts(_ffn_ln_kernel, 5, 1, casts, steps),
        grid=(steps,),
        in_specs=[row, _const_spec(up.shape), _const_spec(down.shape), _layer_spec(g, layer),
                  _layer_spec(b, layer)] + cast_in,
        out_specs=[row] + cast_out,
        out_shape=[jax.ShapeDtypeStruct((n, D_MODEL), F32)] + cast_shapes,
        compiler_params=_params(1),
        name="ffn_ln",
    )(x, up, down, g, b, *_cast_operands(casts))


def _ffn_ple_ln(layer, x, p, up, down, wpg, wp, g, b):
    n = x.shape[0]
    row = pl.BlockSpec((TM_WIDE, D_MODEL), lambda i: (i, 0))
    return pl.pallas_call(
        _ffn_ple_ln_kernel,
        grid=(n // TM_WIDE,),
        in_specs=[row, pl.BlockSpec((None, TM_WIDE, PLE_DIM), lambda i: (layer, i, 0)),
                  _const_spec(up.shape), _const_spec(down.shape), _const_spec(wpg.shape)]
                 + [_layer_spec(a, layer) for a in (wp, g, b)],
        out_specs=row,
        out_shape=jax.ShapeDtypeStruct((n, D_MODEL), F32),
        compiler_params=_params(1),
        name="ffn_ple_ln",
    )(x, p, up, down, wpg, wp, g, b)


def _swap_halves(x, half):
    width = x.shape[-1]
    lane = lax.broadcasted_iota(jnp.int32, x.shape, x.ndim - 1)
    first = (lane % (2 * half)) < half
    return jnp.where(first, pltpu.roll(x, width - half, x.ndim - 1), pltpu.roll(x, half, x.ndim - 1))


def _sgu_part(uv, lng_ref, lnb_ref, ws_ref, bias_ref, o_ref):
    u = jax.nn.gelu(uv[:, :SGU_WIDTH])
    v = _layer_norm(jax.nn.gelu(uv[:, SGU_WIDTH:]), lng_ref[...], lnb_ref[...], LN_EPS)
    vb = v.astype(BF16)
    t_row = lax.broadcasted_iota(jnp.int32, (SGU_BLOCK, SGU_BLOCK), 0)
    t_col = lax.broadcasted_iota(jnp.int32, (SGU_BLOCK, SGU_BLOCK), 1)
    causal = t_row >= t_col
    w_all = jnp.concatenate([jnp.where(causal, ws_ref[g], 0.0).astype(BF16) for g in range(SGU_GROUPS)], axis=1)
    group = lax.broadcasted_iota(jnp.int32, (SGU_BLOCK, SGU_WIDTH), 1) // (SGU_WIDTH // SGU_GROUPS)
    bias = bias_ref[...]
    for blk in range(TM // SGU_BLOCK):
        rows = slice(blk * SGU_BLOCK, (blk + 1) * SGU_BLOCK)
        v_blk = vb[rows]
        v_groups = jnp.concatenate([jnp.where(group == g, v_blk, 0.0) for g in range(SGU_GROUPS)], axis=0)
        o_ref[rows, :] = (u[rows] * (bias + _dot(w_all, v_groups))).astype(o_ref.dtype)


def _ret_part(proj, cos, sin, dmask_ref, rowdec_ref, keydec_ref, tiledec_ref, o_ref, state_ref):
    width = RET_HEADS * RET_KDIM
    q = proj[:, :width]
    k = proj[:, width:2 * width]
    q = q * cos + _swap_halves(q, RET_KDIM // 2) * sin
    k = (k * cos + _swap_halves(k, RET_KDIM // 2) * sin) * RET_KDIM ** -0.5
    v = proj[:, 2 * width:3 * width]
    gate = proj[:, 3 * width:]
    vb = v.astype(BF16)
    kb = k.astype(BF16)

    head = lax.broadcasted_iota(jnp.int32, (RET_TILE, width), 1) // RET_KDIM
    y = _dot(q.astype(BF16), state_ref[...].astype(BF16)) * rowdec_ref[...]
    for h in range(RET_HEADS):
        qh = jnp.where(head == h, q, 0.0).astype(BF16)
        scores = _dot_nt(qh, kb) * dmask_ref[h]
        y = y + jnp.where(head == h, _dot(scores.astype(BF16), vb), 0.0)

    kd = (k * keydec_ref[...]).astype(BF16)
    row_head = lax.broadcasted_iota(jnp.int32, (width, width), 0) // RET_KDIM
    col_head = lax.broadcasted_iota(jnp.int32, (width, width), 1) // RET_KDIM
    kv = jnp.where(row_head == col_head, _dot_tn(kd, vb), 0.0)
    state_ref[...] = state_ref[...] * tiledec_ref[...] + kv

    inv = 1.0 / RET_KDIM
    mu = jnp.zeros_like(y)
    for h in range(RET_HEADS):
        s = jnp.sum(jnp.where(head == h, y, 0.0), axis=1, keepdims=True) * inv
        mu = jnp.where(head == h, s, mu)
    d = y - mu
    var = jnp.zeros_like(y)
    for h in range(RET_HEADS):
        s = jnp.sum(jnp.where(head == h, d * d, 0.0), axis=1, keepdims=True) * inv
        var = jnp.where(head == h, s, var)
    yn = d * lax.rsqrt(var + GN_EPS)
    o_ref[...] = (gate * jax.nn.sigmoid(gate) * yn).astype(o_ref.dtype)


def _retention_tables():
    heads = np.arange(RET_HEADS, dtype=np.float64)
    log_gamma = np.log1p(-np.exp2(-5.0 - heads))
    t = np.arange(RET_TILE)
    chunk = t // CHUNK
    diff = (t[:, None] - t[None, :]).astype(np.float64)
    same = chunk[:, None] == chunk[None, :]
    earlier = chunk[None, :] < chunk[:, None]
    expo = np.where(same, np.abs(diff), diff)
    dmask = np.where((same | earlier)[None], np.exp(log_gamma[:, None, None] * expo[None]), 0.0)
    rowdec = np.exp(log_gamma[None, :] * (t[:, None] + 1.0))
    keydec = np.exp(log_gamma[None, :] * (RET_TILE - 1.0 - t[:, None]))
    tiledec = np.exp(log_gamma * RET_TILE)
    rep = lambda a: np.repeat(a, RET_KDIM, axis=-1)
    width = RET_HEADS * RET_KDIM
    tiledec_full = np.broadcast_to(rep(tiledec[None, :]).T, (width, width))
    return (jnp.asarray(dmask, F32), jnp.asarray(rep(rowdec), F32), jnp.asarray(rep(keydec), F32),
            jnp.asarray(tiledec_full, F32))


def _pool_part(z, wp_ref, scale_ref, o_ref, pool_ref, seq_tile):
    n_rows = POOL_PAD + TM
    pool_ref[0, POOL_PAD:, :] = z
    sums = []
    for level in range(len(POOL_WINDOWS)):
        lo, shift = F32_SUBLANES * (level + 1), 2 ** level
        s = pool_ref[level, lo:n_rows, :] + pool_ref[level, lo - shift:n_rows - shift, :]
        if level + 1 < len(POOL_WINDOWS):
            pool_ref[level + 1, lo:n_rows, :] = s
        sums.append(s[POOL_PAD - lo:])
    group = lax.broadcasted_iota(jnp.int32, (TM, BRANCH_WIDTH), 1) // (BRANCH_WIDTH // len(POOL_WINDOWS))
    win = sums[-1]
    for gi in range(len(POOL_WINDOWS) - 1):
        win = jnp.where(group == gi, sums[gi], win)
    t = seq_tile * TM + lax.broadcasted_iota(jnp.int32, (TM, BRANCH_WIDTH), 0)
    count = jnp.minimum(t + 1, jnp.left_shift(2, group)).astype(F32)
    pooled = win / count - z
    y = _dot(pooled.astype(BF16), wp_ref[...]) * scale_ref[...]
    o_ref[...] = y.astype(o_ref.dtype)
    pool_ref[0, POOL_PAD - POOL_HALO:POOL_PAD, :] = pool_ref[0, n_rows - POOL_HALO:n_rows, :]


def _rms_norm(x, g):
    return x * lax.rsqrt(jnp.mean(x * x, axis=-1, keepdims=True) + RMS_EPS) * g


def _mla_part(proj, cos, sin, qg_ref, kvg_ref, wq1_ref, wq2_ref, wk_ref, wvt_ref, qt_ref, k_ref, vt_ref):
    cq = _rms_norm(proj[:, :MLA_Q_RANK], qg_ref[...]).astype(BF16)
    ckv = _rms_norm(proj[:, MLA_Q_RANK:MLA_Q_RANK + MLA_KV_RANK], kvg_ref[...]).astype(BF16)
    k_raw = proj[:, MLA_Q_RANK + MLA_KV_RANK:]
    k_pe = pltpu.roll(k_raw * cos + _swap_halves(k_raw, MLA_ROPE // 2) * sin, MLA_NOPE, 1)
    ck = pltpu.roll(cos, MLA_NOPE, 1)
    sk = pltpu.roll(sin, MLA_NOPE, 1)
    lane = lax.broadcasted_iota(jnp.int32, ck.shape, 1)
    scale = MLA_QK ** -0.5 * math.log2(math.e)
    cq_tab = (ck + jnp.where(lane < MLA_NOPE, 1.0, 0.0)) * scale
    sq_tab = sk * scale
    tile4 = lambda a: jnp.concatenate([a] * MLA_HEADS, axis=1)
    q = _dot(cq, wq1_ref[...]) * tile4(cq_tab) + _dot(cq, wq2_ref[...]) * tile4(sq_tab)
    k = _dot(ckv, wk_ref[...]) + tile4(k_pe)
    for h in range(MLA_HEADS):
        rows = slice(h * HEAD_PAD, (h + 1) * HEAD_PAD)
        qt_ref[rows, :] = q[:, rows].T.astype(qt_ref.dtype)
    k_ref[...] = k.astype(k_ref.dtype)
    for t in range(TM // ATT_TK):
        vt_ref[t] = _dot_nt(wvt_ref[...], ckv[t * ATT_TK:(t + 1) * ATT_TK]).astype(vt_ref.dtype)


def _mixers_kernel(x_ref, w_ref, cs_ref, expand_ref,
                   lng_ref, lnb_ref, ws_ref, bias_ref, wp_ref, scale_ref,
                   qg_ref, kvg_ref, wq1_ref, wq2_ref, wk_ref, wvt_ref,
                   dmask_ref, rowdec_ref, keydec_ref, tiledec_ref,
                   ya_ref, yb_ref, yc_ref, qt_ref, k_ref, vt_ref, state_ref, pool_ref):
    @pl.when(pl.program_id(1) == 0)
    def _():
        state_ref[...] = jnp.zeros_like(state_ref)
        pool_ref[0, 0:POOL_PAD, :] = jnp.zeros((POOL_PAD, BRANCH_WIDTH), F32)

    values_per_tile = TM // ATT_TK
    for t in range(x_ref.shape[0] // TM):
        rows = slice(t * TM, (t + 1) * TM)
        _mixers_tile(x_ref.at[rows, :], w_ref, cs_ref.at[rows, :], expand_ref,
                     lng_ref, lnb_ref, ws_ref, bias_ref, wp_ref, scale_ref,
                     qg_ref, kvg_ref, wq1_ref, wq2_ref, wk_ref, wvt_ref,
                     dmask_ref, rowdec_ref, keydec_ref, tiledec_ref,
                     ya_ref.at[rows, :], yb_ref.at[rows, :], yc_ref.at[rows, :], qt_ref.at[t],
                     k_ref.at[rows, :], vt_ref.at[t * values_per_tile:(t + 1) * values_per_tile],
                     state_ref, pool_ref, pl.program_id(1) * (x_ref.shape[0] // TM) + t)


def _mixers_tile(x_ref, w_ref, cs_ref, expand_ref,
                 lng_ref, lnb_ref, ws_ref, bias_ref, wp_ref, scale_ref,
                 qg_ref, kvg_ref, wq1_ref, wq2_ref, wk_ref, wvt_ref,
                 dmask_ref, rowdec_ref, keydec_ref, tiledec_ref,
                 ya_ref, yb_ref, yc_ref, qt_ref, k_ref, vt_ref, state_ref, pool_ref, seq_tile):
    xb = x_ref[...].astype(BF16)
    p_sgu = _dot_nt(xb, w_ref[_OFF_SGU:_OFF_RET, :])
    p_pool = _dot_nt(xb, w_ref[_OFF_POOL:_OFF_CQ, :])
    p_mla = _dot_nt(xb, w_ref[_OFF_CQ:MIX_COLS, :])
    p_ret = _dot_nt(xb, w_ref[_OFF_RET:_OFF_POOL, :])
    cs = cs_ref[...]
    hi = cs.astype(BF16)
    rest = cs - hi.astype(F32)
    mid = rest.astype(BF16)
    lo = (rest - mid.astype(F32)).astype(BF16)
    rope = _dot(jnp.concatenate([hi, mid, lo], axis=1), expand_ref[...])
    ret_cos, ret_sin, mla_cos, mla_sin = (rope[:, t * HEAD_PAD:(t + 1) * HEAD_PAD] for t in range(4))
    _sgu_part(p_sgu, lng_ref, lnb_ref, ws_ref, bias_ref, ya_ref)
    _pool_part(p_pool, wp_ref, scale_ref, yc_ref, pool_ref, seq_tile)
    _mla_part(p_mla, mla_cos, mla_sin, qg_ref, kvg_ref, wq1_ref, wq2_ref, wk_ref, wvt_ref,
              qt_ref, k_ref, vt_ref)
    for r in range(TM // RET_TILE):
        rows = slice(r * RET_TILE, (r + 1) * RET_TILE)
        cos = jnp.concatenate([ret_cos[rows], ret_cos[rows]], axis=1)
        sin = jnp.concatenate([ret_sin[rows], ret_sin[rows]], axis=1)
        _ret_part(p_ret[rows], cos, sin, dmask_ref, rowdec_ref, keydec_ref, tiledec_ref,
                  yb_ref.at[rows, :], state_ref)


def _mixers(layer, x3, w_mix, rope, sgu_params, pool_params, mla_params, casts=()):
    bsz, seq, _ = x3.shape
    width = RET_HEADS * RET_KDIM
    ret_tables = _retention_tables()
    block = TM * MIX_TILES
    seq_tiles = seq // block
    tile = lambda last: pl.BlockSpec((None, block, last), lambda b, s: (b, s, 0))
    layered = tuple(sgu_params) + tuple(pool_params) + tuple(mla_params)
    in_specs = ([tile(D_MODEL), _const_spec(w_mix.shape), tile(HEAD_PAD), _const_spec(rope[1].shape)]
                + [_layer_spec(a, layer) for a in layered]
                + [_const_spec(a.shape) for a in ret_tables])
    cast_in, cast_out, cast_shapes = _cast_specs(casts, bsz * seq_tiles, lambda b, s: b * seq_tiles + s)
    branch = jax.ShapeDtypeStruct((bsz, seq, BRANCH_WIDTH), BF16)
    assert TM == ATT_TQ
    qt = jax.ShapeDtypeStruct((bsz, seq // TM, MLA_HEADS * HEAD_PAD, TM), BF16)
    kk = jax.ShapeDtypeStruct((bsz, seq, MLA_HEADS * HEAD_PAD), BF16)
    vt = jax.ShapeDtypeStruct((bsz, seq // ATT_TK, MLA_HEADS * MLA_V, ATT_TK), BF16)
    return pl.pallas_call(
        _with_casts(_mixers_kernel, len(in_specs), 6, casts, bsz * seq_tiles),
        grid=(bsz, seq_tiles),
        in_specs=in_specs + cast_in,
        out_specs=[tile(BRANCH_WIDTH)] * 3
                  + [pl.BlockSpec((None, MIX_TILES, MLA_HEADS * HEAD_PAD, TM), lambda b, s: (b, s, 0, 0)),
                     tile(MLA_HEADS * HEAD_PAD),
                     pl.BlockSpec((None, block // ATT_TK, MLA_HEADS * MLA_V, ATT_TK), lambda b, s: (b, s, 0, 0))]
                  + cast_out,
        out_shape=[branch, branch, branch, qt, kk, vt] + cast_shapes,
        scratch_shapes=[pltpu.VMEM((width, width), F32),
                        pltpu.VMEM((len(POOL_WINDOWS), POOL_PAD + TM, BRANCH_WIDTH), F32)],
        compiler_params=_params(2, "arbitrary"),
        name="token_mixers",
    )(x3, w_mix, *rope, *sgu_params, *pool_params, *mla_params, *ret_tables, *_cast_operands(casts))


def _attn_kernel(qt_ref, k_ref, vt_ref, o_ref, st_ref, m_ref, acc_ref):
    key_chunk = lax.broadcasted_iota(jnp.int32, (ATT_TK, ATT_TQ), 0) // CHUNK
    qry_chunk = lax.broadcasted_iota(jnp.int32, (ATT_TK, ATT_TQ), 1) // CHUNK
    diag_masks = [key_chunk + t * (ATT_TK // CHUNK) <= qry_chunk for t in range(ATT_TQ // ATT_TK)]
    heads = range(MLA_HEADS)
    every_query = slice(0, ATT_TQ)
    upper = slice(ATT_TK, ATT_TQ)
    ones = jnp.ones((ATT_ONES, ATT_TK), BF16)
    n_query_tiles = qt_ref.shape[0]

    def scores(qi, j, slot, queries=every_query, hs=heads):
        rows = pl.ds(pl.multiple_of(j * ATT_TK, ATT_TK), ATT_TK)
        for h in hs:
            cols = slice(h * HEAD_PAD, (h + 1) * HEAD_PAD)
            st_ref[slot, h, :, queries] = _dot(k_ref[rows, cols], qt_ref[qi, cols, queries])

    def softmax_pv(j, slot, mask=None, queries=every_query, hs=heads):
        for h in hs:
            load = lambda: (st_ref[slot, h, :, queries] if mask is None
                            else jnp.where(mask[:, queries], st_ref[slot, h, :, queries], -jnp.inf))
            m = m_ref[h, :, queries]
            m_new = jnp.maximum(m, jnp.max(load(), axis=0, keepdims=True))
            p = jnp.exp2((load() - m_new).astype(BF16))
            lhs = jnp.concatenate([vt_ref[j, h * MLA_V:(h + 1) * MLA_V, :], ones], axis=0)
            acc_ref[h, :, queries] = jnp.exp2(m - m_new) * acc_ref[h, :, queries] + _dot(lhs, p)
            m_ref[h, :, queries] = m_new

    def query_tile(qi, carry):
        def pair(i, carry):
            for h in heads:
                scores(qi, 2 * i + 1, 1, hs=(h,))
                softmax_pv(2 * i, 0, hs=(h,))
            for h in heads:
                scores(qi, 2 * i + 2, 0, hs=(h,))
                softmax_pv(2 * i + 1, 1, hs=(h,))
            return carry

        m_ref[...] = jnp.full(m_ref.shape, -jnp.inf, F32)
        acc_ref[...] = jnp.zeros(acc_ref.shape, F32)
        lax.fori_loop(0, qi, pair, 0)
        for h in heads:
            scores(qi, 2 * qi + 1, 1, upper, hs=(h,))
            softmax_pv(2 * qi, 0, diag_masks[0], hs=(h,))
        softmax_pv(2 * qi + 1, 1, diag_masks[1], upper)
        scores(jnp.minimum(qi + 1, n_query_tiles - 1), 0, 0)
        out_t = jnp.concatenate([acc_ref[h, :MLA_V] / acc_ref[h, MLA_V:MLA_V + 1] for h in heads], axis=0)
        rows = pl.ds(pl.multiple_of(qi * ATT_TQ, ATT_TQ), ATT_TQ)
        o_ref[rows, :] = out_t.T.astype(o_ref.dtype)
        return carry

    scores(0, 0, 0)
    lax.fori_loop(0, n_query_tiles, query_tile, 0)


def _attention(qt4, k3, vt4):
    bsz, seq, _ = k3.shape
    whole = lambda a: pl.BlockSpec((None,) + a.shape[1:], lambda b: (b,) + (0,) * (a.ndim - 1))
    return pl.pallas_call(
        _attn_kernel,
        grid=(bsz,),
        in_specs=[whole(qt4), whole(k3), whole(vt4)],
        out_specs=pl.BlockSpec((None, seq, MLA_HEADS * MLA_V), lambda b: (b, 0, 0)),
        out_shape=jax.ShapeDtypeStruct((bsz, seq, MLA_HEADS * MLA_V), BF16),
        scratch_shapes=[pltpu.VMEM((2, MLA_HEADS, ATT_TK, ATT_TQ), F32),
                        pltpu.VMEM((MLA_HEADS, 1, ATT_TQ), F32),
                        pltpu.VMEM((MLA_HEADS, MLA_V + ATT_ONES, ATT_TQ), F32)],
        compiler_params=_params(1),
        name="mla_attention",
    )(qt4, k3, vt4)


def _merge_kernel(x_ref, ya_ref, yb_ref, yc_ref, yd_ref, wg_ref, wb_ref, wo_ref, g_ref, b_ref, o_ref):
    def residual(rows):
        x = x_ref[rows, :]
        xb = x.astype(BF16)
        merged = None
        for n, y_ref in enumerate((ya_ref, yb_ref, yc_ref, yd_ref)):
            gate = jax.nn.sigmoid(_dot(xb, wg_ref[:, n * D_MODEL:(n + 1) * D_MODEL]))
            term = gate * _dot(y_ref[rows, :], wb_ref[n * BRANCH_WIDTH:(n + 1) * BRANCH_WIDTH, :])
            merged = term if merged is None else merged + term
        return ALPHA * x + _dot(merged.astype(BF16), wo_ref[...])

    _pipelined_rows(o_ref, g_ref, b_ref, residual)


def _merge(layer, x, ys, wg, wb, wo, g, b, casts=()):
    n = x.shape[0]
    steps = n // TM_WIDE
    row = lambda last: pl.BlockSpec((TM_WIDE, last), lambda i: (i, 0))
    cast_in, cast_out, cast_shapes = _cast_specs(casts, steps, lambda i: i)
    return pl.pallas_call(
        _with_casts(_merge_kernel, 10, 1, casts, steps),
        grid=(steps,),
        in_specs=[row(D_MODEL)] + [row(BRANCH_WIDTH)] * N_BRANCH
                 + [_const_spec(a.shape) for a in (wg, wb, wo)]
                 + [_layer_spec(a, layer) for a in (g, b)] + cast_in,
        out_specs=[row(D_MODEL)] + cast_out,
        out_shape=[jax.ShapeDtypeStruct((n, D_MODEL), F32)] + cast_shapes,
        compiler_params=_params(1),
        name="merge_ln",
    )(x, *ys, wg, wb, wo, g, b, *_cast_operands(casts))


def _rope_tables(positions):
    n_r, n_m = RET_KDIM // 2, MLA_ROPE // 2
    inv_r = ROPE_BASE ** (-jnp.arange(0, RET_KDIM, 2, dtype=F32) / RET_KDIM)
    inv_m = ROPE_BASE ** (-jnp.arange(0, MLA_ROPE, 2, dtype=F32) / MLA_ROPE)
    pad = jnp.zeros((HEAD_PAD - 2 * (n_r + n_m),), F32)
    ang = positions.astype(F32)[..., None] * jnp.concatenate([inv_r, inv_r, inv_m, inv_m, pad])
    lane = np.arange(HEAD_PAD)
    is_cos = jnp.asarray((lane < n_r) | ((lane >= 2 * n_r) & (lane < 2 * n_r + n_m)))
    compact = jnp.where(is_cos, jnp.cos(ang), jnp.sin(ang))
    expand = np.zeros((HEAD_PAD, 4 * HEAD_PAD), np.float32)
    for l in range(HEAD_PAD):
        expand[l % n_r, l] = 1.0
        expand[n_r + l % n_r, HEAD_PAD + l] = -1.0 if l % RET_KDIM < n_r else 1.0
        if l < MLA_ROPE:
            expand[2 * n_r + l % n_m, 2 * HEAD_PAD + l] = 1.0
            expand[2 * n_r + n_m + l % n_m, 3 * HEAD_PAD + l] = -1.0 if l < n_m else 1.0
    return compact, jnp.asarray(np.tile(expand, (3, 1)), BF16)


def _swap_cols(w):
    half = w.shape[-1] // 2
    return jnp.concatenate([w[..., half:], w[..., :half]], axis=-1)


def _mla_weights(w_uq, w_ukv):
    depth = w_uq.shape[0]
    uq = w_uq.reshape(depth, MLA_Q_RANK, MLA_HEADS, MLA_QK)
    q_nope, q_rope = uq[..., :MLA_NOPE], uq[..., MLA_NOPE:]
    zq = jnp.zeros((depth, MLA_Q_RANK, MLA_HEADS, HEAD_PAD - MLA_QK), w_uq.dtype)
    wq1 = jnp.concatenate([q_nope, q_rope, zq], axis=-1)
    wq2 = jnp.concatenate([jnp.zeros_like(q_nope), _swap_cols(q_rope), zq], axis=-1)
    ukv = w_ukv.reshape(depth, MLA_KV_RANK, MLA_HEADS, MLA_NOPE + MLA_V)
    k_nope, v = ukv[..., :MLA_NOPE], ukv[..., MLA_NOPE:]
    wk = jnp.concatenate([k_nope, jnp.zeros_like(k_nope)], axis=-1)
    wvt = jnp.swapaxes(v.reshape(depth, MLA_KV_RANK, MLA_HEADS * MLA_V), 1, 2)
    flat = lambda a: a.reshape(depth, a.shape[1], MLA_HEADS * HEAD_PAD).astype(BF16)
    return flat(wq1), flat(wq2), flat(wk), wvt.astype(BF16)


def kernel(x, p, positions, ffn1_up, ffn1_down, ln1_g, ln1_b, w_in, sgu_ln_g, sgu_ln_b, sgu_w, sgu_b,
           pool_w, pool_scale, mla_q_norm, mla_kv_norm, mla_w_uq, mla_w_ukv, w_branch, w_out,
           ln2_g, ln2_b, ffn2_up, ffn2_down, w_ple_gate, w_ple, ln3_g, ln3_b):
    bsz, seq, dm = x.shape
    n = bsz * seq
    rope = _rope_tables(positions)
    bf = lambda a: a.astype(BF16)
    rows = lambda a: a[:, None, :]

    up1, down1 = bf(ffn1_up[0]), bf(ffn1_down[0])
    w_in_t = jnp.swapaxes(w_in, 1, 2)
    w_branch2 = w_branch.reshape(DEPTH, N_BRANCH * BRANCH_WIDTH, dm)
    wp = bf(w_ple)
    wq1, wq2, wk, wvt = _mla_weights(mla_w_uq, mla_w_ukv)
    sgu_bias = jnp.repeat(jnp.swapaxes(sgu_b, 1, 2), SGU_WIDTH // SGU_GROUPS, axis=2)
    groups = len(POOL_WINDOWS)
    pool_bd = bf(jnp.einsum("lgcd,gh->lgchd", pool_w, jnp.eye(groups, dtype=pool_w.dtype))
                 .reshape(DEPTH, BRANCH_WIDTH, BRANCH_WIDTH))
    p2 = p.reshape(DEPTH, n, PLE_DIM)

    h = x.reshape(n, dm)
    for i in range(DEPTH):
        last = i + 1 == DEPTH
        h, w_mix, w_gate, wb, wo = _ffn_ln(
            i, h, up1, down1, rows(ln1_g), rows(ln1_b),
            (_cast(w_in_t, i, MIX_COLS), _CastT(w_in_t, i, _OFF_GATE, N_BRANCH * dm), _cast(w_branch2, i),
             _cast(w_out, i)))
        y_a, y_b, y_c, q, k, vt, up2, wpg, *nxt_up = _mixers(
            i, h.reshape(bsz, seq, dm), w_mix, rope,
            (rows(sgu_ln_g), rows(sgu_ln_b), sgu_w, sgu_bias), (pool_bd, rows(pool_scale)),
            (rows(mla_q_norm), rows(mla_kv_norm), wq1, wq2, wk, wvt),
            (_cast(ffn2_up, i), _cast(w_ple_gate, i)) + (() if last else (_cast(ffn1_up, i + 1),)))
        y_d = _attention(q, k, vt)
        ys = tuple(y.reshape(n, BRANCH_WIDTH) for y in (y_a, y_b, y_c, y_d))
        h, down2, *nxt_down = _merge(
            i, h, ys, w_gate, wb, wo, rows(ln2_g), rows(ln2_b),
            (_cast(ffn2_down, i),) + (() if last else (_cast(ffn1_down, i + 1),)))
        h = _ffn_ple_ln(i, h, p2, up2, down2, wpg, wp, rows(ln3_g), rows(ln3_b))
        if not last:
            (up1,), (down1,) = nxt_up, nxt_down
    return h.reshape(bsz, seq, dm)
```

```python
import math
from typing import NamedTuple

import numpy as np
import jax
import jax.numpy as jnp
from jax import lax
from jax.experimental import pallas as pl
from jax.experimental.pallas import tpu as pltpu

D_MODEL = 1024
DEPTH = 2
CHUNK = 64
SGU_WIDTH = 256
SGU_BLOCK = 128
SGU_GROUPS = 4
RET_HEADS = 4
RET_KDIM = 64
POOL_WINDOWS = (2, 4, 8, 16)
POOL_HALO = 16
POOL_PAD = 32
MLA_HEADS = 4
MLA_NOPE = 64
MLA_ROPE = 32
MLA_QK = MLA_NOPE + MLA_ROPE
MLA_V = 64
MLA_Q_RANK = 256
MLA_KV_RANK = 128
HEAD_PAD = 128
F32_SUBLANES = 8
ROPE_BASE = 10000.0
N_BRANCH = 4
BRANCH_WIDTH = 256
D_FF = 2816
PLE_DIM = 256
ALPHA = (2 * DEPTH) ** 0.25
LN_EPS = 1e-5
RMS_EPS = 1e-6
GN_EPS = 1e-5

_OFF_SGU = 0
_OFF_RET = 512
_OFF_POOL = 1536
_OFF_CQ = 1792
_OFF_GATE = 2208
MIX_COLS = 2304

TM = 512
MIX_TILES = 2
TM_WIDE = 1024
SUB = 256
ROPE_PIECES = 8
RET_TILE = 256
ATT_TQ = 512
ATT_TK = 256
ATT_ONES = 16
VMEM_LIMIT = 56 * 1024 * 1024

BF16 = jnp.bfloat16
F32 = jnp.float32


def _dot(a, b):
    return jnp.dot(a, b, preferred_element_type=F32)


def _dot_nt(a, b):
    return lax.dot_general(a, b, (((1,), (1,)), ((), ())), preferred_element_type=F32)


def _dot_tn(a, b):
    return lax.dot_general(a, b, (((0,), (0,)), ((), ())), preferred_element_type=F32)


def _layer_norm(r, g, b, eps):
    mu = jnp.mean(r, axis=-1, keepdims=True)
    d = r - mu
    var = jnp.mean(d * d, axis=-1, keepdims=True)
    return d * lax.rsqrt(var + eps) * g + b


def _const_spec(shape):
    zeros = (0,) * len(shape)
    return pl.BlockSpec(shape, lambda *_: zeros, pipeline_mode=pl.Buffered(1))


def _layer_spec(arr, layer):
    index = (layer,) + (0,) * (arr.ndim - 1)
    return pl.BlockSpec((None,) + arr.shape[1:], lambda *_: index, pipeline_mode=pl.Buffered(1))


def _params(n_grid, semantics="parallel"):
    return pltpu.CompilerParams(
        dimension_semantics=(semantics,) * n_grid, vmem_limit_bytes=VMEM_LIMIT)


class _Cast(NamedTuple):
    src: jax.Array
    layer: int
    rows: int


class _CastT(NamedTuple):
    src: jax.Array
    layer: int
    row0: int
    rows: int


def _cast(src, layer, rows=None):
    return _Cast(src, layer, src.shape[1] if rows is None else rows)


def _cast_specs(casts, n_steps, step_of):
    in_specs, out_specs, out_shapes = [], [], []
    for c in casts:
        blk, cols = c.rows // n_steps, c.src.shape[2]
        if isinstance(c, _Cast):
            in_specs.append(pl.BlockSpec((None, blk, cols), lambda *g, c=c: (c.layer, step_of(*g), 0)))
            out_specs.append(pl.BlockSpec((blk, cols), lambda *g: (step_of(*g), 0)))
            out_shapes.append(jax.ShapeDtypeStruct((c.rows, cols), BF16))
        else:
            first = c.row0 // blk
            in_specs += [pl.BlockSpec((None, blk, cols),
                                      lambda *g, c=c, k=k: (c.layer, first + step_of(*g) + k, 0))
                         for k in range(2)]
            out_specs.append(pl.BlockSpec((cols, blk), lambda *g: (0, step_of(*g))))
            out_shapes.append(jax.ShapeDtypeStruct((cols, c.rows), BF16))
    return in_specs, out_specs, out_shapes


def _cast_operands(casts):
    return [c.src for c in casts for _ in range(1 if isinstance(c, _Cast) else 2)]


def _with_casts(body, n_in, n_out, casts, n_steps):
    n_src = sum(1 if isinstance(c, _Cast) else 2 for c in casts)

    def kernel(*refs):
        ins, rest = refs[:n_in], refs[n_in:]
        cast_in, rest = list(rest[:n_src]), rest[n_src:]
        outs, rest = rest[:n_out], rest[n_out:]
        cast_out, scratch = rest[:len(casts)], rest[len(casts):]
        body(*ins, *outs, *scratch)
        for c, dst_ref in zip(casts, cast_out):
            if isinstance(c, _Cast):
                dst_ref[...] = cast_in.pop(0)[...].astype(BF16)
            else:
                lo_ref, hi_ref = cast_in.pop(0), cast_in.pop(0)
                off = c.row0 % (c.rows // n_steps)
                window = jnp.concatenate([lo_ref[off:, :], hi_ref[:off, :]], axis=0)
                dst_ref[...] = window.T.astype(BF16)

    return kernel


def _pipelined_rows(o_ref, g_ref, b_ref, residual_fn):
    pending = None
    for s in range(o_ref.shape[0] // SUB):
        rows = slice(s * SUB, (s + 1) * SUB)
        r = residual_fn(rows)
        if pending is not None:
            o_ref[pending[0], :] = _layer_norm(pending[1], g_ref[...], b_ref[...], LN_EPS)
        pending = (rows, r)
    o_ref[pending[0], :] = _layer_norm(pending[1], g_ref[...], b_ref[...], LN_EPS)


def _ffn_body(x, up_ref, down_ref):
    xb = x.astype(BF16)
    a = _dot(xb, up_ref[:, :D_FF])
    b = _dot(xb, up_ref[:, D_FF:])
    h = (a * jax.nn.sigmoid(a) * b).astype(BF16)
    return xb, _dot(h, down_ref[...])


def _ffn_ln_kernel(x_ref, up_ref, down_ref, g_ref, b_ref, o_ref):
    def residual(rows):
        x = x_ref[rows, :]
        _, y = _ffn_body(x, up_ref, down_ref)
        return ALPHA * x + 0.5 * y

    _pipelined_rows(o_ref, g_ref, b_ref, residual)


def _ffn_ple_ln_kernel(x_ref, p_ref, up_ref, down_ref, wpg_ref, wp_ref, g_ref, b_ref, o_ref):
    def residual(rows):
        x = x_ref[rows, :]
        xb, y = _ffn_body(x, up_ref, down_ref)
        ple = jax.nn.sigmoid(_dot(xb, wpg_ref[...])) * _dot(p_ref[rows, :].astype(BF16), wp_ref[...])
        return ALPHA * x + 0.5 * y + ple

    _pipelined_rows(o_ref, g_ref, b_ref, residual)


def _ffn_ln_rope_kernel(x_ref, up_ref, down_ref, g_ref, b_ref, pos_ref, freq_ref, o_ref, cs_ref):
    n_r, n_m = RET_KDIM // 2, MLA_ROPE // 2

    def residual(rows):
        x = x_ref[rows, :]
        xb = x.astype(BF16)
        a = _dot(xb, up_ref[:, :D_FF])
        b = _dot(xb, up_ref[:, D_FF:])
        piece = SUB // ROPE_PIECES
        for c in range(ROPE_PIECES):
            sub = slice(c * piece, (c + 1) * piece)
            anchor = (b if c % 2 else a)[sub, (c // 2) * 4 * HEAD_PAD:((c // 2) * 4 + 1) * HEAD_PAD]
            bits = lax.bitcast_convert_type(anchor, jnp.int32)
            zero = lax.shift_right_logical(lax.shift_right_logical(bits, 16), 16).astype(F32)
            here = slice(rows.start + c * piece, rows.start + (c + 1) * piece)
            ang = pos_ref[here, :].astype(F32) * freq_ref[...] + zero
            lane = lax.broadcasted_iota(jnp.int32, ang.shape, 1)
            is_cos = (lane < n_r) | ((lane >= 2 * n_r) & (lane < 2 * n_r + n_m))
            cs_ref[here, :] = jnp.where(is_cos, jnp.cos(ang), jnp.sin(ang))
        h = (a * jax.nn.sigmoid(a) * b).astype(BF16)
        return ALPHA * x + 0.5 * _dot(h, down_ref[...])

    _pipelined_rows(o_ref, g_ref, b_ref, residual)


def _ffn_ln(layer, x, up, down, g, b, casts=(), rope=None):
    n = x.shape[0]
    steps = n // TM_WIDE
    row = lambda last: pl.BlockSpec((TM_WIDE, last), lambda i: (i, 0))
    cast_in, cast_out, cast_shapes = _cast_specs(casts, steps, lambda i: i)
    in_specs = [row(D_MODEL), _const_spec(up.shape), _const_spec(down.shape), _layer_spec(g, layer),
                _layer_spec(b, layer)]
    out_specs, out_shape, operands = [row(D_MODEL)], [jax.ShapeDtypeStruct((n, D_MODEL), F32)], [x, up, down, g, b]
    body = _ffn_ln_kernel
    if rope is not None:
        body = _ffn_ln_rope_kernel
        in_specs += [row(1), _const_spec(rope[1].shape)]
        out_specs.append(row(HEAD_PAD))
        out_shape.append(jax.ShapeDtypeStruct((n, HEAD_PAD), F32))
        operands += list(rope)
    return pl.pallas_call(
        _with_casts(body, len(in_specs), len(out_specs), casts, steps),
        grid=(steps,),
        in_specs=in_specs + cast_in,
        out_specs=out_specs + cast_out,
        out_shape=out_shape + cast_shapes,
        compiler_params=_params(1),
        name="ffn_ln",
    )(*operands, *_cast_operands(casts))


def _ffn_ple_ln(layer, x, p, up, down, wpg, wp, g, b):
    n = x.shape[0]
    row = pl.BlockSpec((TM_WIDE, D_MODEL), lambda i: (i, 0))
    return pl.pallas_call(
        _ffn_ple_ln_kernel,
        grid=(n // TM_WIDE,),
        in_specs=[row, pl.BlockSpec((None, TM_WIDE, PLE_DIM), lambda i: (layer, i, 0)),
                  _const_spec(up.shape), _const_spec(down.shape), _const_spec(wpg.shape)]
                 + [_layer_spec(a, layer) for a in (wp, g, b)],
        out_specs=row,
        out_shape=jax.ShapeDtypeStruct((n, D_MODEL), F32),
        compiler_params=_params(1),
        name="ffn_ple_ln",
    )(x, p, up, down, wpg, wp, g, b)


def _swap_halves(x, half):
    width = x.shape[-1]
    lane = lax.broadcasted_iota(jnp.int32, x.shape, x.ndim - 1)
    first = (lane % (2 * half)) < half
    return jnp.where(first, pltpu.roll(x, width - half, x.ndim - 1), pltpu.roll(x, half, x.ndim - 1))


def _sgu_part(uv, lng_ref, lnb_ref, ws_ref, bias_ref, o_ref):
    u = jax.nn.gelu(uv[:, :SGU_WIDTH])
    v = _layer_norm(jax.nn.gelu(uv[:, SGU_WIDTH:]), lng_ref[...], lnb_ref[...], LN_EPS)
    vb = v.astype(BF16)
    t_row = lax.broadcasted_iota(jnp.int32, (SGU_BLOCK, SGU_BLOCK), 0)
    t_col = lax.broadcasted_iota(jnp.int32, (SGU_BLOCK, SGU_BLOCK), 1)
    causal = t_row >= t_col
    w_all = jnp.concatenate([jnp.where(causal, ws_ref[g], 0.0).astype(BF16) for g in range(SGU_GROUPS)], axis=1)
    group = lax.broadcasted_iota(jnp.int32, (SGU_BLOCK, SGU_WIDTH), 1) // (SGU_WIDTH // SGU_GROUPS)
    bias = bias_ref[...]
    for blk in range(TM // SGU_BLOCK):
        rows = slice(blk * SGU_BLOCK, (blk + 1) * SGU_BLOCK)
        v_blk = vb[rows]
        v_groups = jnp.concatenate([jnp.where(group == g, v_blk, 0.0) for g in range(SGU_GROUPS)], axis=0)
        o_ref[rows, :] = (u[rows] * (bias + _dot(w_all, v_groups))).astype(o_ref.dtype)


def _ret_part(proj, cos, sin, dmask_ref, rowdec_ref, keydec_ref, tiledec_ref, o_ref, state_ref):
    width = RET_HEADS * RET_KDIM
    q = proj[:, :width]
    k = proj[:, width:2 * width]
    q = q * cos + _swap_halves(q, RET_KDIM // 2) * sin
    k = (k * cos + _swap_halves(k, RET_KDIM // 2) * sin) * RET_KDIM ** -0.5
    v = proj[:, 2 * width:3 * width]
    gate = proj[:, 3 * width:]
    vb = v.astype(BF16)
    kb = k.astype(BF16)

    head = lax.broadcasted_iota(jnp.int32, (RET_TILE, width), 1) // RET_KDIM
    y = _dot(q.astype(BF16), state_ref[...].astype(BF16)) * rowdec_ref[...]
    for h in range(RET_HEADS):
        qh = jnp.where(head == h, q, 0.0).astype(BF16)
        scores = _dot_nt(qh, kb) * dmask_ref[h]
        y = y + jnp.where(head == h, _dot(scores.astype(BF16), vb), 0.0)

    kd = (k * keydec_ref[...]).astype(BF16)
    row_head = lax.broadcasted_iota(jnp.int32, (width, width), 0) // RET_KDIM
    col_head = lax.broadcasted_iota(jnp.int32, (width, width), 1) // RET_KDIM
    kv = jnp.where(row_head == col_head, _dot_tn(kd, vb), 0.0)
    state_ref[...] = state_ref[...] * tiledec_ref[...] + kv

    inv = 1.0 / RET_KDIM
    mu = jnp.zeros_like(y)
    for h in range(RET_HEADS):
        s = jnp.sum(jnp.where(head == h, y, 0.0), axis=1, keepdims=True) * inv
        mu = jnp.where(head == h, s, mu)
    d = y - mu
    var = jnp.zeros_like(y)
    for h in range(RET_HEADS):
        s = jnp.sum(jnp.where(head == h, d * d, 0.0), axis=1, keepdims=True) * inv
        var = jnp.where(head == h, s, var)
    yn = d * lax.rsqrt(var + GN_EPS)
    o_ref[...] = (gate * jax.nn.sigmoid(gate) * yn).astype(o_ref.dtype)


def _retention_tables():
    heads = np.arange(RET_HEADS, dtype=np.float64)
    log_gamma = np.log1p(-np.exp2(-5.0 - heads))
    t = np.arange(RET_TILE)
    chunk = t // CHUNK
    diff = (t[:, None] - t[None, :]).astype(np.float64)
    same = chunk[:, None] == chunk[None, :]
    earlier = chunk[None, :] < chunk[:, None]
    expo = np.where(same, np.abs(diff), diff)
    dmask = np.where((same | earlier)[None], np.exp(log_gamma[:, None, None] * expo[None]), 0.0)
    rowdec = np.exp(log_gamma[None, :] * (t[:, None] + 1.0))
    keydec = np.exp(log_gamma[None, :] * (RET_TILE - 1.0 - t[:, None]))
    tiledec = np.exp(log_gamma * RET_TILE)
    rep = lambda a: np.repeat(a, RET_KDIM, axis=-1)
    width = RET_HEADS * RET_KDIM
    tiledec_full = np.broadcast_to(rep(tiledec[None, :]).T, (width, width))
    return (jnp.asarray(dmask, F32), jnp.asarray(rep(rowdec), F32), jnp.asarray(rep(keydec), F32),
            jnp.asarray(tiledec_full, F32))


def _pool_part(z, wp_ref, scale_ref, o_ref, pool_ref, seq_tile):
    n_rows = POOL_PAD + TM
    pool_ref[0, POOL_PAD:, :] = z
    sums = []
    for level in range(len(POOL_WINDOWS)):
        lo, shift = F32_SUBLANES * (level + 1), 2 ** level
        s = pool_ref[level, lo:n_rows, :] + pool_ref[level, lo - shift:n_rows - shift, :]
        if level + 1 < len(POOL_WINDOWS):
            pool_ref[level + 1, lo:n_rows, :] = s
        sums.append(s[POOL_PAD - lo:])
    group = lax.broadcasted_iota(jnp.int32, (TM, BRANCH_WIDTH), 1) // (BRANCH_WIDTH // len(POOL_WINDOWS))
    win = sums[-1]
    for gi in range(len(POOL_WINDOWS) - 1):
        win = jnp.where(group == gi, sums[gi], win)
    t = seq_tile * TM + lax.broadcasted_iota(jnp.int32, (TM, BRANCH_WIDTH), 0)
    count = jnp.minimum(t + 1, jnp.left_shift(2, group)).astype(F32)
    pooled = win / count - z
    y = _dot(pooled.astype(BF16), wp_ref[...]) * scale_ref[...]
    o_ref[...] = y.astype(o_ref.dtype)
    pool_ref[0, POOL_PAD - POOL_HALO:POOL_PAD, :] = pool_ref[0, n_rows - POOL_HALO:n_rows, :]


def _rms_norm(x, g):
    return x * lax.rsqrt(jnp.mean(x * x, axis=-1, keepdims=True) + RMS_EPS) * g


def _mla_part(proj, cos, sin, qg_ref, kvg_ref, wq1_ref, wq2_ref, wk_ref, wvt_ref, qt_ref, k_ref, vt_ref):
    cq = _rms_norm(proj[:, :MLA_Q_RANK], qg_ref[...]).astype(BF16)
    ckv = _rms_norm(proj[:, MLA_Q_RANK:MLA_Q_RANK + MLA_KV_RANK], kvg_ref[...]).astype(BF16)
    k_raw = proj[:, MLA_Q_RANK + MLA_KV_RANK:]
    k_pe = pltpu.roll(k_raw * cos + _swap_halves(k_raw, MLA_ROPE // 2) * sin, MLA_NOPE, 1)
    ck = pltpu.roll(cos, MLA_NOPE, 1)
    sk = pltpu.roll(sin, MLA_NOPE, 1)
    lane = lax.broadcasted_iota(jnp.int32, ck.shape, 1)
    scale = MLA_QK ** -0.5 * math.log2(math.e)
    cq_tab = (ck + jnp.where(lane < MLA_NOPE, 1.0, 0.0)) * scale
    sq_tab = sk * scale
    tile4 = lambda a: jnp.concatenate([a] * MLA_HEADS, axis=1)
    q = _dot(cq, wq1_ref[...]) * tile4(cq_tab) + _dot(cq, wq2_ref[...]) * tile4(sq_tab)
    k = _dot(ckv, wk_ref[...]) + tile4(k_pe)
    for h in range(MLA_HEADS):
        rows = slice(h * HEAD_PAD, (h + 1) * HEAD_PAD)
        qt_ref[rows, :] = q[:, rows].T.astype(qt_ref.dtype)
    k_ref[...] = k.astype(k_ref.dtype)
    for t in range(TM // ATT_TK):
        vt_ref[t] = _dot_nt(wvt_ref[...], ckv[t * ATT_TK:(t + 1) * ATT_TK]).astype(vt_ref.dtype)


def _mixers_kernel(x_ref, w_ref, cs_ref, expand_ref,
                   lng_ref, lnb_ref, ws_ref, bias_ref, wp_ref, scale_ref,
                   qg_ref, kvg_ref, wq1_ref, wq2_ref, wk_ref, wvt_ref,
                   dmask_ref, rowdec_ref, keydec_ref, tiledec_ref,
                   ya_ref, yb_ref, yc_ref, qt_ref, k_ref, vt_ref, state_ref, pool_ref):
    @pl.when(pl.program_id(1) == 0)
    def _():
        state_ref[...] = jnp.zeros_like(state_ref)
        pool_ref[0, 0:POOL_PAD, :] = jnp.zeros((POOL_PAD, BRANCH_WIDTH), F32)

    values_per_tile = TM // ATT_TK
    for t in range(x_ref.shape[0] // TM):
        rows = slice(t * TM, (t + 1) * TM)
        _mixers_tile(x_ref.at[rows, :], w_ref, cs_ref.at[rows, :], expand_ref,
                     lng_ref, lnb_ref, ws_ref, bias_ref, wp_ref, scale_ref,
                     qg_ref, kvg_ref, wq1_ref, wq2_ref, wk_ref, wvt_ref,
                     dmask_ref, rowdec_ref, keydec_ref, tiledec_ref,
                     ya_ref.at[rows, :], yb_ref.at[rows, :], yc_ref.at[rows, :], qt_ref.at[t],
                     k_ref.at[rows, :], vt_ref.at[t * values_per_tile:(t + 1) * values_per_tile],
                     state_ref, pool_ref, pl.program_id(1) * (x_ref.shape[0] // TM) + t)


def _mixers_tile(x_ref, w_ref, cs_ref, expand_ref,
                 lng_ref, lnb_ref, ws_ref, bias_ref, wp_ref, scale_ref,
                 qg_ref, kvg_ref, wq1_ref, wq2_ref, wk_ref, wvt_ref,
                 dmask_ref, rowdec_ref, keydec_ref, tiledec_ref,
                 ya_ref, yb_ref, yc_ref, qt_ref, k_ref, vt_ref, state_ref, pool_ref, seq_tile):
    xb = x_ref[...].astype(BF16)
    p_sgu = _dot_nt(xb, w_ref[_OFF_SGU:_OFF_RET, :])
    p_pool = _dot_nt(xb, w_ref[_OFF_POOL:_OFF_CQ, :])
    p_mla = _dot_nt(xb, w_ref[_OFF_CQ:MIX_COLS, :])
    p_ret = _dot_nt(xb, w_ref[_OFF_RET:_OFF_POOL, :])
    cs = cs_ref[...]
    hi = cs.astype(BF16)
    rest = cs - hi.astype(F32)
    mid = rest.astype(BF16)
    lo = (rest - mid.astype(F32)).astype(BF16)
    rope = _dot(jnp.concatenate([hi, mid, lo], axis=1), expand_ref[...])
    ret_cos, ret_sin, mla_cos, mla_sin = (rope[:, t * HEAD_PAD:(t + 1) * HEAD_PAD] for t in range(4))
    _sgu_part(p_sgu, lng_ref, lnb_ref, ws_ref, bias_ref, ya_ref)
    _pool_part(p_pool, wp_ref, scale_ref, yc_ref, pool_ref, seq_tile)
    _mla_part(p_mla, mla_cos, mla_sin, qg_ref, kvg_ref, wq1_ref, wq2_ref, wk_ref, wvt_ref,
              qt_ref, k_ref, vt_ref)
    for r in range(TM // RET_TILE):
        rows = slice(r * RET_TILE, (r + 1) * RET_TILE)
        cos = jnp.concatenate([ret_cos[rows], ret_cos[rows]], axis=1)
        sin = jnp.concatenate([ret_sin[rows], ret_sin[rows]], axis=1)
        _ret_part(p_ret[rows], cos, sin, dmask_ref, rowdec_ref, keydec_ref, tiledec_ref,
                  yb_ref.at[rows, :], state_ref)


def _mixers(layer, x3, w_mix, rope, sgu_params, pool_params, mla_params, casts=()):
    bsz, seq, _ = x3.shape
    width = RET_HEADS * RET_KDIM
    ret_tables = _retention_tables()
    block = TM * MIX_TILES
    seq_tiles = seq // block
    tile = lambda last: pl.BlockSpec((None, block, last), lambda b, s: (b, s, 0))
    layered = tuple(sgu_params) + tuple(pool_params) + tuple(mla_params)
    in_specs = ([tile(D_MODEL), _const_spec(w_mix.shape), tile(HEAD_PAD), _const_spec(rope[1].shape)]
                + [_layer_spec(a, layer) for a in layered]
                + [_const_spec(a.shape) for a in ret_tables])
    cast_in, cast_out, cast_shapes = _cast_specs(casts, bsz * seq_tiles, lambda b, s: b * seq_tiles + s)
    branch = jax.ShapeDtypeStruct((bsz, seq, BRANCH_WIDTH), BF16)
    assert TM == ATT_TQ
    qt = jax.ShapeDtypeStruct((bsz, seq // TM, MLA_HEADS * HEAD_PAD, TM), BF16)
    kk = jax.ShapeDtypeStruct((bsz, seq, MLA_HEADS * HEAD_PAD), BF16)
    vt = jax.ShapeDtypeStruct((bsz, seq // ATT_TK, MLA_HEADS * MLA_V, ATT_TK), BF16)
    return pl.pallas_call(
        _with_casts(_mixers_kernel, len(in_specs), 6, casts, bsz * seq_tiles),
        grid=(bsz, seq_tiles),
        in_specs=in_specs + cast_in,
        out_specs=[tile(BRANCH_WIDTH)] * 3
                  + [pl.BlockSpec((None, MIX_TILES, MLA_HEADS * HEAD_PAD, TM), lambda b, s: (b, s, 0, 0)),
                     tile(MLA_HEADS * HEAD_PAD),
                     pl.BlockSpec((None, block // ATT_TK, MLA_HEADS * MLA_V, ATT_TK), lambda b, s: (b, s, 0, 0))]
                  + cast_out,
        out_shape=[branch, branch, branch, qt, kk, vt] + cast_shapes,
        scratch_shapes=[pltpu.VMEM((width, width), F32),
                        pltpu.VMEM((len(POOL_WINDOWS), POOL_PAD + TM, BRANCH_WIDTH), F32)],
        compiler_params=_params(2, "arbitrary"),
        name="token_mixers",
    )(x3, w_mix, *rope, *sgu_params, *pool_params, *mla_params, *ret_tables, *_cast_operands(casts))


def _attn_kernel(qt_ref, k_ref, vt_ref, o_ref, st_ref, m_ref, acc_ref):
    key_chunk = lax.broadcasted_iota(jnp.int32, (ATT_TK, ATT_TQ), 0) // CHUNK
    qry_chunk = lax.broadcasted_iota(jnp.int32, (ATT_TK, ATT_TQ), 1) // CHUNK
    diag_masks = [key_chunk + t * (ATT_TK // CHUNK) <= qry_chunk for t in range(ATT_TQ // ATT_TK)]
    heads = range(MLA_HEADS)
    every_query = slice(0, ATT_TQ)
    upper = slice(ATT_TK, ATT_TQ)
    ones = jnp.ones((ATT_ONES, ATT_TK), BF16)
    n_query_tiles = qt_ref.shape[0]

    def scores(qi, j, slot, queries=every_query, hs=heads):
        rows = pl.ds(pl.multiple_of(j * ATT_TK, ATT_TK), ATT_TK)
        for h in hs:
            cols = slice(h * HEAD_PAD, (h + 1) * HEAD_PAD)
            st_ref[slot, h, :, queries] = _dot(k_ref[rows, cols], qt_ref[qi, cols, queries])

    def softmax_pv(j, slot, mask=None, queries=every_query, hs=heads):
        for h in hs:
            load = lambda: (st_ref[slot, h, :, queries] if mask is None
                            else jnp.where(mask[:, queries], st_ref[slot, h, :, queries], -jnp.inf))
            m = m_ref[h, :, queries]
            m_new = jnp.maximum(m, jnp.max(load(), axis=0, keepdims=True))
            p = jnp.exp2((load() - m_new).astype(BF16))
            lhs = jnp.concatenate([vt_ref[j, h * MLA_V:(h + 1) * MLA_V, :], ones], axis=0)
            acc_ref[h, :, queries] = jnp.exp2(m - m_new) * acc_ref[h, :, queries] + _dot(lhs, p)
            m_ref[h, :, queries] = m_new

    def query_tile(qi, carry):
        def pair(i, carry):
            for h in heads:
                scores(qi, 2 * i + 1, 1, hs=(h,))
                softmax_pv(2 * i, 0, hs=(h,))
            for h in heads:
                scores(qi, 2 * i + 2, 0, hs=(h,))
                softmax_pv(2 * i + 1, 1, hs=(h,))
            return carry

        m_ref[...] = jnp.full(m_ref.shape, -jnp.inf, F32)
        acc_ref[...] = jnp.zeros(acc_ref.shape, F32)
        lax.fori_loop(0, qi, pair, 0)
        for h in heads:
            scores(qi, 2 * qi + 1, 1, upper, hs=(h,))
            softmax_pv(2 * qi, 0, diag_masks[0], hs=(h,))
        softmax_pv(2 * qi + 1, 1, diag_masks[1], upper)
        scores(jnp.minimum(qi + 1, n_query_tiles - 1), 0, 0)
        out_t = jnp.concatenate([acc_ref[h, :MLA_V] / acc_ref[h, MLA_V:MLA_V + 1] for h in heads], axis=0)
        rows = pl.ds(pl.multiple_of(qi * ATT_TQ, ATT_TQ), ATT_TQ)
        o_ref[rows, :] = out_t.T.astype(o_ref.dtype)
        return carry

    scores(0, 0, 0)
    lax.fori_loop(0, n_query_tiles, query_tile, 0)


def _attention(qt4, k3, vt4):
    bsz, seq, _ = k3.shape
    whole = lambda a: pl.BlockSpec((None,) + a.shape[1:], lambda b: (b,) + (0,) * (a.ndim - 1))
    return pl.pallas_call(
        _attn_kernel,
        grid=(bsz,),
        in_specs=[whole(qt4), whole(k3), whole(vt4)],
        out_specs=pl.BlockSpec((None, seq, MLA_HEADS * MLA_V), lambda b: (b, 0, 0)),
        out_shape=jax.ShapeDtypeStruct((bsz, seq, MLA_HEADS * MLA_V), BF16),
        scratch_shapes=[pltpu.VMEM((2, MLA_HEADS, ATT_TK, ATT_TQ), F32),
                        pltpu.VMEM((MLA_HEADS, 1, ATT_TQ), F32),
                        pltpu.VMEM((MLA_HEADS, MLA_V + ATT_ONES, ATT_TQ), F32)],
        compiler_params=_params(1),
        name="mla_attention",
    )(qt4, k3, vt4)


def _merge_kernel(x_ref, ya_ref, yb_ref, yc_ref, yd_ref, wg_ref, wb_ref, wo_ref, g_ref, b_ref, o_ref):
    def residual(rows):
        x = x_ref[rows, :]
        xb = x.astype(BF16)
        merged = None
        for n, y_ref in enumerate((ya_ref, yb_ref, yc_ref, yd_ref)):
            gate = jax.nn.sigmoid(_dot(xb, wg_ref[:, n * D_MODEL:(n + 1) * D_MODEL]))
            term = gate * _dot(y_ref[rows, :], wb_ref[n * BRANCH_WIDTH:(n + 1) * BRANCH_WIDTH, :])
            merged = term if merged is None else merged + term
        return ALPHA * x + _dot(merged.astype(BF16), wo_ref[...])

    _pipelined_rows(o_ref, g_ref, b_ref, residual)


def _merge(layer, x, ys, wg, wb, wo, g, b, casts=()):
    n = x.shape[0]
    steps = n // TM_WIDE
    row = lambda last: pl.BlockSpec((TM_WIDE, last), lambda i: (i, 0))
    cast_in, cast_out, cast_shapes = _cast_specs(casts, steps, lambda i: i)
    return pl.pallas_call(
        _with_casts(_merge_kernel, 10, 1, casts, steps),
        grid=(steps,),
        in_specs=[row(D_MODEL)] + [row(BRANCH_WIDTH)] * N_BRANCH
                 + [_const_spec(a.shape) for a in (wg, wb, wo)]
                 + [_layer_spec(a, layer) for a in (g, b)] + cast_in,
        out_specs=[row(D_MODEL)] + cast_out,
        out_shape=[jax.ShapeDtypeStruct((n, D_MODEL), F32)] + cast_shapes,
        compiler_params=_params(1),
        name="merge_ln",
    )(x, *ys, wg, wb, wo, g, b, *_cast_operands(casts))


def _rope_constants():
    n_r, n_m = RET_KDIM // 2, MLA_ROPE // 2
    inv_r = ROPE_BASE ** (-jnp.arange(0, RET_KDIM, 2, dtype=F32) / RET_KDIM)
    inv_m = ROPE_BASE ** (-jnp.arange(0, MLA_ROPE, 2, dtype=F32) / MLA_ROPE)
    pad = jnp.zeros((HEAD_PAD - 2 * (n_r + n_m),), F32)
    freq = jnp.concatenate([inv_r, inv_r, inv_m, inv_m, pad])[None, :]
    expand = np.zeros((HEAD_PAD, 4 * HEAD_PAD), np.float32)
    for l in range(HEAD_PAD):
        expand[l % n_r, l] = 1.0
        expand[n_r + l % n_r, HEAD_PAD + l] = -1.0 if l % RET_KDIM < n_r else 1.0
        if l < MLA_ROPE:
            expand[2 * n_r + l % n_m, 2 * HEAD_PAD + l] = 1.0
            expand[2 * n_r + n_m + l % n_m, 3 * HEAD_PAD + l] = -1.0 if l < n_m else 1.0
    return freq, jnp.asarray(np.tile(expand, (3, 1)), BF16)


def _swap_cols(w):
    half = w.shape[-1] // 2
    return jnp.concatenate([w[..., half:], w[..., :half]], axis=-1)


def _mla_weights(w_uq, w_ukv):
    depth = w_uq.shape[0]
    uq = w_uq.reshape(depth, MLA_Q_RANK, MLA_HEADS, MLA_QK)
    q_nope, q_rope = uq[..., :MLA_NOPE], uq[..., MLA_NOPE:]
    zq = jnp.zeros((depth, MLA_Q_RANK, MLA_HEADS, HEAD_PAD - MLA_QK), w_uq.dtype)
    wq1 = jnp.concatenate([q_nope, q_rope, zq], axis=-1)
    wq2 = jnp.concatenate([jnp.zeros_like(q_nope), _swap_cols(q_rope), zq], axis=-1)
    ukv = w_ukv.reshape(depth, MLA_KV_RANK, MLA_HEADS, MLA_NOPE + MLA_V)
    k_nope, v = ukv[..., :MLA_NOPE], ukv[..., MLA_NOPE:]
    wk = jnp.concatenate([k_nope, jnp.zeros_like(k_nope)], axis=-1)
    wvt = jnp.swapaxes(v.reshape(depth, MLA_KV_RANK, MLA_HEADS * MLA_V), 1, 2)
    flat = lambda a: a.reshape(depth, a.shape[1], MLA_HEADS * HEAD_PAD).astype(BF16)
    return flat(wq1), flat(wq2), flat(wk), wvt.astype(BF16)


def kernel(x, p, positions, ffn1_up, ffn1_down, ln1_g, ln1_b, w_in, sgu_ln_g, sgu_ln_b, sgu_w, sgu_b,
           pool_w, pool_scale, mla_q_norm, mla_kv_norm, mla_w_uq, mla_w_ukv, w_branch, w_out,
           ln2_g, ln2_b, ffn2_up, ffn2_down, w_ple_gate, w_ple, ln3_g, ln3_b):
    bsz, seq, dm = x.shape
    n = bsz * seq
    freq, expand = _rope_constants()
    compact = None
    bf = lambda a: a.astype(BF16)
    rows = lambda a: a[:, None, :]

    up1, down1 = bf(ffn1_up[0]), bf(ffn1_down[0])
    w_in_t = jnp.swapaxes(w_in, 1, 2)
    w_branch2 = w_branch.reshape(DEPTH, N_BRANCH * BRANCH_WIDTH, dm)
    wp = bf(w_ple)
    wq1, wq2, wk, wvt = _mla_weights(mla_w_uq, mla_w_ukv)
    sgu_bias = jnp.repeat(jnp.swapaxes(sgu_b, 1, 2), SGU_WIDTH // SGU_GROUPS, axis=2)
    groups = len(POOL_WINDOWS)
    pool_bd = bf(jnp.einsum("lgcd,gh->lgchd", pool_w, jnp.eye(groups, dtype=pool_w.dtype))
                 .reshape(DEPTH, BRANCH_WIDTH, BRANCH_WIDTH))
    p2 = p.reshape(DEPTH, n, PLE_DIM)

    h = x.reshape(n, dm)
    for i in range(DEPTH):
        last = i + 1 == DEPTH
        h, *made = _ffn_ln(
            i, h, up1, down1, rows(ln1_g), rows(ln1_b),
            (_cast(w_in_t, i, MIX_COLS), _CastT(w_in_t, i, _OFF_GATE, N_BRANCH * dm), _cast(w_branch2, i),
             _cast(w_out, i)),
            rope=(positions.reshape(n, 1), freq) if compact is None else None)
        if compact is None:
            compact = made.pop(0).reshape(bsz, seq, HEAD_PAD)
        w_mix, w_gate, wb, wo = made
        y_a, y_b, y_c, q, k, vt, up2, wpg, *nxt_up = _mixers(
            i, h.reshape(bsz, seq, dm), w_mix, (compact, expand),
            (rows(sgu_ln_g), rows(sgu_ln_b), sgu_w, sgu_bias), (pool_bd, rows(pool_scale)),
            (rows(mla_q_norm), rows(mla_kv_norm), wq1, wq2, wk, wvt),
            (_cast(ffn2_up, i), _cast(w_ple_gate, i)) + (() if last else (_cast(ffn1_up, i + 1),)))
        y_d = _attention(q, k, vt)
        ys = tuple(y.reshape(n, BRANCH_WIDTH) for y in (y_a, y_b, y_c, y_d))
        h, down2, *nxt_down = _merge(
            i, h, ys, w_gate, wb, wo, rows(ln2_g), rows(ln2_b),
            (_cast(ffn2_down, i),) + (() if last else (_cast(ffn1_down, i + 1),)))
        h = _ffn_ple_ln(i, h, p2, up2, down2, wpg, wp, rows(ln3_g), rows(ln3_b))
        if not last:
            (up1,), (down1,) = nxt_up, nxt_down
    return h.reshape(bsz, seq, dm)
```

```python
import math
from typing import NamedTuple

import numpy as np
import jax
import jax.numpy as jnp
from jax import lax
from jax.experimental import pallas as pl
from jax.experimental.pallas import tpu as pltpu

D_MODEL = 1024
DEPTH = 2
CHUNK = 64
SGU_WIDTH = 256
SGU_BLOCK = 128
SGU_GROUPS = 4
RET_HEADS = 4
RET_KDIM = 64
POOL_WINDOWS = (2, 4, 8, 16)
POOL_HALO = 16
POOL_PAD = 32
MLA_HEADS = 4
MLA_NOPE = 64
MLA_ROPE = 32
MLA_QK = MLA_NOPE + MLA_ROPE
MLA_V = 64
MLA_Q_RANK = 256
MLA_KV_RANK = 128
HEAD_PAD = 128
F32_SUBLANES = 8
ROPE_BASE = 10000.0
N_BRANCH = 4
BRANCH_WIDTH = 256
D_FF = 2816
PLE_DIM = 256
ALPHA = (2 * DEPTH) ** 0.25
LN_EPS = 1e-5
RMS_EPS = 1e-6
GN_EPS = 1e-5

_OFF_SGU = 0
_OFF_RET = 512
_OFF_POOL = 1536
_OFF_CQ = 1792
_OFF_GATE = 2208
MIX_COLS = 2304

TM = 512
MIX_TILES = 2
TM_WIDE = 1024
SUB = 256
RET_TILE = 256
ATT_TQ = 512
ATT_TK = 256
ATT_ONES = 16
VMEM_LIMIT = 56 * 1024 * 1024

BF16 = jnp.bfloat16
F32 = jnp.float32


def _dot(a, b):
    return jnp.dot(a, b, preferred_element_type=F32)


def _dot_nt(a, b):
    return lax.dot_general(a, b, (((1,), (1,)), ((), ())), preferred_element_type=F32)


def _dot_tn(a, b):
    return lax.dot_general(a, b, (((0,), (0,)), ((), ())), preferred_element_type=F32)


def _layer_norm(r, g, b, eps):
    mu = jnp.mean(r, axis=-1, keepdims=True)
    d = r - mu
    var = jnp.mean(d * d, axis=-1, keepdims=True)
    return d * lax.rsqrt(var + eps) * g + b


def _const_spec(shape):
    zeros = (0,) * len(shape)
    return pl.BlockSpec(shape, lambda *_: zeros, pipeline_mode=pl.Buffered(1))


def _layer_spec(arr, layer):
    index = (layer,) + (0,) * (arr.ndim - 1)
    return pl.BlockSpec((None,) + arr.shape[1:], lambda *_: index, pipeline_mode=pl.Buffered(1))


def _params(n_grid, semantics="parallel"):
    return pltpu.CompilerParams(
        dimension_semantics=(semantics,) * n_grid, vmem_limit_bytes=VMEM_LIMIT)


class _Cast(NamedTuple):
    src: jax.Array
    layer: int
    rows: int


class _CastT(NamedTuple):
    src: jax.Array
    layer: int
    row0: int
    rows: int


def _cast(src, layer, rows=None):
    return _Cast(src, layer, src.shape[1] if rows is None else rows)


def _cast_specs(casts, n_steps, step_of):
    in_specs, out_specs, out_shapes = [], [], []
    for c in casts:
        blk, cols = c.rows // n_steps, c.src.shape[2]
        if isinstance(c, _Cast):
            in_specs.append(pl.BlockSpec((None, blk, cols), lambda *g, c=c: (c.layer, step_of(*g), 0)))
            out_specs.append(pl.BlockSpec((blk, cols), lambda *g: (step_of(*g), 0)))
            out_shapes.append(jax.ShapeDtypeStruct((c.rows, cols), BF16))
        else:
            first = c.row0 // blk
            in_specs += [pl.BlockSpec((None, blk, cols),
                                      lambda *g, c=c, k=k: (c.layer, first + step_of(*g) + k, 0))
                         for k in range(2)]
            out_specs.append(pl.BlockSpec((cols, blk), lambda *g: (0, step_of(*g))))
            out_shapes.append(jax.ShapeDtypeStruct((cols, c.rows), BF16))
    return in_specs, out_specs, out_shapes


def _cast_operands(casts):
    return [c.src for c in casts for _ in range(1 if isinstance(c, _Cast) else 2)]


def _with_casts(body, n_in, n_out, casts, n_steps):
    n_src = sum(1 if isinstance(c, _Cast) else 2 for c in casts)

    def kernel(*refs):
        ins, rest = refs[:n_in], refs[n_in:]
        cast_in, rest = list(rest[:n_src]), rest[n_src:]
        outs, rest = rest[:n_out], rest[n_out:]
        cast_out, scratch = rest[:len(casts)], rest[len(casts):]
        body(*ins, *outs, *scratch)
        for c, dst_ref in zip(casts, cast_out):
            if isinstance(c, _Cast):
                dst_ref[...] = cast_in.pop(0)[...].astype(BF16)
            else:
                lo_ref, hi_ref = cast_in.pop(0), cast_in.pop(0)
                off = c.row0 % (c.rows // n_steps)
                window = jnp.concatenate([lo_ref[off:, :], hi_ref[:off, :]], axis=0)
                dst_ref[...] = window.T.astype(BF16)

    return kernel


def _pipelined_rows(o_ref, g_ref, b_ref, residual_fn):
    pending = None
    for s in range(o_ref.shape[0] // SUB):
        rows = slice(s * SUB, (s + 1) * SUB)
        r = residual_fn(rows)
        if pending is not None:
            o_ref[pending[0], :] = _layer_norm(pending[1], g_ref[...], b_ref[...], LN_EPS)
        pending = (rows, r)
    o_ref[pending[0], :] = _layer_norm(pending[1], g_ref[...], b_ref[...], LN_EPS)


def _ffn_body(x, up_ref, down_ref):
    xb = x.astype(BF16)
    a = _dot(xb, up_ref[:, :D_FF])
    b = _dot(xb, up_ref[:, D_FF:])
    h = (a * jax.nn.sigmoid(a) * b).astype(BF16)
    return xb, _dot(h, down_ref[...])


def _ffn_ln_kernel(x_ref, up_ref, down_ref, g_ref, b_ref, o_ref):
    def residual(rows):
        x = x_ref[rows, :]
        _, y = _ffn_body(x, up_ref, down_ref)
        return ALPHA * x + 0.5 * y

    _pipelined_rows(o_ref, g_ref, b_ref, residual)


def _ffn_ple_ln_kernel(x_ref, p_ref, up_ref, down_ref, wpg_ref, wp_ref, g_ref, b_ref, o_ref):
    def residual(rows):
        x = x_ref[rows, :]
        xb, y = _ffn_body(x, up_ref, down_ref)
        ple = jax.nn.sigmoid(_dot(xb, wpg_ref[...])) * _dot(p_ref[rows, :].astype(BF16), wp_ref[...])
        return ALPHA * x + 0.5 * y + ple

    _pipelined_rows(o_ref, g_ref, b_ref, residual)


def _ffn_ln(layer, x, up, down, g, b, casts=()):
    n = x.shape[0]
    steps = n // TM_WIDE
    row = pl.BlockSpec((TM_WIDE, D_MODEL), lambda i: (i, 0))
    cast_in, cast_out, cast_shapes = _cast_specs(casts, steps, lambda i: i)
    return pl.pallas_call(
        _with_casts(_ffn_ln_kernel, 5, 1, casts, steps),
        grid=(steps,),
        in_specs=[row, _const_spec(up.shape), _const_spec(down.shape), _layer_spec(g, layer),
                  _layer_spec(b, layer)] + cast_in,
        out_specs=[row] + cast_out,
        out_shape=[jax.ShapeDtypeStruct((n, D_MODEL), F32)] + cast_shapes,
        compiler_params=_params(1),
        name="ffn_ln",
    )(x, up, down, g, b, *_cast_operands(casts))


def _ffn_ple_ln(layer, x, p, up, down, wpg, wp, g, b):
    n = x.shape[0]
    row = pl.BlockSpec((TM_WIDE, D_MODEL), lambda i: (i, 0))
    return pl.pallas_call(
        _ffn_ple_ln_kernel,
        grid=(n // TM_WIDE,),
        in_specs=[row, pl.BlockSpec((None, TM_WIDE, PLE_DIM), lambda i: (layer, i, 0)),
                  _const_spec(up.shape), _const_spec(down.shape), _const_spec(wpg.shape)]
                 + [_layer_spec(a, layer) for a in (wp, g, b)],
        out_specs=row,
        out_shape=jax.ShapeDtypeStruct((n, D_MODEL), F32),
        compiler_params=_params(1),
        name="ffn_ple_ln",
    )(x, p, up, down, wpg, wp, g, b)


def _swap_halves(x, half):
    width = x.shape[-1]
    lane = lax.broadcasted_iota(jnp.int32, x.shape, x.ndim - 1)
    first = (lane % (2 * half)) < half
    return jnp.where(first, pltpu.roll(x, width - half, x.ndim - 1), pltpu.roll(x, half, x.ndim - 1))


def _sgu_part(uv, lng_ref, lnb_ref, ws_ref, bias_ref, o_ref):
    u = jax.nn.gelu(uv[:, :SGU_WIDTH])
    v = _layer_norm(jax.nn.gelu(uv[:, SGU_WIDTH:]), lng_ref[...], lnb_ref[...], LN_EPS)
    vb = v.astype(BF16)
    t_row = lax.broadcasted_iota(jnp.int32, (SGU_BLOCK, SGU_BLOCK), 0)
    t_col = lax.broadcasted_iota(jnp.int32, (SGU_BLOCK, SGU_BLOCK), 1)
    causal = t_row >= t_col
    w_all = jnp.concatenate([jnp.where(causal, ws_ref[g], 0.0).astype(BF16) for g in range(SGU_GROUPS)], axis=1)
    group = lax.broadcasted_iota(jnp.int32, (SGU_BLOCK, SGU_WIDTH), 1) // (SGU_WIDTH // SGU_GROUPS)
    bias = bias_ref[...]
    for blk in range(TM // SGU_BLOCK):
        rows = slice(blk * SGU_BLOCK, (blk + 1) * SGU_BLOCK)
        v_blk = vb[rows]
        v_groups = jnp.concatenate([jnp.where(group == g, v_blk, 0.0) for g in range(SGU_GROUPS)], axis=0)
        o_ref[rows, :] = (u[rows] * (bias + _dot(w_all, v_groups))).astype(o_ref.dtype)


def _ret_part(proj, cos, sin, dmask_ref, rowdec_ref, keydec_ref, tiledec_ref, o_ref, state_ref):
    width = RET_HEADS * RET_KDIM
    q = proj[:, :width]
    k = proj[:, width:2 * width]
    q = q * cos + _swap_halves(q, RET_KDIM // 2) * sin
    k = (k * cos + _swap_halves(k, RET_KDIM // 2) * sin) * RET_KDIM ** -0.5
    v = proj[:, 2 * width:3 * width]
    gate = proj[:, 3 * width:]
    vb = v.astype(BF16)
    kb = k.astype(BF16)

    head = lax.broadcasted_iota(jnp.int32, (RET_TILE, width), 1) // RET_KDIM
    y = _dot(q.astype(BF16), state_ref[...].astype(BF16)) * rowdec_ref[...]
    for h in range(RET_HEADS):
        qh = jnp.where(head == h, q, 0.0).astype(BF16)
        scores = _dot_nt(qh, kb) * dmask_ref[h]
        y = y + jnp.where(head == h, _dot(scores.astype(BF16), vb), 0.0)

    kd = (k * keydec_ref[...]).astype(BF16)
    row_head = lax.broadcasted_iota(jnp.int32, (width, width), 0) // RET_KDIM
    col_head = lax.broadcasted_iota(jnp.int32, (width, width), 1) // RET_KDIM
    kv = jnp.where(row_head == col_head, _dot_tn(kd, vb), 0.0)
    state_ref[...] = state_ref[...] * tiledec_ref[...] + kv

    inv = 1.0 / RET_KDIM
    mu = jnp.zeros_like(y)
    for h in range(RET_HEADS):
        s = jnp.sum(jnp.where(head == h, y, 0.0), axis=1, keepdims=True) * inv
        mu = jnp.where(head == h, s, mu)
    d = y - mu
    var = jnp.zeros_like(y)
    for h in range(RET_HEADS):
        s = jnp.sum(jnp.where(head == h, d * d, 0.0), axis=1, keepdims=True) * inv
        var = jnp.where(head == h, s, var)
    yn = d * lax.rsqrt(var + GN_EPS)
    o_ref[...] = (gate * jax.nn.sigmoid(gate) * yn).astype(o_ref.dtype)


def _retention_tables():
    heads = np.arange(RET_HEADS, dtype=np.float64)
    log_gamma = np.log1p(-np.exp2(-5.0 - heads))
    t = np.arange(RET_TILE)
    chunk = t // CHUNK
    diff = (t[:, None] - t[None, :]).astype(np.float64)
    same = chunk[:, None] == chunk[None, :]
    earlier = chunk[None, :] < chunk[:, None]
    expo = np.where(same, np.abs(diff), diff)
    dmask = np.where((same | earlier)[None], np.exp(log_gamma[:, None, None] * expo[None]), 0.0)
    rowdec = np.exp(log_gamma[None, :] * (t[:, None] + 1.0))
    keydec = np.exp(log_gamma[None, :] * (RET_TILE - 1.0 - t[:, None]))
    tiledec = np.exp(log_gamma * RET_TILE)
    rep = lambda a: np.repeat(a, RET_KDIM, axis=-1)
    width = RET_HEADS * RET_KDIM
    tiledec_full = np.broadcast_to(rep(tiledec[None, :]).T, (width, width))
    return (jnp.asarray(dmask, F32), jnp.asarray(rep(rowdec), F32), jnp.asarray(rep(keydec), F32),
            jnp.asarray(tiledec_full, F32))


def _pool_part(z, wp_ref, scale_ref, o_ref, pool_ref, seq_tile):
    n_rows = POOL_PAD + TM
    pool_ref[0, POOL_PAD:, :] = z
    sums = []
    for level in range(len(POOL_WINDOWS)):
        lo, shift = F32_SUBLANES * (level + 1), 2 ** level
        s = pool_ref[level, lo:n_rows, :] + pool_ref[level, lo - shift:n_rows - shift, :]
        if level + 1 < len(POOL_WINDOWS):
            pool_ref[level + 1, lo:n_rows, :] = s
        sums.append(s[POOL_PAD - lo:])
    group = lax.broadcasted_iota(jnp.int32, (TM, BRANCH_WIDTH), 1) // (BRANCH_WIDTH // len(POOL_WINDOWS))
    win = sums[-1]
    for gi in range(len(POOL_WINDOWS) - 1):
        win = jnp.where(group == gi, sums[gi], win)
    t = seq_tile * TM + lax.broadcasted_iota(jnp.int32, (TM, BRANCH_WIDTH), 0)
    count = jnp.minimum(t + 1, jnp.left_shift(2, group)).astype(F32)
    pooled = win / count - z
    y = _dot(pooled.astype(BF16), wp_ref[...]) * scale_ref[...]
    o_ref[...] = y.astype(o_ref.dtype)
    pool_ref[0, POOL_PAD - POOL_HALO:POOL_PAD, :] = pool_ref[0, n_rows - POOL_HALO:n_rows, :]


def _rms_norm(x, g):
    return x * lax.rsqrt(jnp.mean(x * x, axis=-1, keepdims=True) + RMS_EPS) * g


def _mla_part(proj, cos, sin, qg_ref, kvg_ref, wq1_ref, wq2_ref, wk_ref, wvt_ref, qt_ref, k_ref, vt_ref):
    cq = _rms_norm(proj[:, :MLA_Q_RANK], qg_ref[...]).astype(BF16)
    ckv = _rms_norm(proj[:, MLA_Q_RANK:MLA_Q_RANK + MLA_KV_RANK], kvg_ref[...]).astype(BF16)
    k_raw = proj[:, MLA_Q_RANK + MLA_KV_RANK:]
    k_pe = pltpu.roll(k_raw * cos + _swap_halves(k_raw, MLA_ROPE // 2) * sin, MLA_NOPE, 1)
    ck = pltpu.roll(cos, MLA_NOPE, 1)
    sk = pltpu.roll(sin, MLA_NOPE, 1)
    lane = lax.broadcasted_iota(jnp.int32, ck.shape, 1)
    scale = MLA_QK ** -0.5 * math.log2(math.e)
    cq_tab = (ck + jnp.where(lane < MLA_NOPE, 1.0, 0.0)) * scale
    sq_tab = sk * scale
    tile4 = lambda a: jnp.concatenate([a] * MLA_HEADS, axis=1)
    q = _dot(cq, wq1_ref[...]) * tile4(cq_tab) + _dot(cq, wq2_ref[...]) * tile4(sq_tab)
    k = _dot(ckv, wk_ref[...]) + tile4(k_pe)
    for h in range(MLA_HEADS):
        rows = slice(h * HEAD_PAD, (h + 1) * HEAD_PAD)
        qt_ref[rows, :] = q[:, rows].T.astype(qt_ref.dtype)
    k_ref[...] = k.astype(k_ref.dtype)
    for t in range(TM // ATT_TK):
        vt_ref[t] = _dot_nt(wvt_ref[...], ckv[t * ATT_TK:(t + 1) * ATT_TK]).astype(vt_ref.dtype)


def _mixers_kernel(x_ref, w_ref, cs_ref, expand_ref,
                   lng_ref, lnb_ref, ws_ref, bias_ref, wp_ref, scale_ref,
                   qg_ref, kvg_ref, wq1_ref, wq2_ref, wk_ref, wvt_ref,
                   dmask_ref, rowdec_ref, keydec_ref, tiledec_ref,
                   ya_ref, yb_ref, yc_ref, qt_ref, k_ref, vt_ref, state_ref, pool_ref):
    @pl.when(pl.program_id(1) == 0)
    def _():
        state_ref[...] = jnp.zeros_like(state_ref)
        pool_ref[0, 0:POOL_PAD, :] = jnp.zeros((POOL_PAD, BRANCH_WIDTH), F32)

    values_per_tile = TM // ATT_TK
    for t in range(x_ref.shape[0] // TM):
        rows = slice(t * TM, (t + 1) * TM)
        _mixers_tile(x_ref.at[rows, :], w_ref, cs_ref.at[rows, :], expand_ref,
                     lng_ref, lnb_ref, ws_ref, bias_ref, wp_ref, scale_ref,
                     qg_ref, kvg_ref, wq1_ref, wq2_ref, wk_ref, wvt_ref,
                     dmask_ref, rowdec_ref, keydec_ref, tiledec_ref,
                     ya_ref.at[rows, :], yb_ref.at[rows, :], yc_ref.at[rows, :], qt_ref.at[t],
                     k_ref.at[rows, :], vt_ref.at[t * values_per_tile:(t + 1) * values_per_tile],
                     state_ref, pool_ref, pl.program_id(1) * (x_ref.shape[0] // TM) + t)


def _mixers_tile(x_ref, w_ref, cs_ref, expand_ref,
                 lng_ref, lnb_ref, ws_ref, bias_ref, wp_ref, scale_ref,
                 qg_ref, kvg_ref, wq1_ref, wq2_ref, wk_ref, wvt_ref,
                 dmask_ref, rowdec_ref, keydec_ref, tiledec_ref,
                 ya_ref, yb_ref, yc_ref, qt_ref, k_ref, vt_ref, state_ref, pool_ref, seq_tile):
    xb = x_ref[...].astype(BF16)
    p_sgu = _dot_nt(xb, w_ref[_OFF_SGU:_OFF_RET, :])
    p_pool = _dot_nt(xb, w_ref[_OFF_POOL:_OFF_CQ, :])
    p_mla = _dot_nt(xb, w_ref[_OFF_CQ:MIX_COLS, :])
    p_ret = _dot_nt(xb, w_ref[_OFF_RET:_OFF_POOL, :])
    cs = cs_ref[...]
    hi = cs.astype(BF16)
    rest = cs - hi.astype(F32)
    mid = rest.astype(BF16)
    lo = (rest - mid.astype(F32)).astype(BF16)
    rope = _dot(jnp.concatenate([hi, mid, lo], axis=1), expand_ref[...])
    ret_cos, ret_sin, mla_cos, mla_sin = (rope[:, t * HEAD_PAD:(t + 1) * HEAD_PAD] for t in range(4))
    _sgu_part(p_sgu, lng_ref, lnb_ref, ws_ref, bias_ref, ya_ref)
    _pool_part(p_pool, wp_ref, scale_ref, yc_ref, pool_ref, seq_tile)
    _mla_part(p_mla, mla_cos, mla_sin, qg_ref, kvg_ref, wq1_ref, wq2_ref, wk_ref, wvt_ref,
              qt_ref, k_ref, vt_ref)
    for r in range(TM // RET_TILE):
        rows = slice(r * RET_TILE, (r + 1) * RET_TILE)
        cos = jnp.concatenate([ret_cos[rows], ret_cos[rows]], axis=1)
        sin = jnp.concatenate([ret_sin[rows], ret_sin[rows]], axis=1)
        _ret_part(p_ret[rows], cos, sin, dmask_ref, rowdec_ref, keydec_ref, tiledec_ref,
                  yb_ref.at[rows, :], state_ref)


def _mixers(layer, x3, w_mix, rope, sgu_params, pool_params, mla_params, casts=()):
    bsz, seq, _ = x3.shape
    width = RET_HEADS * RET_KDIM
    ret_tables = _retention_tables()
    block = TM * MIX_TILES
    seq_tiles = seq // block
    tile = lambda last: pl.BlockSpec((None, block, last), lambda b, s: (b, s, 0))
    layered = tuple(sgu_params) + tuple(pool_params) + tuple(mla_params)
    in_specs = ([tile(D_MODEL), _const_spec(w_mix.shape), tile(HEAD_PAD), _const_spec(rope[1].shape)]
                + [_layer_spec(a, layer) for a in layered]
                + [_const_spec(a.shape) for a in ret_tables])
    cast_in, cast_out, cast_shapes = _cast_specs(casts, bsz * seq_tiles, lambda b, s: b * seq_tiles + s)
    branch = jax.ShapeDtypeStruct((bsz, seq, BRANCH_WIDTH), BF16)
    assert TM == ATT_TQ
    qt = jax.ShapeDtypeStruct((bsz, seq // TM, MLA_HEADS * HEAD_PAD, TM), BF16)
    kk = jax.ShapeDtypeStruct((bsz, seq, MLA_HEADS * HEAD_PAD), BF16)
    vt = jax.ShapeDtypeStruct((bsz, seq // ATT_TK, MLA_HEADS * MLA_V, ATT_TK), BF16)
    return pl.pallas_call(
        _with_casts(_mixers_kernel, len(in_specs), 6, casts, bsz * seq_tiles),
        grid=(bsz, seq_tiles),
        in_specs=in_specs + cast_in,
        out_specs=[tile(BRANCH_WIDTH)] * 3
                  + [pl.BlockSpec((None, MIX_TILES, MLA_HEADS * HEAD_PAD, TM), lambda b, s: (b, s, 0, 0)),
                     tile(MLA_HEADS * HEAD_PAD),
                     pl.BlockSpec((None, block // ATT_TK, MLA_HEADS * MLA_V, ATT_TK), lambda b, s: (b, s, 0, 0))]
                  + cast_out,
        out_shape=[branch, branch, branch, qt, kk, vt] + cast_shapes,
        scratch_shapes=[pltpu.VMEM((width, width), F32),
                        pltpu.VMEM((len(POOL_WINDOWS), POOL_PAD + TM, BRANCH_WIDTH), F32)],
        compiler_params=_params(2, "arbitrary"),
        name="token_mixers",
    )(x3, w_mix, *rope, *sgu_params, *pool_params, *mla_params, *ret_tables, *_cast_operands(casts))


def _attn_kernel(qt_ref, k_ref, vt_ref, o_ref, st_ref, m_ref, acc_ref):
    key_chunk = lax.broadcasted_iota(jnp.int32, (ATT_TK, ATT_TQ), 0) // CHUNK
    qry_chunk = lax.broadcasted_iota(jnp.int32, (ATT_TK, ATT_TQ), 1) // CHUNK
    diag_masks = [key_chunk + t * (ATT_TK // CHUNK) <= qry_chunk for t in range(ATT_TQ // ATT_TK)]
    heads = range(MLA_HEADS)
    every_query = slice(0, ATT_TQ)
    upper = slice(ATT_TK, ATT_TQ)
    ones = jnp.ones((ATT_ONES, ATT_TK), BF16)
    n_query_tiles = qt_ref.shape[0]

    def scores(qi, j, slot, queries=every_query, hs=heads):
        rows = pl.ds(pl.multiple_of(j * ATT_TK, ATT_TK), ATT_TK)
        for h in hs:
            cols = slice(h * HEAD_PAD, (h + 1) * HEAD_PAD)
            st_ref[slot, h, :, queries] = _dot(k_ref[rows, cols], qt_ref[qi, cols, queries])

    def softmax_pv(j, slot, mask=None, queries=every_query, hs=heads):
        for h in hs:
            load = lambda: (st_ref[slot, h, :, queries] if mask is None
                            else jnp.where(mask[:, queries], st_ref[slot, h, :, queries], -jnp.inf))
            m = m_ref[h, :, queries]
            m_new = jnp.maximum(m, jnp.max(load(), axis=0, keepdims=True))
            p = jnp.exp2((load() - m_new).astype(BF16))
            lhs = jnp.concatenate([vt_ref[j, h * MLA_V:(h + 1) * MLA_V, :], ones], axis=0)
            acc_ref[h, :, queries] = jnp.exp2(m - m_new) * acc_ref[h, :, queries] + _dot(lhs, p)
            m_ref[h, :, queries] = m_new

    def query_tile(qi, carry):
        def pair(i, carry):
            for h in heads:
                scores(qi, 2 * i + 1, 1, hs=(h,))
                softmax_pv(2 * i, 0, hs=(h,))
            for h in heads:
                scores(qi, 2 * i + 2, 0, hs=(h,))
                softmax_pv(2 * i + 1, 1, hs=(h,))
            return carry

        m_ref[...] = jnp.full(m_ref.shape, -jnp.inf, F32)
        acc_ref[...] = jnp.zeros(acc_ref.shape, F32)
        lax.fori_loop(0, qi, pair, 0)
        for h in heads:
            scores(qi, 2 * qi + 1, 1, upper, hs=(h,))
            softmax_pv(2 * qi, 0, diag_masks[0], hs=(h,))
        softmax_pv(2 * qi + 1, 1, diag_masks[1], upper)
        scores(jnp.minimum(qi + 1, n_query_tiles - 1), 0, 0)
        out_t = jnp.concatenate([acc_ref[h, :MLA_V] / acc_ref[h, MLA_V:MLA_V + 1] for h in heads], axis=0)
        rows = pl.ds(pl.multiple_of(qi * ATT_TQ, ATT_TQ), ATT_TQ)
        o_ref[rows, :] = out_t.T.astype(o_ref.dtype)
        return carry

    scores(0, 0, 0)
    lax.fori_loop(0, n_query_tiles, query_tile, 0)


def _attention(qt4, k3, vt4):
    bsz, seq, _ = k3.shape
    whole = lambda a: pl.BlockSpec((None,) + a.shape[1:], lambda b: (b,) + (0,) * (a.ndim - 1))
    return pl.pallas_call(
        _attn_kernel,
        grid=(bsz,),
        in_specs=[whole(qt4), whole(k3), whole(vt4)],
        out_specs=pl.BlockSpec((None, seq, MLA_HEADS * MLA_V), lambda b: (b, 0, 0)),
        out_shape=jax.ShapeDtypeStruct((bsz, seq, MLA_HEADS * MLA_V), BF16),
        scratch_shapes=[pltpu.VMEM((2, MLA_HEADS, ATT_TK, ATT_TQ), F32),
                        pltpu.VMEM((MLA_HEADS, 1, ATT_TQ), F32),
                        pltpu.VMEM((MLA_HEADS, MLA_V + ATT_ONES, ATT_TQ), F32)],
        compiler_params=_params(1),
        name="mla_attention",
    )(qt4, k3, vt4)


def _merge_kernel(x_ref, ya_ref, yb_ref, yc_ref, yd_ref, wg_ref, wb_ref, wo_ref, g_ref, b_ref, o_ref):
    def residual(rows):
        x = x_ref[rows, :]
        xb = x.astype(BF16)
        merged = None
        logits = _dot(xb, wg_ref[...])
        for n, y_ref in enumerate((ya_ref, yb_ref, yc_ref, yd_ref)):
            gate = jax.nn.sigmoid(logits[:, n * D_MODEL:(n + 1) * D_MODEL])
            term = gate * _dot(y_ref[rows, :], wb_ref[n * BRANCH_WIDTH:(n + 1) * BRANCH_WIDTH, :])
            merged = term if merged is None else merged + term
        return ALPHA * x + _dot(merged.astype(BF16), wo_ref[...])

    _pipelined_rows(o_ref, g_ref, b_ref, residual)


def _merge(layer, x, ys, wg, wb, wo, g, b, casts=()):
    n = x.shape[0]
    steps = n // TM_WIDE
    row = lambda last: pl.BlockSpec((TM_WIDE, last), lambda i: (i, 0))
    cast_in, cast_out, cast_shapes = _cast_specs(casts, steps, lambda i: i)
    return pl.pallas_call(
        _with_casts(_merge_kernel, 10, 1, casts, steps),
        grid=(steps,),
        in_specs=[row(D_MODEL)] + [row(BRANCH_WIDTH)] * N_BRANCH
                 + [_const_spec(a.shape) for a in (wg, wb, wo)]
                 + [_layer_spec(a, layer) for a in (g, b)] + cast_in,
        out_specs=[row(D_MODEL)] + cast_out,
        out_shape=[jax.ShapeDtypeStruct((n, D_MODEL), F32)] + cast_shapes,
        compiler_params=_params(1),
        name="merge_ln",
    )(x, *ys, wg, wb, wo, g, b, *_cast_operands(casts))


def _rope_tables(positions):
    n_r, n_m = RET_KDIM // 2, MLA_ROPE // 2
    inv_r = ROPE_BASE ** (-jnp.arange(0, RET_KDIM, 2, dtype=F32) / RET_KDIM)
    inv_m = ROPE_BASE ** (-jnp.arange(0, MLA_ROPE, 2, dtype=F32) / MLA_ROPE)
    pad = jnp.zeros((HEAD_PAD - 2 * (n_r + n_m),), F32)
    ang = positions.astype(F32)[..., None] * jnp.concatenate([inv_r, inv_r, inv_m, inv_m, pad])
    lane = np.arange(HEAD_PAD)
    is_cos = jnp.asarray((lane < n_r) | ((lane >= 2 * n_r) & (lane < 2 * n_r + n_m)))
    compact = jnp.where(is_cos, jnp.cos(ang), jnp.sin(ang))
    expand = np.zeros((HEAD_PAD, 4 * HEAD_PAD), np.float32)
    for l in range(HEAD_PAD):
        expand[l % n_r, l] = 1.0
        expand[n_r + l % n_r, HEAD_PAD + l] = -1.0 if l % RET_KDIM < n_r else 1.0
        if l < MLA_ROPE:
            expand[2 * n_r + l % n_m, 2 * HEAD_PAD + l] = 1.0
            expand[2 * n_r + n_m + l % n_m, 3 * HEAD_PAD + l] = -1.0 if l < n_m else 1.0
    return compact, jnp.asarray(np.tile(expand, (3, 1)), BF16)


def _swap_cols(w):
    half = w.shape[-1] // 2
    return jnp.concatenate([w[..., half:], w[..., :half]], axis=-1)


def _mla_weights(w_uq, w_ukv):
    depth = w_uq.shape[0]
    uq = w_uq.reshape(depth, MLA_Q_RANK, MLA_HEADS, MLA_QK)
    q_nope, q_rope = uq[..., :MLA_NOPE], uq[..., MLA_NOPE:]
    zq = jnp.zeros((depth, MLA_Q_RANK, MLA_HEADS, HEAD_PAD - MLA_QK), w_uq.dtype)
    wq1 = jnp.concatenate([q_nope, q_rope, zq], axis=-1)
    wq2 = jnp.concatenate([jnp.zeros_like(q_nope), _swap_cols(q_rope), zq], axis=-1)
    ukv = w_ukv.reshape(depth, MLA_KV_RANK, MLA_HEADS, MLA_NOPE + MLA_V)
    k_nope, v = ukv[..., :MLA_NOPE], ukv[..., MLA_NOPE:]
    wk = jnp.concatenate([k_nope, jnp.zeros_like(k_nope)], axis=-1)
    wvt = jnp.swapaxes(v.reshape(depth, MLA_KV_RANK, MLA_HEADS * MLA_V), 1, 2)
    flat = lambda a: a.reshape(depth, a.shape[1], MLA_HEADS * HEAD_PAD).astype(BF16)
    return flat(wq1), flat(wq2), flat(wk), wvt.astype(BF16)


def kernel(x, p, positions, ffn1_up, ffn1_down, ln1_g, ln1_b, w_in, sgu_ln_g, sgu_ln_b, sgu_w, sgu_b,
           pool_w, pool_scale, mla_q_norm, mla_kv_norm, mla_w_uq, mla_w_ukv, w_branch, w_out,
           ln2_g, ln2_b, ffn2_up, ffn2_down, w_ple_gate, w_ple, ln3_g, ln3_b):
    bsz, seq, dm = x.shape
    n = bsz * seq
    rope = _rope_tables(positions)
    bf = lambda a: a.astype(BF16)
    rows = lambda a: a[:, None, :]

    up1, down1 = bf(ffn1_up[0]), bf(ffn1_down[0])
    w_in_t = jnp.swapaxes(w_in, 1, 2)
    w_branch2 = w_branch.reshape(DEPTH, N_BRANCH * BRANCH_WIDTH, dm)
    wp = bf(w_ple)
    wq1, wq2, wk, wvt = _mla_weights(mla_w_uq, mla_w_ukv)
    sgu_bias = jnp.repeat(jnp.swapaxes(sgu_b, 1, 2), SGU_WIDTH // SGU_GROUPS, axis=2)
    groups = len(POOL_WINDOWS)
    pool_bd = bf(jnp.einsum("lgcd,gh->lgchd", pool_w, jnp.eye(groups, dtype=pool_w.dtype))
                 .reshape(DEPTH, BRANCH_WIDTH, BRANCH_WIDTH))
    p2 = p.reshape(DEPTH, n, PLE_DIM)

    h = x.reshape(n, dm)
    for i in range(DEPTH):
        last = i + 1 == DEPTH
        h, w_mix, w_gate, wb, wo = _ffn_ln(
            i, h, up1, down1, rows(ln1_g), rows(ln1_b),
            (_cast(w_in_t, i, MIX_COLS), _CastT(w_in_t, i, _OFF_GATE, N_BRANCH * dm), _cast(w_branch2, i),
             _cast(w_out, i)))
        y_a, y_b, y_c, q, k, vt, up2, wpg, *nxt_up = _mixers(
            i, h.reshape(bsz, seq, dm), w_mix, rope,
            (rows(sgu_ln_g), rows(sgu_ln_b), sgu_w, sgu_bias), (pool_bd, rows(pool_scale)),
            (rows(mla_q_norm), rows(mla_kv_norm), wq1, wq2, wk, wvt),
            (_cast(ffn2_up, i), _cast(w_ple_gate, i)) + (() if last else (_cast(ffn1_up, i + 1),)))
        y_d = _attention(q, k, vt)
        ys = tuple(y.reshape(n, BRANCH_WIDTH) for y in (y_a, y_b, y_c, y_d))
        h, down2, *nxt_down = _merge(
            i, h, ys, w_gate, wb, wo, rows(ln2_g), rows(ln2_b),
            (_cast(ffn2_down, i),) + (() if last else (_cast(ffn1_down, i + 1),)))
        h = _ffn_ple_ln(i, h, p2, up2, down2, wpg, wp, rows(ln3_g), rows(ln3_b))
        if not last:
            (up1,), (down1,) = nxt_up, nxt_down
    return h.reshape(bsz, seq, dm)
```
